```python
import jax, jax.numpy as jnp
from jax import lax
import numpy as np

D_MODEL = 4096
BATCH = 2
SEQ = 8192
DEPTH = 1

EPS = 1e-6
ROPE_THETA = 10000.0

MLA_HEADS = 16
MLA_Q_LORA = 1024
MLA_KV_LORA = 512
MLA_NOPE_DIM = 128
MLA_ROPE_DIM = 64
MLA_QK_DIM = MLA_NOPE_DIM + MLA_ROPE_DIM
MLA_V_DIM = 128
MLA_WIDTH = MLA_HEADS * MLA_V_DIM
Q_BLK = 128

GLA_HEADS = 4
GLA_V_WIDTH = D_MODEL - MLA_WIDTH
GLA_K_WIDTH = GLA_V_WIDTH // 2
GLA_HEAD_K = GLA_K_WIDTH // GLA_HEADS
GLA_HEAD_V = GLA_V_WIDTH // GLA_HEADS
GLA_GATE_RANK = 16
GLA_GATE_TAU = 16.0
GLA_CHUNK = 64

MIX_WIDTH = MLA_WIDTH + GLA_V_WIDTH
IN_SIZES = (MLA_Q_LORA, MLA_KV_LORA, MLA_ROPE_DIM, GLA_K_WIDTH, GLA_K_WIDTH, GLA_V_WIDTH, GLA_GATE_RANK, GLA_V_WIDTH)
IN_WIDTH = int(sum(IN_SIZES))
IN_SPLIT_POINTS = tuple(int(c) for c in np.cumsum(IN_SIZES)[:-1])

MEM_TOKENS = 256
XATTN_HEADS = 4
XATTN_HEAD_DIM = 256
XATTN_WIDTH = XATTN_HEADS * XATTN_HEAD_DIM

PEER_HEADS = 8
PEER_N_KEYS = 128
PEER_N_EXPERTS = PEER_N_KEYS * PEER_N_KEYS
PEER_QUERY_DIM = 256
PEER_HALF = PEER_QUERY_DIM // 2
PEER_TOPK = 16
PEER_TOK_BLK = 64

kernel_name = 'hybrid_mla_gla_peer_block'


def rms_norm(x, g):
    xf = x.astype(jnp.float32)
    y = xf * lax.rsqrt(jnp.mean(xf * xf, axis=-1, keepdims=True) + EPS)
    return (y * g.astype(jnp.float32)).astype(x.dtype)


def apply_rope(x, positions):
    d = x.shape[-1]
    inv_freq = ROPE_THETA ** (-jnp.arange(0, d, 2, dtype=jnp.float32) / d)
    ang = positions.astype(jnp.float32)[:, :, None] * inv_freq
    cos = jnp.cos(ang)[:, :, None, :]
    sin = jnp.sin(ang)[:, :, None, :]
    x1, x2 = jnp.split(x.astype(jnp.float32), 2, axis=-1)
    return jnp.concatenate([x1 * cos - x2 * sin, x2 * cos + x1 * sin], axis=-1).astype(x.dtype)


def causal_block_attention(q, k, v, scale):
    seq = q.shape[1]
    outs = []
    for blk in range(seq // Q_BLK):
        lo, hi = blk * Q_BLK, (blk + 1) * Q_BLK
        s = jnp.einsum('bqhd,bkhd->bhqk', q[:, lo:hi], k[:, :hi]).astype(jnp.float32) * scale
        mask = jnp.arange(hi)[None, :] <= jnp.arange(lo, hi)[:, None]
        s = jnp.where(mask, s, -jnp.inf)
        p = jax.nn.softmax(s, axis=-1).astype(v.dtype)
        outs.append(jnp.einsum('bhqk,bkhd->bqhd', p, v[:, :hi]))
    return jnp.concatenate(outs, axis=1)


def gla_chunked(q, k, v, g):
    B, S, H, dk = q.shape
    dv = v.shape[-1]
    C = GLA_CHUNK
    N = S // C

    def to_chunks(t):
        return t.reshape(B, N, C, H, t.shape[-1]).transpose(0, 3, 1, 2, 4)

    q, k, v, g = to_chunks(q), to_chunks(k), to_chunks(v), to_chunks(g)
    b = jnp.cumsum(g, axis=3)
    b_last = b[:, :, :, -1:, :]
    qe = q * (dk ** -0.5) * jnp.exp(b)
    ke = k * jnp.exp(-b)
    kd = k * jnp.exp(b_last - b)
    causal = jnp.tril(jnp.ones((C, C), dtype=bool))
    att = jnp.where(causal, jnp.einsum('bhncd,bhnjd->bhncj', qe, ke), 0.0)
    o_intra = jnp.einsum('bhncj,bhnjv->bhncv', att, v)

    def step(state, inp):
        qe_n, kd_n, v_n, dec_n = inp
        o_n = jnp.einsum('bhcd,bhdv->bhcv', qe_n, state)
        state = state * dec_n[..., None] + jnp.einsum('bhcd,bhcv->bhdv', kd_n, v_n)
        return state, o_n

    xs = (jnp.moveaxis(qe, 2, 0), jnp.moveaxis(kd, 2, 0), jnp.moveaxis(v, 2, 0),
          jnp.moveaxis(jnp.exp(b_last[:, :, :, 0, :]), 2, 0))
    state0 = jnp.zeros((B, H, dk, dv), jnp.float32)
    _, o_inter = lax.scan(step, state0, xs)
    o = o_intra + jnp.moveaxis(o_inter, 0, 2)
    return o.transpose(0, 2, 3, 1, 4).reshape(B, S, H, dv)


def hybrid_mixer(xn, positions, w_in, mla_q_norm, w_uq, mla_kv_norm, w_ukv, mla_out_norm,
                 w_gate_up, b_gate, gla_out_norm, w_out):
    B, S, _ = xn.shape
    h = xn @ w_in
    c_q, c_kv, k_r, g_q, g_k, g_v, g_lr, g_og = jnp.split(h, IN_SPLIT_POINTS, axis=-1)

    q = (rms_norm(c_q, mla_q_norm) @ w_uq).reshape(B, S, MLA_HEADS, MLA_QK_DIM)
    q_full = jnp.concatenate([q[..., :MLA_NOPE_DIM], apply_rope(q[..., MLA_NOPE_DIM:], positions)], axis=-1)
    kv = (rms_norm(c_kv, mla_kv_norm) @ w_ukv).reshape(B, S, MLA_HEADS, MLA_NOPE_DIM + MLA_V_DIM)
    k_nope, v_mla = kv[..., :MLA_NOPE_DIM], kv[..., MLA_NOPE_DIM:]
    k_rope = apply_rope(k_r[:, :, None, :], positions)
    k_full = jnp.concatenate([k_nope, jnp.broadcast_to(k_rope, (B, S, MLA_HEADS, MLA_ROPE_DIM))], axis=-1)
    o_mla = causal_block_attention(q_full, k_full, v_mla, MLA_QK_DIM ** -0.5).reshape(B, S, MLA_WIDTH)
    o_mla = rms_norm(o_mla, mla_out_norm)

    log_a = jax.nn.log_sigmoid((g_lr @ w_gate_up + b_gate).astype(jnp.float32)) / GLA_GATE_TAU
    f32 = jnp.float32
    o_gla = gla_chunked(g_q.astype(f32).reshape(B, S, GLA_HEADS, GLA_HEAD_K),
                        g_k.astype(f32).reshape(B, S, GLA_HEADS, GLA_HEAD_K),
                        g_v.astype(f32).reshape(B, S, GLA_HEADS, GLA_HEAD_V),
                        log_a.reshape(B, S, GLA_HEADS, GLA_HEAD_K))
    o_gla = rms_norm(o_gla, gla_out_norm).reshape(B, S, GLA_V_WIDTH) * jax.nn.silu(g_og.astype(f32))

    mixed = jnp.concatenate([o_mla, o_gla.astype(xn.dtype)], axis=-1)
    return mixed @ w_out


def memory_cross_attention(xn, mn, w_cq, w_ck, w_cv, w_co):
    B, S, _ = xn.shape
    M = mn.shape[1]
    q = (xn @ w_cq).reshape(B, S, XATTN_HEADS, XATTN_HEAD_DIM)
    k = (mn @ w_ck).reshape(B, M, XATTN_HEADS, XATTN_HEAD_DIM)
    v = (mn @ w_cv).reshape(B, M, XATTN_HEADS, XATTN_HEAD_DIM)
    s = jnp.einsum('bshd,bmhd->bhsm', q, k).astype(jnp.float32) * (XATTN_HEAD_DIM ** -0.5)
    p = jax.nn.softmax(s, axis=-1).astype(v.dtype)
    o = jnp.einsum('bhsm,bmhd->bshd', p, v).reshape(B, S, XATTN_WIDTH)
    return o @ w_co


def peer_ffn(xn, w_peer_q, peer_sub_keys, peer_u, peer_v):
    B, S, D = xn.shape
    q = (xn @ w_peer_q).reshape(B, S, PEER_HEADS, 2, PEER_HALF)
    s = jnp.einsum('bshpd,hpnd->bshpn', q, peer_sub_keys).astype(jnp.float32)
    top_s, top_i = lax.top_k(s, PEER_TOPK)
    cand = top_s[..., 0, :, None] + top_s[..., 1, None, :]
    cand_s, cand_i = lax.top_k(cand.reshape(B, S, PEER_HEADS, PEER_TOPK * PEER_TOPK), PEER_TOPK)
    i1 = jnp.take_along_axis(top_i[..., 0, :], cand_i // PEER_TOPK, axis=-1)
    i2 = jnp.take_along_axis(top_i[..., 1, :], cand_i % PEER_TOPK, axis=-1)
    expert = i1 * PEER_N_KEYS + i2
    gate = jax.nn.softmax(cand_s, axis=-1).astype(xn.dtype)

    n_blk = (B * S) // PEER_TOK_BLK
    xb = xn.reshape(n_blk, PEER_TOK_BLK, D)
    eb = expert.reshape(n_blk, PEER_TOK_BLK, PEER_HEADS * PEER_TOPK)
    gb = gate.reshape(n_blk, PEER_TOK_BLK, PEER_HEADS * PEER_TOPK)

    def token_block(args):
        xt, et, gt = args
        a = jax.nn.gelu(jnp.einsum('td,ted->te', xt, peer_u[et]), approximate=False) * gt
        return jnp.einsum('te,ted->td', a, peer_v[et])

    return lax.map(token_block, (xb, eb, gb)).reshape(B, S, D)


def setup_inputs(seed: int = 0) -> dict:
    key = jax.random.key(seed)
    ks = jax.random.split(key, 32)
    f32 = jnp.float32
    L = DEPTH

    def normal(k, shape, scale):
        return jax.random.normal(k, shape, f32) * scale

    def gain(k, shape):
        return 1.0 + 0.01 * jax.random.normal(k, shape, f32)

    return {
        'x': normal(ks[0], (BATCH, SEQ, D_MODEL), 1.0),
        'mem': normal(ks[1], (BATCH, MEM_TOKENS, D_MODEL), 1.0),
        'positions': jnp.arange(SEQ, dtype=jnp.int32)[None, :]
                     + jax.random.randint(ks[2], (BATCH, 1), 0, 4096, dtype=jnp.int32),
        'norm_mem': gain(ks[3], (D_MODEL,)),
        'norm_mix': gain(ks[4], (L, D_MODEL)),
        'w_in': normal(ks[5], (L, D_MODEL, IN_WIDTH), D_MODEL ** -0.5),
        'mla_q_norm': gain(ks[6], (L, MLA_Q_LORA)),
        'w_uq': normal(ks[7], (L, MLA_Q_LORA, MLA_HEADS * MLA_QK_DIM), MLA_Q_LORA ** -0.5),
        'mla_kv_norm': gain(ks[8], (L, MLA_KV_LORA)),
        'w_ukv': normal(ks[9], (L, MLA_KV_LORA, MLA_HEADS * (MLA_NOPE_DIM + MLA_V_DIM)), MLA_KV_LORA ** -0.5),
        'mla_out_norm': gain(ks[10], (L, MLA_WIDTH)),
        'w_gate_up': normal(ks[11], (L, GLA_GATE_RANK, GLA_K_WIDTH), GLA_GATE_RANK ** -0.5),
        'b_gate': normal(ks[12], (L, GLA_K_WIDTH), 0.1),
        'gla_out_norm': gain(ks[13], (L, GLA_HEAD_V)),
        'w_out': normal(ks[14], (L, MIX_WIDTH, D_MODEL), MIX_WIDTH ** -0.5),
        'norm_cross': gain(ks[15], (L, D_MODEL)),
        'w_cq': normal(ks[16], (L, D_MODEL, XATTN_WIDTH), D_MODEL ** -0.5),
        'w_ck': normal(ks[17], (L, D_MODEL, XATTN_WIDTH), D_MODEL ** -0.5),
        'w_cv': normal(ks[18], (L, D_MODEL, XATTN_WIDTH), D_MODEL ** -0.5),
        'w_co': normal(ks[19], (L, XATTN_WIDTH, D_MODEL), XATTN_WIDTH ** -0.5),
        'norm_ffn': gain(ks[20], (L, D_MODEL)),
        'w_peer_q': normal(ks[21], (L, D_MODEL, PEER_HEADS * PEER_QUERY_DIM), D_MODEL ** -0.5),
        'peer_sub_keys': normal(ks[22], (L, PEER_HEADS, 2, PEER_N_KEYS, PEER_HALF), PEER_HALF ** -0.5),
        'peer_u': normal(ks[23], (L, PEER_N_EXPERTS, D_MODEL), D_MODEL ** -0.5),
        'peer_v': normal(ks[24], (L, PEER_N_EXPERTS, D_MODEL), PEER_HEADS ** -0.5),
        'norm_final': gain(ks[25], (D_MODEL,)),
    }


def reference(x, mem, positions, norm_mem, norm_mix, w_in, mla_q_norm, w_uq, mla_kv_norm, w_ukv,
              mla_out_norm, w_gate_up, b_gate, gla_out_norm, w_out, norm_cross, w_cq, w_ck, w_cv, w_co,
              norm_ffn, w_peer_q, peer_sub_keys, peer_u, peer_v, norm_final):
    mn = rms_norm(mem, norm_mem)
    h = x
    for l in range(DEPTH):
        h = h + hybrid_mixer(rms_norm(h, norm_mix[l]), positions, w_in[l], mla_q_norm[l], w_uq[l],
                             mla_kv_norm[l], w_ukv[l], mla_out_norm[l], w_gate_up[l], b_gate[l],
                             gla_out_norm[l], w_out[l])
        h = h + memory_cross_attention(rms_norm(h, norm_cross[l]), mn, w_cq[l], w_ck[l], w_cv[l], w_co[l])
        h = h + peer_ffn(rms_norm(h, norm_ffn[l]), w_peer_q[l], peer_sub_keys[l], peer_u[l], peer_v[l])
    return rms_norm(h, norm_final)
```

```python
import functools
import math

import numpy as np
import jax
import jax.numpy as jnp
from jax import lax
from jax.experimental import pallas as pl
from jax.experimental.pallas import tpu as pltpu

F32 = jnp.float32
BF16 = jnp.bfloat16
I32 = jnp.int32

EPS = 1e-6
ROPE_THETA = 10000.0

MLA_HEADS = 16
MLA_Q_LORA = 1024
MLA_KV_LORA = 512
MLA_NOPE = 128
MLA_ROPE = 64
MLA_QK = MLA_NOPE + MLA_ROPE
MLA_V = 128
MLA_WIDTH = MLA_HEADS * MLA_V

GLA_HEADS = 4
GLA_HEAD_K = 256
GLA_HEAD_V = 512
GLA_K_WIDTH = GLA_HEADS * GLA_HEAD_K
GLA_V_WIDTH = GLA_HEADS * GLA_HEAD_V
GLA_RANK = 16
GLA_TAU = 16.0
GLA_CHUNK = 64

XATTN_HEADS = 4
XATTN_DIM = 256
XATTN_WIDTH = XATTN_HEADS * XATTN_DIM

PEER_HEADS = 8
PEER_KEYS = 128
PEER_HALF = 128
PEER_TOPK = 16
PEER_SLOTS = PEER_HEADS * PEER_TOPK
PEER_EXPERTS = PEER_KEYS * PEER_KEYS

LANES = 128
VMEM_LIMIT = 56 * 1024 * 1024

COL_CQ = 0
COL_CKV = COL_CQ + MLA_Q_LORA
COL_GQ = COL_CKV + MLA_KV_LORA
COL_GK = COL_GQ + GLA_K_WIDTH
COL_GV = COL_GK + GLA_K_WIDTH
COL_OG = COL_GV + GLA_V_WIDTH
COL_KR = COL_OG + GLA_V_WIDTH
COL_LR = COL_KR + LANES
IN_PAD = 8192

NEG = -1e30


def _cparams(*sem):
    return pltpu.CompilerParams(dimension_semantics=sem, vmem_limit_bytes=VMEM_LIMIT)


def _rms(x, g):
    ms = jnp.mean(x * x, axis=-1, keepdims=True)
    return x * lax.rsqrt(ms + EPS) * g


def _dot(a, b):
    return jnp.dot(a, b, preferred_element_type=F32)


def _dot_nt(a, b):
    return lax.dot_general(a, b, (((1,), (1,)), ((), ())), preferred_element_type=F32)


def _dot_tn(a, b):
    return lax.dot_general(a, b, (((0,), (0,)), ((), ())), preferred_element_type=F32)


def _norm_matmul_kernel(x_ref, g_ref, w_ref, o_ref, *rest, emit_xn):
    if emit_xn:
        xn_out_ref, xn_ref = rest
    else:
        (xn_ref,) = rest

    @pl.when(pl.program_id(1) == 0)
    def _():
        xn = _rms(x_ref[...].astype(F32), g_ref[...]).astype(BF16)
        xn_ref[...] = xn
        if emit_xn:
            xn_out_ref[...] = xn

    o_ref[...] = _dot(xn_ref[...], w_ref[...]).astype(o_ref.dtype)


def norm_matmul(x, gain, w, *, k, x_col_blk=0, tm, tn, out_dtype, emit_xn=False, name):
    m = x.shape[0]
    n = w.shape[1]
    tm = min(tm, m)
    tn = min(tn, n)
    out_shape = [jax.ShapeDtypeStruct((m, n), out_dtype)]
    out_specs = [pl.BlockSpec((tm, tn), lambda i, j: (i, j))]
    if emit_xn:
        out_shape.append(jax.ShapeDtypeStruct((m, k), BF16))
        out_specs.append(pl.BlockSpec((tm, k), lambda i, j: (i, 0)))
    res = pl.pallas_call(
        functools.partial(_norm_matmul_kernel, emit_xn=emit_xn),
        out_shape=out_shape,
        grid=(m // tm, n // tn),
        in_specs=[pl.BlockSpec((tm, k), lambda i, j: (i, x_col_blk)),
                  pl.BlockSpec((1, k), lambda i, j: (0, 0)),
                  pl.BlockSpec((k, tn), lambda i, j: (0, j))],
        out_specs=out_specs,
        scratch_shapes=[pltpu.VMEM((tm, k), BF16)],
        compiler_params=_cparams("parallel", "arbitrary"),
        name=name,
    )(x, gain.reshape(1, k).astype(F32), w)
    return res if emit_xn else res[0]


def _rope_pair(r, cosv, sinv):
    return r * cosv + pltpu.roll(r, 2 * 32, axis=1) * sinv


def _qproj_kernel(c_ref, g_ref, w_ref, cos_ref, sin_ref, o_ref, *, scale):
    xn = _rms(c_ref[...].astype(F32), g_ref[...]).astype(BF16)
    q = _dot(xn, w_ref[...])
    cosv = cos_ref[...]
    sinv = sin_ref[...]
    lane = lax.broadcasted_iota(I32, (1, LANES), 1)
    rope_base = MLA_HEADS * MLA_NOPE
    for j in range(MLA_HEADS // 2):
        r = q[:, rope_base + j * LANES: rope_base + (j + 1) * LANES]
        r = _rope_pair(r, cosv, sinv) * scale
        for p in range(2):
            h = 2 * j + p
            own = ((lane // 32) % 2) == p
            o_ref[:, h * 256: h * 256 + 128] = (q[:, h * 128:(h + 1) * 128] * scale).astype(o_ref.dtype)
            o_ref[:, h * 256 + 128: h * 256 + 256] = jnp.where(own, r, 0.0).astype(o_ref.dtype)


def mla_q_proj(h_in, gain, w_uq_perm, cos_t, sin_t, *, tm):
    m = h_in.shape[0]
    tm = min(tm, m)
    return pl.pallas_call(
        functools.partial(_qproj_kernel, scale=MLA_QK ** -0.5),
        out_shape=jax.ShapeDtypeStruct((m, MLA_HEADS * 256), BF16),
        grid=(m // tm,),
        in_specs=[pl.BlockSpec((tm, MLA_Q_LORA), lambda i: (i, COL_CQ // MLA_Q_LORA)),
                  pl.BlockSpec((1, MLA_Q_LORA), lambda i: (0, 0)),
                  pl.BlockSpec(w_uq_perm.shape, lambda i: (0, 0)),
                  pl.BlockSpec((tm, LANES), lambda i: (i, 0)),
                  pl.BlockSpec((tm, LANES), lambda i: (i, 0))],
        out_specs=pl.BlockSpec((tm, MLA_HEADS * 256), lambda i: (i, 0)),
        compiler_params=_cparams("parallel"),
        name="mla_q_proj",
    )(h_in, gain.reshape(1, -1).astype(F32), w_uq_perm, cos_t, sin_t)


def _kvproj_kernel(c_ref, g_ref, w_ref, kr_ref, cos_ref, sin_ref, kv_ref, kro_ref):
    xn = _rms(c_ref[...].astype(F32), g_ref[...]).astype(BF16)
    kv_ref[...] = _dot(xn, w_ref[...]).astype(kv_ref.dtype)
    kro_ref[...] = _rope_pair(kr_ref[...].astype(F32), cos_ref[...], sin_ref[...]).astype(kro_ref.dtype)


def mla_kv_proj(h_in, gain, w_ukv, cos_t, sin_t, *, tm):
    m = h_in.shape[0]
    tm = min(tm, m)
    n = w_ukv.shape[1]
    return pl.pallas_call(
        _kvproj_kernel,
        out_shape=[jax.ShapeDtypeStruct((m, n), BF16), jax.ShapeDtypeStruct((m, LANES), BF16)],
        grid=(m // tm,),
        in_specs=[pl.BlockSpec((tm, MLA_KV_LORA), lambda i: (i, COL_CKV // MLA_KV_LORA)),
                  pl.BlockSpec((1, MLA_KV_LORA), lambda i: (0, 0)),
                  pl.BlockSpec(w_ukv.shape, lambda i: (0, 0)),
                  pl.BlockSpec((tm, LANES), lambda i: (i, COL_KR // LANES)),
                  pl.BlockSpec((tm, LANES), lambda i: (i, 0)),
                  pl.BlockSpec((tm, LANES), lambda i: (i, 0))],
        out_specs=[pl.BlockSpec((tm, n), lambda i: (i, 0)),
                   pl.BlockSpec((tm, LANES), lambda i: (i, 0))],
        compiler_params=_cparams("parallel"),
        name="mla_kv_proj",
    )(h_in, gain.reshape(1, -1).astype(F32), w_ukv, h_in, cos_t, sin_t)


def _mla_attn_kernel(q_ref, kn_ref, kr_ref, v_ref, o_ref, *, blk):
    qi = pl.program_id(2)
    q = q_ref[...]

    def step(j, carry, masked):
        m, l, acc = carry
        start = pl.multiple_of(j * blk, blk)
        k = jnp.concatenate([kn_ref[pl.ds(start, blk), :], kr_ref[pl.ds(start, blk), :]], axis=1)
        s = _dot_nt(q, k)
        if masked:
            row = lax.broadcasted_iota(I32, (blk, blk), 0)
            col = lax.broadcasted_iota(I32, (blk, blk), 1)
            s = jnp.where(col <= row, s, NEG)
        m_new = jnp.maximum(m, jnp.max(s, axis=1, keepdims=True))
        alpha = jnp.exp(m - m_new)
        p = jnp.exp(s - m_new)
        l = alpha * l + jnp.sum(p, axis=1, keepdims=True)
        acc = alpha * acc + _dot(p.astype(BF16), v_ref[pl.ds(start, blk), :])
        return m_new, l, acc

    init = (jnp.full((blk, 1), NEG, F32), jnp.zeros((blk, 1), F32), jnp.zeros((blk, MLA_V), F32))
    carry = lax.fori_loop(0, qi, lambda j, c: step(j, c, False), init)
    _, l, acc = step(qi, carry, True)
    o_ref[...] = (acc / l).astype(o_ref.dtype)


def mla_attention(q_full, kv, k_rope, *, batch, seq, blk):
    blk = min(blk, seq)
    nq = seq // blk
    m = batch * seq
    return pl.pallas_call(
        functools.partial(_mla_attn_kernel, blk=blk),
        out_shape=jax.ShapeDtypeStruct((m, MLA_WIDTH), BF16),
        grid=(batch, MLA_HEADS, nq),
        in_specs=[pl.BlockSpec((blk, 256), lambda b, h, i: (b * nq + i, h)),
                  pl.BlockSpec((seq, MLA_NOPE), lambda b, h, i: (b, 2 * h)),
                  pl.BlockSpec((seq, LANES), lambda b, h, i: (b, 0)),
                  pl.BlockSpec((seq, MLA_V), lambda b, h, i: (b, 2 * h + 1))],
        out_specs=pl.BlockSpec((blk, MLA_V), lambda b, h, i: (b * nq + i, h)),
        compiler_params=_cparams("parallel", "parallel", "arbitrary"),
        name="mla_attention",
    )(q_full, kv, k_rope, kv)


def _gla_kernel(q_ref, k_ref, v_ref, og_ref, lr_ref, wg_ref, bg_ref, gn_ref, o_ref, st_ref, *, sb, c):
    @pl.when(pl.program_id(2) == 0)
    def _():
        st_ref[...] = jnp.zeros_like(st_ref)

    z = _dot(lr_ref[...], wg_ref[...]) + bg_ref[...]
    log_a = (jnp.minimum(z, 0.0) - jnp.log(1.0 + jnp.exp(-jnp.abs(z)))) * (1.0 / GLA_TAU)
    rows = lax.broadcasted_iota(I32, (c, c), 0)
    cols = lax.broadcasted_iota(I32, (c, c), 1)
    causal = cols <= rows
    tril = causal.astype(F32)
    gn = gn_ref[...]
    scale = GLA_HEAD_K ** -0.5

    for n in range(sb // c):
        sl = slice(n * c, (n + 1) * c)
        g = log_a[sl, :]
        b = jnp.dot(tril, g, preferred_element_type=F32, precision=lax.Precision.HIGHEST)
        b_last = b[c - 1:c, :]
        q = q_ref[sl, :].astype(F32)
        k = k_ref[sl, :].astype(F32)
        v = v_ref[sl, :]
        qe = (q * scale * jnp.exp(b)).astype(BF16)
        ke = (k * jnp.exp(-b)).astype(BF16)
        kd = (k * jnp.exp(b_last - b)).astype(BF16)
        att = jnp.where(causal, _dot_nt(qe, ke), 0.0).astype(BF16)
        st = st_ref[...]
        o = _dot(att, v) + _dot_nt(qe, st.astype(BF16))
        st_ref[...] = st * jnp.exp(b_last) + _dot_tn(v, kd)
        og = og_ref[sl, :].astype(F32)
        o = _rms(o, gn) * (og * (1.0 / (1.0 + jnp.exp(-og))))
        o_ref[sl, :] = o.astype(o_ref.dtype)


def gla_mixer(h_in, w_gate_pad, b_gate, gla_out_norm, *, batch, seq, sb):
    sb = min(sb, seq)
    nsb = seq // sb
    m = batch * seq
    return pl.pallas_call(
        functools.partial(_gla_kernel, sb=sb, c=GLA_CHUNK),
        out_shape=jax.ShapeDtypeStruct((m, GLA_V_WIDTH), BF16),
        grid=(batch, GLA_HEADS, nsb),
        in_specs=[pl.BlockSpec((sb, GLA_HEAD_K), lambda b, h, s: (b * nsb + s, COL_GQ // GLA_HEAD_K + h)),
                  pl.BlockSpec((sb, GLA_HEAD_K), lambda b, h, s: (b * nsb + s, COL_GK // GLA_HEAD_K + h)),
                  pl.BlockSpec((sb, GLA_HEAD_V), lambda b, h, s: (b * nsb + s, COL_GV // GLA_HEAD_V + h)),
                  pl.BlockSpec((sb, GLA_HEAD_V), lambda b, h, s: (b * nsb + s, COL_OG // GLA_HEAD_V + h)),
                  pl.BlockSpec((sb, LANES), lambda b, h, s: (b * nsb + s, COL_LR // LANES)),
                  pl.BlockSpec((LANES, GLA_HEAD_K), lambda b, h, s: (0, h)),
                  pl.BlockSpec((1, GLA_HEAD_K), lambda b, h, s: (0, h)),
                  pl.BlockSpec((1, GLA_HEAD_V), lambda b, h, s: (0, 0))],
        out_specs=pl.BlockSpec((sb, GLA_HEAD_V), lambda b, h, s: (b * nsb + s, h)),
        scratch_shapes=[pltpu.VMEM((GLA_HEAD_V, GLA_HEAD_K), F32)],
        compiler_params=_cparams("parallel", "parallel", "arbitrary"),
        name="gla_mixer",
    )(h_in, h_in, h_in, h_in, h_in, w_gate_pad, b_gate.reshape(1, -1).astype(F32),
      gla_out_norm.reshape(1, -1).astype(F32))


def _out_proj_kernel(om_ref, g_ref, og_ref, w_ref, x_ref, o_ref, mix_ref):
    @pl.when(pl.program_id(1) == 0)
    def _():
        mix_ref[:, :MLA_WIDTH] = _rms(om_ref[...].astype(F32), g_ref[...]).astype(BF16)
        mix_ref[:, MLA_WIDTH:] = og_ref[...]

    o_ref[...] = x_ref[...] + _dot(mix_ref[...], w_ref[...])


def mixer_out_proj(o_mla, mla_out_norm, o_gla, w_out, x, *, tm, tn):
    m, d = x.shape
    tm = min(tm, m)
    tn = min(tn, d)
    kmix = MLA_WIDTH + GLA_V_WIDTH
    return pl.pallas_call(
        _out_proj_kernel,
        out_shape=jax.ShapeDtypeStruct((m, d), F32),
        grid=(m // tm, d // tn),
        in_specs=[pl.BlockSpec((tm, MLA_WIDTH), lambda i, j: (i, 0)),
                  pl.BlockSpec((1, MLA_WIDTH), lambda i, j: (0, 0)),
                  pl.BlockSpec((tm, GLA_V_WIDTH), lambda i, j: (i, 0)),
                  pl.BlockSpec((kmix, tn), lambda i, j: (0, j)),
                  pl.BlockSpec((tm, tn), lambda i, j: (i, j))],
        out_specs=pl.BlockSpec((tm, tn), lambda i, j: (i, j)),
        scratch_shapes=[pltpu.VMEM((tm, kmix), BF16)],
        compiler_params=_cparams("parallel", "arbitrary"),
        name="mixer_out_proj",
    )(o_mla, mla_out_norm.reshape(1, -1).astype(F32), o_gla, w_out, x)


def _xattn_kernel(q_ref, kv_ref, w_ref, h_ref, o_ref):
    scale = XATTN_DIM ** -0.5
    outs = []
    for h in range(XATTN_HEADS):
        q = q_ref[:, h * XATTN_DIM:(h + 1) * XATTN_DIM]
        k = kv_ref[:, h * XATTN_DIM:(h + 1) * XATTN_DIM]
        v = kv_ref[:, XATTN_WIDTH + h * XATTN_DIM: XATTN_WIDTH + (h + 1) * XATTN_DIM]
        s = _dot_nt(q, k) * scale
        p = jnp.exp(s - jnp.max(s, axis=1, keepdims=True))
        p = p / jnp.sum(p, axis=1, keepdims=True)
        outs.append(_dot(p.astype(BF16), v).astype(BF16))
    o = jnp.concatenate(outs, axis=1)
    o_ref[...] = h_ref[...] + _dot(o, w_ref[...])


def cross_attention(qx, kvm, w_co, h1, *, batch, seq, mem_tokens, tm):
    m, d = h1.shape
    tm = min(tm, seq)
    per_b = seq // tm
    return pl.pallas_call(
        _xattn_kernel,
        out_shape=jax.ShapeDtypeStruct((m, d), F32),
        grid=(m // tm,),
        in_specs=[pl.BlockSpec((tm, XATTN_WIDTH), lambda i: (i, 0)),
                  pl.BlockSpec((mem_tokens, 2 * XATTN_WIDTH), lambda i: (i // per_b, 0)),
                  pl.BlockSpec(w_co.shape, lambda i: (0, 0)),
                  pl.BlockSpec((tm, d), lambda i: (i, 0))],
        out_specs=pl.BlockSpec((tm, d), lambda i: (i, 0)),
        compiler_params=_cparams("parallel"),
        name="cross_attention",
    )(qx, kvm, w_co, h1)


def _top16_rows(s, n_rows):
    iota = lax.broadcasted_iota(I32, s.shape, 0)
    vals, idxs = [], []
    for _ in range(PEER_TOPK):
        mx = jnp.max(s, axis=0, keepdims=True)
        ix = jnp.min(jnp.where(s == mx, iota, n_rows), axis=0, keepdims=True)
        vals.append(mx)
        idxs.append(ix)
        s = jnp.where(iota == ix, -jnp.inf, s)
    return vals, idxs


_PAIRS = [(a, b) for a in range(PEER_TOPK) for b in range(PEER_TOPK) if (a + 1) * (b + 1) <= PEER_TOPK]
_PAIR_ROWS = -(-len(_PAIRS) // 8) * 8


def _peer_topk_kernel(q_ref, keys_ref, eid_ref, gate_ref, eid_t, gate_t, *, tm):
    def head(h, carry):
        tops = []
        for p in range(2):
            col = pl.multiple_of((2 * h + p) * PEER_HALF, PEER_HALF)
            qh = q_ref[:, pl.ds(col, PEER_HALF)]
            s = _dot_nt(keys_ref[2 * h + p], qh)
            tops.append(_top16_rows(s, PEER_KEYS))
        (s1, i1), (s2, i2) = tops
        cand_s = [s1[a] + s2[b] for a, b in _PAIRS]
        cand_e = [i1[a] * PEER_KEYS + i2[b] for a, b in _PAIRS]
        pad = _PAIR_ROWS - len(_PAIRS)
        cs = jnp.concatenate(cand_s + [jnp.full((pad, tm), -jnp.inf, F32)], axis=0)
        ce = jnp.concatenate(cand_e + [jnp.zeros((pad, tm), I32)], axis=0)
        iota = lax.broadcasted_iota(I32, cs.shape, 0)
        sel_s, sel_e = [], []
        for _ in range(PEER_TOPK):
            mx = jnp.max(cs, axis=0, keepdims=True)
            ix = jnp.min(jnp.where(cs == mx, iota, _PAIR_ROWS), axis=0, keepdims=True)
            hit = iota == ix
            sel_s.append(mx)
            sel_e.append(jnp.max(jnp.where(hit, ce, -1), axis=0, keepdims=True))
            cs = jnp.where(hit, -jnp.inf, cs)
        top_s = jnp.concatenate(sel_s, axis=0)
        top_e = jnp.concatenate(sel_e, axis=0)
        w = jnp.exp(top_s - top_s[0:1, :])
        gate = w / jnp.sum(w, axis=0, keepdims=True)
        row = pl.multiple_of(h * PEER_TOPK, PEER_TOPK)
        eid_t[pl.ds(row, PEER_TOPK), :] = top_e
        gate_t[pl.ds(row, PEER_TOPK), :] = gate
        return carry

    lax.fori_loop(0, PEER_HEADS, head, 0)
    eid_ref[...] = eid_t[...].T
    gate_ref[...] = gate_t[...].T


def peer_topk(qp, keys, *, tm):
    m = qp.shape[0]
    tm = min(tm, m)
    return pl.pallas_call(
        functools.partial(_peer_topk_kernel, tm=tm),
        out_shape=[jax.ShapeDtypeStruct((m, PEER_SLOTS), I32), jax.ShapeDtypeStruct((m, PEER_SLOTS), F32)],
        grid=(m // tm,),
        in_specs=[pl.BlockSpec((tm, qp.shape[1]), lambda i: (i, 0)),
                  pl.BlockSpec(keys.shape, lambda i: (0, 0, 0))],
        out_specs=[pl.BlockSpec((tm, PEER_SLOTS), lambda i: (i, 0)),
                   pl.BlockSpec((tm, PEER_SLOTS), lambda i: (i, 0))],
        scratch_shapes=[pltpu.VMEM((PEER_SLOTS, tm), I32), pltpu.VMEM((PEER_SLOTS, tm), F32)],
        compiler_params=_cparams("parallel"),
        name="peer_topk",
    )(qp, keys)


def _peer_gates_kernel(eid_ref, gate_ref, o_ref, g3_ref, *, tm):
    sub = lax.broadcasted_iota(I32, (PEER_KEYS, PEER_SLOTS), 0)

    def token(t, carry):
        e = eid_ref[pl.ds(t, 1), :]
        g = gate_ref[pl.ds(t, 1), :]
        hi = lax.shift_right_logical(e, 7)
        lo = jnp.bitwise_and(e, PEER_KEYS - 1)
        a1 = jnp.where(sub == hi, g, 0.0).astype(BF16)
        a2 = jnp.where(sub == lo, 1.0, 0.0).astype(BF16)
        g3_ref[pl.ds(pl.multiple_of(t * PEER_KEYS, PEER_KEYS), PEER_KEYS), :] = _dot_nt(a1, a2)
        return carry

    lax.fori_loop(0, tm, token, 0)
    for e1 in range(PEER_KEYS):
        rows = g3_ref[pl.ds(e1, tm, stride=PEER_KEYS), :]
        o_ref[:, e1 * PEER_KEYS:(e1 + 1) * PEER_KEYS] = rows.astype(o_ref.dtype)


def peer_gates(eid, gate, *, tm):
    m = eid.shape[0]
    tm = min(tm, m)
    return pl.pallas_call(
        functools.partial(_peer_gates_kernel, tm=tm),
        out_shape=jax.ShapeDtypeStruct((m, PEER_EXPERTS), BF16),
        grid=(m // tm,),
        in_specs=[pl.BlockSpec((tm, PEER_SLOTS), lambda i: (i, 0)),
                  pl.BlockSpec((tm, PEER_SLOTS), lambda i: (i, 0))],
        out_specs=pl.BlockSpec((tm, PEER_EXPERTS), lambda i: (i, 0)),
        scratch_shapes=[pltpu.VMEM((tm * PEER_KEYS, PEER_KEYS), F32)],
        compiler_params=_cparams("parallel"),
        name="peer_gates",
    )(eid, gate)


def _peer_dense_kernel(x_ref, u_ref, g_ref, v_ref, o_ref):
    @pl.when(pl.program_id(1) == 0)
    def _():
        o_ref[...] = jnp.zeros_like(o_ref)

    s = _dot_nt(x_ref[...], u_ref[...])
    a = 0.5 * s * (1.0 + lax.erf(s * (2.0 ** -0.5)))
    a = (a * g_ref[...].astype(F32)).astype(BF16)
    o_ref[...] += _dot(a, v_ref[...])


def peer_dense(xn, u, g, v, *, tm, te):
    m, d = xn.shape
    e = u.shape[0]
    tm = min(tm, m)
    return pl.pallas_call(
        _peer_dense_kernel,
        out_shape=jax.ShapeDtypeStruct((m, d), F32),
        grid=(m // tm, e // te),
        in_specs=[pl.BlockSpec((tm, d), lambda i, j: (i, 0)),
                  pl.BlockSpec((te, d), lambda i, j: (j, 0)),
                  pl.BlockSpec((tm, te), lambda i, j: (i, j)),
                  pl.BlockSpec((te, d), lambda i, j: (j, 0))],
        out_specs=pl.BlockSpec((tm, d), lambda i, j: (i, 0)),
        compiler_params=_cparams("parallel", "arbitrary"),
        name="peer_dense",
    )(xn, u, g, v)


def _final_kernel(h_ref, p_ref, g_ref, o_ref):
    o_ref[...] = _rms(h_ref[...] + p_ref[...], g_ref[...])


def final_norm(h2, peer, gain, *, tm):
    m, d = h2.shape
    tm = min(tm, m)
    return pl.pallas_call(
        _final_kernel,
        out_shape=jax.ShapeDtypeStruct((m, d), F32),
        grid=(m // tm,),
        in_specs=[pl.BlockSpec((tm, d), lambda i: (i, 0)),
                  pl.BlockSpec((tm, d), lambda i: (i, 0)),
                  pl.BlockSpec((1, d), lambda i: (0, 0))],
        out_specs=pl.BlockSpec((tm, d), lambda i: (i, 0)),
        compiler_params=_cparams("parallel"),
        name="final_norm",
    )(h2, peer, gain.reshape(1, d).astype(F32))


def _w_in_columns():
    src = {}
    off = 0
    for name, width in (("cq", MLA_Q_LORA), ("ckv", MLA_KV_LORA), ("kr", MLA_ROPE), ("gq", GLA_K_WIDTH),
                        ("gk", GLA_K_WIDTH), ("gv", GLA_V_WIDTH), ("lr", GLA_RANK), ("og", GLA_V_WIDTH)):
        src[name] = np.arange(off, off + width)
        off += width
    cols = np.full((IN_PAD,), -1, np.int64)
    for name, start in (("cq", COL_CQ), ("ckv", COL_CKV), ("gq", COL_GQ), ("gk", COL_GK),
                        ("gv", COL_GV), ("og", COL_OG), ("lr", COL_LR)):
        cols[start:start + len(src[name])] = src[name]
    half = MLA_ROPE // 2
    x1, x2 = src["kr"][:half], src["kr"][half:]
    cols[COL_KR:COL_KR + LANES] = np.concatenate([x1, x1, x2, x2])
    return cols


def _w_uq_columns():
    half = MLA_ROPE // 2
    nope = [h * MLA_QK + d for h in range(MLA_HEADS) for d in range(MLA_NOPE)]
    rope = []
    for j in range(MLA_HEADS // 2):
        a, b = 2 * j, 2 * j + 1
        for part in (0, 1):
            for h in (a, b):
                rope += [h * MLA_QK + MLA_NOPE + part * half + r for r in range(half)]
    return np.array(nope + rope, np.int64)


def _take_cols(w, cols):
    safe = np.where(cols < 0, 0, cols)
    out = jnp.take(w, jnp.asarray(safe, I32), axis=1)
    if (cols < 0).any():
        out = jnp.where(jnp.asarray(cols >= 0)[None, :], out, 0.0)
    return out


def _rope_tables(positions):
    half = MLA_ROPE // 2
    inv_freq = ROPE_THETA ** (-jnp.arange(0, MLA_ROPE, 2, dtype=F32) / MLA_ROPE)
    ang = positions.astype(F32).reshape(-1, 1) * inv_freq
    cos = jnp.cos(ang)
    sin = jnp.sin(ang)
    return jnp.tile(cos, (1, 4)), jnp.concatenate([-sin, -sin, sin, sin], axis=1)


def kernel(x, mem, positions, norm_mem, norm_mix, w_in, mla_q_norm, w_uq, mla_kv_norm, w_ukv, mla_out_norm,
           w_gate_up, b_gate, gla_out_norm, w_out, norm_cross, w_cq, w_ck, w_cv, w_co, norm_ffn, w_peer_q,
           peer_sub_keys, peer_u, peer_v, norm_final):
    batch, seq, d = x.shape
    mem_tokens = mem.shape[1]
    m = batch * seq
    h = x.reshape(m, d)
    cos_t, sin_t = _rope_tables(positions)

    mn_kv = norm_matmul(mem.reshape(batch * mem_tokens, d), norm_mem,
                        jnp.concatenate([w_ck[0], w_cv[0]], axis=1).astype(BF16),
                        k=d, tm=512, tn=1024, out_dtype=BF16, name="mem_kv_proj")

    for l in range(norm_mix.shape[0]):
        w_in_p = _take_cols(w_in[l], _w_in_columns()).astype(BF16)
        h_in = norm_matmul(h, norm_mix[l], w_in_p, k=d, tm=512, tn=1024, out_dtype=BF16, name="in_proj")

        w_uq_p = _take_cols(w_uq[l], _w_uq_columns()).astype(BF16)
        q_full = mla_q_proj(h_in, mla_q_norm[l], w_uq_p, cos_t, sin_t, tm=512)
        kv, k_rope = mla_kv_proj(h_in, mla_kv_norm[l], w_ukv[l].astype(BF16), cos_t, sin_t, tm=512)
        o_mla = mla_attention(q_full, kv, k_rope, batch=batch, seq=seq, blk=512)

        w_gate_pad = jnp.zeros((LANES, GLA_K_WIDTH), F32).at[:GLA_RANK].set(w_gate_up[l]).astype(BF16)
        o_gla = gla_mixer(h_in, w_gate_pad, b_gate[l], gla_out_norm[l], batch=batch, seq=seq, sb=1024)

        h = mixer_out_proj(o_mla, mla_out_norm[l], o_gla, w_out[l].astype(BF16), h, tm=512, tn=1024)

        qx = norm_matmul(h, norm_cross[l], w_cq[l].astype(BF16), k=d, tm=512, tn=1024, out_dtype=BF16,
                         name="xattn_q_proj")
        h = cross_attention(qx, mn_kv, w_co[l].astype(BF16), h, batch=batch, seq=seq,
                            mem_tokens=mem_tokens, tm=256)

        qp, hn = norm_matmul(h, norm_ffn[l], w_peer_q[l].astype(BF16), k=d, tm=512, tn=512, out_dtype=BF16,
                             emit_xn=True, name="peer_q_proj")
        keys = peer_sub_keys[l].reshape(PEER_HEADS * 2, PEER_KEYS, PEER_HALF).astype(BF16)
        eid, gate = peer_topk(qp, keys, tm=256)
        g = peer_gates(eid, gate, tm=128)
        peer = peer_dense(hn, peer_u[l].astype(BF16), g, peer_v[l].astype(BF16), tm=512, te=512)
        if l + 1 < norm_mix.shape[0]:
            h = h + peer
    out = final_norm(h, peer, norm_final, tm=256)
    return out.reshape(batch, seq, d)
```

```python
import functools
import math

import numpy as np
import jax
import jax.numpy as jnp
from jax import lax
from jax.experimental import pallas as pl
from jax.experimental.pallas import tpu as pltpu

F32 = jnp.float32
BF16 = jnp.bfloat16
I32 = jnp.int32

EPS = 1e-6
ROPE_THETA = 10000.0

MLA_HEADS = 16
MLA_Q_LORA = 1024
MLA_KV_LORA = 512
MLA_NOPE = 128
MLA_ROPE = 64
MLA_QK = MLA_NOPE + MLA_ROPE
MLA_V = 128
MLA_WIDTH = MLA_HEADS * MLA_V

GLA_HEADS = 4
GLA_HEAD_K = 256
GLA_HEAD_V = 512
GLA_K_WIDTH = GLA_HEADS * GLA_HEAD_K
GLA_V_WIDTH = GLA_HEADS * GLA_HEAD_V
GLA_RANK = 16
GLA_TAU = 16.0
GLA_CHUNK = 64

XATTN_HEADS = 4
XATTN_DIM = 256
XATTN_WIDTH = XATTN_HEADS * XATTN_DIM

PEER_HEADS = 8
PEER_KEYS = 128
PEER_HALF = 128
PEER_TOPK = 16
PEER_SLOTS = PEER_HEADS * PEER_TOPK
PEER_EXPERTS = PEER_KEYS * PEER_KEYS

LANES = 128
VMEM_LIMIT = 56 * 1024 * 1024

COL_CQ = 0
COL_CKV = COL_CQ + MLA_Q_LORA
COL_GQ = COL_CKV + MLA_KV_LORA
COL_GK = COL_GQ + GLA_K_WIDTH
COL_GV = COL_GK + GLA_K_WIDTH
COL_OG = COL_GV + GLA_V_WIDTH
COL_KR = COL_OG + GLA_V_WIDTH
COL_LR = COL_KR + LANES
IN_PAD = 8192

NEG = -1e30


def _cparams(*sem):
    return pltpu.CompilerParams(dimension_semantics=sem, vmem_limit_bytes=VMEM_LIMIT)


def _rms(x, g):
    ms = jnp.mean(x * x, axis=-1, keepdims=True)
    return x * lax.rsqrt(ms + EPS) * g


def _dot(a, b):
    return jnp.dot(a, b, preferred_element_type=F32)


def _dot_nt(a, b):
    return lax.dot_general(a, b, (((1,), (1,)), ((), ())), preferred_element_type=F32)


def _dot_tn(a, b):
    return lax.dot_general(a, b, (((0,), (0,)), ((), ())), preferred_element_type=F32)


def _norm_matmul_kernel(x_ref, g_ref, w_ref, o_ref, *rest, emit_xn):
    if emit_xn:
        xn_out_ref, xn_ref = rest
    else:
        (xn_ref,) = rest

    @pl.when(pl.program_id(1) == 0)
    def _():
        xn = _rms(x_ref[...].astype(F32), g_ref[...]).astype(BF16)
        xn_ref[...] = xn
        if emit_xn:
            xn_out_ref[...] = xn

    o_ref[...] = _dot(xn_ref[...], w_ref[...]).astype(o_ref.dtype)


def norm_matmul(x, gain, w, *, k, x_col_blk=0, tm, tn, out_dtype, emit_xn=False, name):
    m = x.shape[0]
    n = w.shape[1]
    tm = min(tm, m)
    tn = min(tn, n)
    out_shape = [jax.ShapeDtypeStruct((m, n), out_dtype)]
    out_specs = [pl.BlockSpec((tm, tn), lambda i, j: (i, j))]
    if emit_xn:
        out_shape.append(jax.ShapeDtypeStruct((m, k), BF16))
        out_specs.append(pl.BlockSpec((tm, k), lambda i, j: (i, 0)))
    res = pl.pallas_call(
        functools.partial(_norm_matmul_kernel, emit_xn=emit_xn),
        out_shape=out_shape,
        grid=(m // tm, n // tn),
        in_specs=[pl.BlockSpec((tm, k), lambda i, j: (i, x_col_blk)),
                  pl.BlockSpec((1, k), lambda i, j: (0, 0)),
                  pl.BlockSpec((k, tn), lambda i, j: (0, j))],
        out_specs=out_specs,
        scratch_shapes=[pltpu.VMEM((tm, k), BF16)],
        compiler_params=_cparams("parallel", "arbitrary"),
        name=name,
    )(x, gain.reshape(1, k).astype(F32), w)
    return res if emit_xn else res[0]


def _rope_pair(r, cosv, sinv):
    return r * cosv + pltpu.roll(r, 2 * 32, axis=1) * sinv


def _qproj_kernel(c_ref, g_ref, w_ref, cos_ref, sin_ref, o_ref, *, scale):
    xn = _rms(c_ref[...].astype(F32), g_ref[...]).astype(BF16)
    q = _dot(xn, w_ref[...])
    cosv = cos_ref[...]
    sinv = sin_ref[...]
    lane = lax.broadcasted_iota(I32, (1, LANES), 1)
    rope_base = MLA_HEADS * MLA_NOPE
    for j in range(MLA_HEADS // 2):
        r = q[:, rope_base + j * LANES: rope_base + (j + 1) * LANES]
        r = _rope_pair(r, cosv, sinv) * scale
        for p in range(2):
            h = 2 * j + p
            own = ((lane // 32) % 2) == p
            o_ref[:, h * 256: h * 256 + 128] = (q[:, h * 128:(h + 1) * 128] * scale).astype(o_ref.dtype)
            o_ref[:, h * 256 + 128: h * 256 + 256] = jnp.where(own, r, 0.0).astype(o_ref.dtype)


def mla_q_proj(h_in, gain, w_uq_perm, cos_t, sin_t, *, tm):
    m = h_in.shape[0]
    tm = min(tm, m)
    return pl.pallas_call(
        functools.partial(_qproj_kernel, scale=MLA_QK ** -0.5 * math.log2(math.e)),
        out_shape=jax.ShapeDtypeStruct((m, MLA_HEADS * 256), BF16),
        grid=(m // tm,),
        in_specs=[pl.BlockSpec((tm, MLA_Q_LORA), lambda i: (i, COL_CQ // MLA_Q_LORA)),
                  pl.BlockSpec((1, MLA_Q_LORA), lambda i: (0, 0)),
                  pl.BlockSpec(w_uq_perm.shape, lambda i: (0, 0)),
                  pl.BlockSpec((tm, LANES), lambda i: (i, 0)),
                  pl.BlockSpec((tm, LANES), lambda i: (i, 0))],
        out_specs=pl.BlockSpec((tm, MLA_HEADS * 256), lambda i: (i, 0)),
        compiler_params=_cparams("parallel"),
        name="mla_q_proj",
    )(h_in, gain.reshape(1, -1).astype(F32), w_uq_perm, cos_t, sin_t)


def _kvproj_kernel(c_ref, g_ref, w_ref, kr_ref, cos_ref, sin_ref, kv_ref, kro_ref):
    xn = _rms(c_ref[...].astype(F32), g_ref[...]).astype(BF16)
    kv_ref[...] = _dot(xn, w_ref[...]).astype(kv_ref.dtype)
    kro_ref[...] = _rope_pair(kr_ref[...].astype(F32), cos_ref[...], sin_ref[...]).astype(kro_ref.dtype)


def mla_kv_proj(h_in, gain, w_ukv, cos_t, sin_t, *, tm):
    m = h_in.shape[0]
    tm = min(tm, m)
    n = w_ukv.shape[1]
    return pl.pallas_call(
        _kvproj_kernel,
        out_shape=[jax.ShapeDtypeStruct((m, n), BF16), jax.ShapeDtypeStruct((m, LANES), BF16)],
        grid=(m // tm,),
        in_specs=[pl.BlockSpec((tm, MLA_KV_LORA), lambda i: (i, COL_CKV // MLA_KV_LORA)),
                  pl.BlockSpec((1, MLA_KV_LORA), lambda i: (0, 0)),
                  pl.BlockSpec(w_ukv.shape, lambda i: (0, 0)),
                  pl.BlockSpec((tm, LANES), lambda i: (i, COL_KR // LANES)),
                  pl.BlockSpec((tm, LANES), lambda i: (i, 0)),
                  pl.BlockSpec((tm, LANES), lambda i: (i, 0))],
        out_specs=[pl.BlockSpec((tm, n), lambda i: (i, 0)),
                   pl.BlockSpec((tm, LANES), lambda i: (i, 0))],
        compiler_params=_cparams("parallel"),
        name="mla_kv_proj",
    )(h_in, gain.reshape(1, -1).astype(F32), w_ukv, h_in, cos_t, sin_t)


ATTN_UNROLL = 4


def _mla_attn_kernel(q_ref, kv_ref, kr_ref, o_ref, s0_scr, s1_scr, m_scr, acc_scr, *, blk):
    qi = pl.program_id(2)
    q = q_ref[...]
    ones = jnp.ones((blk, LANES), BF16)
    m_scr[...] = jnp.full(m_scr.shape, NEG, F32)
    acc_scr[...] = jnp.zeros(acc_scr.shape, F32)

    def scores(j, s_scr):
        start = pl.multiple_of(j * blk, blk)
        k = jnp.concatenate([kv_ref[pl.ds(start, blk), :MLA_NOPE], kr_ref[pl.ds(start, blk), :]], axis=1)
        s_scr[...] = _dot_nt(q, k)

    def finish(j, s_scr, masked):
        start = pl.multiple_of(j * blk, blk)
        s = s_scr[...]
        if masked:
            row = lax.broadcasted_iota(I32, (blk, blk), 0)
            col = lax.broadcasted_iota(I32, (blk, blk), 1)
            s = jnp.where(col <= row, s, NEG)
        m = m_scr[...]
        m_new = jnp.maximum(m, jnp.max(s, axis=1, keepdims=True))
        alpha = jnp.exp2(m - m_new)
        p = jnp.exp2(s - pltpu.repeat(m_new, blk // LANES, axis=1)).astype(BF16)
        m_scr[...] = m_new
        v = jnp.concatenate([kv_ref[pl.ds(start, blk), MLA_NOPE:], ones], axis=1)
        acc_scr[...] = pltpu.repeat(alpha, 2, axis=1) * acc_scr[...] + _dot(p, v)

    s_scr = (s0_scr, s1_scr)

    def run(j0, n):
        for u in range(n):
            scores(j0 + u + 1, s_scr[(u + 1) % 2])
            finish(j0 + u, s_scr[u % 2], False)

    scores(0, s0_scr)

    def body(t, carry):
        run(t * ATTN_UNROLL, ATTN_UNROLL)
        return carry

    lax.fori_loop(0, qi // ATTN_UNROLL, body, 0)
    done = (qi // ATTN_UNROLL) * ATTN_UNROLL
    n = ATTN_UNROLL // 2
    while n >= 1:
        @pl.when((qi & n) != 0)
        def _(done=done, n=n):
            run(done, n)

        done = done + (qi & n)
        n //= 2

    @pl.when(qi % 2 == 0)
    def _():
        finish(qi, s0_scr, True)

    @pl.when(qi % 2 == 1)
    def _():
        finish(qi, s1_scr, True)

    acc = acc_scr[...]
    o_ref[...] = (acc[:, :MLA_V] / acc[:, MLA_V:]).astype(o_ref.dtype)


def mla_attention(q_full, kv, k_rope, *, batch, seq, blk):
    blk = min(blk, seq)
    nq = seq // blk
    m = batch * seq
    return pl.pallas_call(
        functools.partial(_mla_attn_kernel, blk=blk),
        out_shape=jax.ShapeDtypeStruct((m, MLA_WIDTH), BF16),
        grid=(batch, MLA_HEADS, nq),
        in_specs=[pl.BlockSpec((blk, 256), lambda b, h, i: (b * nq + i, h)),
                  pl.BlockSpec((seq, 256), lambda b, h, i: (b, h)),
                  pl.BlockSpec((seq, LANES), lambda b, h, i: (b, 0))],
        out_specs=pl.BlockSpec((blk, MLA_V), lambda b, h, i: (b * nq + i, h)),
        scratch_shapes=[pltpu.VMEM((blk, blk), F32), pltpu.VMEM((blk, blk), F32),
                        pltpu.VMEM((blk, LANES), F32), pltpu.VMEM((blk, 2 * MLA_V), F32)],
        compiler_params=_cparams("parallel", "parallel", "arbitrary"),
        name="mla_attention",
    )(q_full, kv, k_rope)


def _gla_kernel(q_ref, k_ref, v_ref, og_ref, lr_ref, wg_ref, bg_ref, gn_ref, o_ref, st_ref, *, sb, c):
    @pl.when(pl.program_id(2) == 0)
    def _():
        st_ref[...] = jnp.zeros_like(st_ref)

    z = _dot(lr_ref[...], wg_ref[...]) + bg_ref[...]
    log_a = (jnp.minimum(z, 0.0) - jnp.log(1.0 + jnp.exp(-jnp.abs(z)))) * (1.0 / GLA_TAU)
    rows = lax.broadcasted_iota(I32, (c, c), 0)
    cols = lax.broadcasted_iota(I32, (c, c), 1)
    causal = cols <= rows
    tril = causal.astype(F32)
    gn = gn_ref[...]
    scale = GLA_HEAD_K ** -0.5

    for n in range(sb // c):
        sl = slice(n * c, (n + 1) * c)
        g = log_a[sl, :]
        b = jnp.dot(tril, g, preferred_element_type=F32, precision=lax.Precision.HIGHEST)
        b_last = b[c - 1:c, :]
        q = q_ref[sl, :].astype(F32)
        k = k_ref[sl, :].astype(F32)
        v = v_ref[sl, :]
        qe = (q * scale * jnp.exp(b)).astype(BF16)
        ke = (k * jnp.exp(-b)).astype(BF16)
        kd = (k * jnp.exp(b_last - b)).astype(BF16)
        att = jnp.where(causal, _dot_nt(qe, ke), 0.0).astype(BF16)
        st = st_ref[...]
        o = _dot(att, v) + _dot_nt(qe, st.astype(BF16))
        st_ref[...] = st * jnp.exp(b_last) + _dot_tn(v, kd)
        og = og_ref[sl, :].astype(F32)
        o = _rms(o, gn) * (og * (1.0 / (1.0 + jnp.exp(-og))))
        o_ref[sl, :] = o.astype(o_ref.dtype)


def gla_mixer(h_in, w_gate_pad, b_gate, gla_out_norm, *, batch, seq, sb):
    sb = min(sb, seq)
    nsb = seq // sb
    m = batch * seq
    return pl.pallas_call(
        functools.partial(_gla_kernel, sb=sb, c=GLA_CHUNK),
        out_shape=jax.ShapeDtypeStruct((m, GLA_V_WIDTH), BF16),
        grid=(batch, GLA_HEADS, nsb),
        in_specs=[pl.BlockSpec((sb, GLA_HEAD_K), lambda b, h, s: (b * nsb + s, COL_GQ // GLA_HEAD_K + h)),
                  pl.BlockSpec((sb, GLA_HEAD_K), lambda b, h, s: (b * nsb + s, COL_GK // GLA_HEAD_K + h)),
                  pl.BlockSpec((sb, GLA_HEAD_V), lambda b, h, s: (b * nsb + s, COL_GV // GLA_HEAD_V + h)),
                  pl.BlockSpec((sb, GLA_HEAD_V), lambda b, h, s: (b * nsb + s, COL_OG // GLA_HEAD_V + h)),
                  pl.BlockSpec((sb, LANES), lambda b, h, s: (b * nsb + s, COL_LR // LANES)),
                  pl.BlockSpec((LANES, GLA_HEAD_K), lambda b, h, s: (0, h)),
                  pl.BlockSpec((1, GLA_HEAD_K), lambda b, h, s: (0, h)),
                  pl.BlockSpec((1, GLA_HEAD_V), lambda b, h, s: (0, 0))],
        out_specs=pl.BlockSpec((sb, GLA_HEAD_V), lambda b, h, s: (b * nsb + s, h)),
        scratch_shapes=[pltpu.VMEM((GLA_HEAD_V, GLA_HEAD_K), F32)],
        compiler_params=_cparams("parallel", "parallel", "arbitrary"),
        name="gla_mixer",
    )(h_in, h_in, h_in, h_in, h_in, w_gate_pad, b_gate.reshape(1, -1).astype(F32),
      gla_out_norm.reshape(1, -1).astype(F32))


def _out_proj_kernel(om_ref, g_ref, og_ref, w_ref, x_ref, o_ref, mix_ref):
    @pl.when(pl.program_id(1) == 0)
    def _():
        mix_ref[:, :MLA_WIDTH] = _rms(om_ref[...].astype(F32), g_ref[...]).astype(BF16)
        mix_ref[:, MLA_WIDTH:] = og_ref[...]

    o_ref[...] = x_ref[...] + _dot(mix_ref[...], w_ref[...])


def mixer_out_proj(o_mla, mla_out_norm, o_gla, w_out, x, *, tm, tn):
    m, d = x.shape
    tm = min(tm, m)
    tn = min(tn, d)
    kmix = MLA_WIDTH + GLA_V_WIDTH
    return pl.pallas_call(
        _out_proj_kernel,
        out_shape=jax.ShapeDtypeStruct((m, d), F32),
        grid=(m // tm, d // tn),
        in_specs=[pl.BlockSpec((tm, MLA_WIDTH), lambda i, j: (i, 0)),
                  pl.BlockSpec((1, MLA_WIDTH), lambda i, j: (0, 0)),
                  pl.BlockSpec((tm, GLA_V_WIDTH), lambda i, j: (i, 0)),
                  pl.BlockSpec((kmix, tn), lambda i, j: (0, j)),
                  pl.BlockSpec((tm, tn), lambda i, j: (i, j))],
        out_specs=pl.BlockSpec((tm, tn), lambda i, j: (i, j)),
        scratch_shapes=[pltpu.VMEM((tm, kmix), BF16)],
        compiler_params=_cparams("parallel", "arbitrary"),
        name="mixer_out_proj",
    )(o_mla, mla_out_norm.reshape(1, -1).astype(F32), o_gla, w_out, x)


def _xattn_kernel(q_ref, kv_ref, w_ref, h_ref, o_ref):
    scale = XATTN_DIM ** -0.5
    outs = []
    for h in range(XATTN_HEADS):
        q = q_ref[:, h * XATTN_DIM:(h + 1) * XATTN_DIM]
        k = kv_ref[:, h * XATTN_DIM:(h + 1) * XATTN_DIM]
        v = kv_ref[:, XATTN_WIDTH + h * XATTN_DIM: XATTN_WIDTH + (h + 1) * XATTN_DIM]
        s = _dot_nt(q, k) * scale
        p = jnp.exp(s - jnp.max(s, axis=1, keepdims=True))
        p = p / jnp.sum(p, axis=1, keepdims=True)
        outs.append(_dot(p.astype(BF16), v).astype(BF16))
    o = jnp.concatenate(outs, axis=1)
    o_ref[...] = h_ref[...] + _dot(o, w_ref[...])


def cross_attention(qx, kvm, w_co, h1, *, batch, seq, mem_tokens, tm):
    m, d = h1.shape
    tm = min(tm, seq)
    per_b = seq // tm
    return pl.pallas_call(
        _xattn_kernel,
        out_shape=jax.ShapeDtypeStruct((m, d), F32),
        grid=(m // tm,),
        in_specs=[pl.BlockSpec((tm, XATTN_WIDTH), lambda i: (i, 0)),
                  pl.BlockSpec((mem_tokens, 2 * XATTN_WIDTH), lambda i: (i // per_b, 0)),
                  pl.BlockSpec(w_co.shape, lambda i: (0, 0)),
                  pl.BlockSpec((tm, d), lambda i: (i, 0))],
        out_specs=pl.BlockSpec((tm, d), lambda i: (i, 0)),
        compiler_params=_cparams("parallel"),
        name="cross_attention",
    )(qx, kvm, w_co, h1)


def _top16_rows(s, n_rows):
    iota = lax.broadcasted_iota(I32, s.shape, 0)
    vals, idxs = [], []
    for _ in range(PEER_TOPK):
        mx = jnp.max(s, axis=0, keepdims=True)
        ix = jnp.min(jnp.where(s == mx, iota, n_rows), axis=0, keepdims=True)
        vals.append(mx)
        idxs.append(ix)
        s = jnp.where(iota == ix, -jnp.inf, s)
    return vals, idxs


_PAIRS = [(a, b) for a in range(PEER_TOPK) for b in range(PEER_TOPK) if (a + 1) * (b + 1) <= PEER_TOPK]
_PAIR_ROWS = -(-len(_PAIRS) // 8) * 8


def _peer_topk_kernel(q_ref, keys_ref, eid_ref, gate_ref, eid_t, gate_t, *, tm):
    def head(h, carry):
        tops = []
        for p in range(2):
            col = pl.multiple_of((2 * h + p) * PEER_HALF, PEER_HALF)
            qh = q_ref[:, pl.ds(col, PEER_HALF)]
            s = _dot_nt(keys_ref[2 * h + p], qh)
            tops.append(_top16_rows(s, PEER_KEYS))
        (s1, i1), (s2, i2) = tops
        cand_s = [s1[a] + s2[b] for a, b in _PAIRS]
        cand_e = [i1[a] * PEER_KEYS + i2[b] for a, b in _PAIRS]
        pad = _PAIR_ROWS - len(_PAIRS)
        cs = jnp.concatenate(cand_s + [jnp.full((pad, tm), -jnp.inf, F32)], axis=0)
        ce = jnp.concatenate(cand_e + [jnp.zeros((pad, tm), I32)], axis=0)
        iota = lax.broadcasted_iota(I32, cs.shape, 0)
        sel_s, sel_e = [], []
        for _ in range(PEER_TOPK):
            mx = jnp.max(cs, axis=0, keepdims=True)
            ix = jnp.min(jnp.where(cs == mx, iota, _PAIR_ROWS), axis=0, keepdims=True)
            hit = iota == ix
            sel_s.append(mx)
            sel_e.append(jnp.max(jnp.where(hit, ce, -1), axis=0, keepdims=True))
            cs = jnp.where(hit, -jnp.inf, cs)
        top_s = jnp.concatenate(sel_s, axis=0)
        top_e = jnp.concatenate(sel_e, axis=0)
        w = jnp.exp(top_s - top_s[0:1, :])
        gate = w / jnp.sum(w, axis=0, keepdims=True)
        row = pl.multiple_of(h * PEER_TOPK, PEER_TOPK)
        eid_t[pl.ds(row, PEER_TOPK), :] = top_e
        gate_t[pl.ds(row, PEER_TOPK), :] = gate
        return carry

    lax.fori_loop(0, PEER_HEADS, head, 0)
    eid_ref[...] = eid_t[...].T
    gate_ref[...] = gate_t[...].T


def peer_topk(qp, keys, *, tm):
    m = qp.shape[0]
    tm = min(tm, m)
    return pl.pallas_call(
        functools.partial(_peer_topk_kernel, tm=tm),
        out_shape=[jax.ShapeDtypeStruct((m, PEER_SLOTS), I32), jax.ShapeDtypeStruct((m, PEER_SLOTS), F32)],
        grid=(m // tm,),
        in_specs=[pl.BlockSpec((tm, qp.shape[1]), lambda i: (i, 0)),
                  pl.BlockSpec(keys.shape, lambda i: (0, 0, 0))],
        out_specs=[pl.BlockSpec((tm, PEER_SLOTS), lambda i: (i, 0)),
                   pl.BlockSpec((tm, PEER_SLOTS), lambda i: (i, 0))],
        scratch_shapes=[pltpu.VMEM((PEER_SLOTS, tm), I32), pltpu.VMEM((PEER_SLOTS, tm), F32)],
        compiler_params=_cparams("parallel"),
        name="peer_topk",
    )(qp, keys)


GATE_UNROLL = 8


def _peer_gates_kernel(eid_ref, gate_ref, o_ref, g3_ref, *, tm, pitch):
    sub = lax.broadcasted_iota(I32, (PEER_KEYS, PEER_SLOTS), 0)

    def token(t, carry):
        e = eid_ref[pl.ds(t, 1), :]
        g = gate_ref[pl.ds(t, 1), :]
        hi = lax.shift_right_logical(e, 7)
        lo = jnp.bitwise_and(e, PEER_KEYS - 1)
        a1 = jnp.where(sub == hi, g, 0.0).astype(BF16)
        a2 = jnp.where(sub == lo, 1.0, 0.0).astype(BF16)
        g3_ref[pl.ds(t, PEER_KEYS, stride=pitch), :] = _dot_nt(a1, a2)
        return carry

    lax.fori_loop(0, tm, token, 0, unroll=GATE_UNROLL)
    for e1 in range(PEER_KEYS):
        o_ref[:, e1 * PEER_KEYS:(e1 + 1) * PEER_KEYS] = g3_ref[e1 * pitch:e1 * pitch + tm, :].astype(o_ref.dtype)


def peer_gates(eid, gate, *, tm):
    m = eid.shape[0]
    tm = min(tm, m)
    pitch = tm + 8
    return pl.pallas_call(
        functools.partial(_peer_gates_kernel, tm=tm, pitch=pitch),
        out_shape=jax.ShapeDtypeStruct((m, PEER_EXPERTS), BF16),
        grid=(m // tm,),
        in_specs=[pl.BlockSpec((tm, PEER_SLOTS), lambda i: (i, 0)),
                  pl.BlockSpec((tm, PEER_SLOTS), lambda i: (i, 0))],
        out_specs=pl.BlockSpec((tm, PEER_EXPERTS), lambda i: (i, 0)),
        scratch_shapes=[pltpu.VMEM((PEER_KEYS * pitch, PEER_KEYS), F32)],
        compiler_params=_cparams("parallel"),
        name="peer_gates",
    )(eid, gate)


def _peer_dense_kernel(x_ref, u_ref, g_ref, v_ref, o_ref):
    @pl.when(pl.program_id(1) == 0)
    def _():
        o_ref[...] = jnp.zeros_like(o_ref)

    s = _dot_nt(x_ref[...], u_ref[...])
    a = 0.5 * s * (1.0 + lax.erf(s * (2.0 ** -0.5)))
    a = (a * g_ref[...].astype(F32)).astype(BF16)
    o_ref[...] += _dot(a, v_ref[...])


def peer_dense(xn, u, g, v, *, tm, te):
    m, d = xn.shape
    e = u.shape[0]
    tm = min(tm, m)
    return pl.pallas_call(
        _peer_dense_kernel,
        out_shape=jax.ShapeDtypeStruct((m, d), F32),
        grid=(m // tm, e // te),
        in_specs=[pl.BlockSpec((tm, d), lambda i, j: (i, 0)),
                  pl.BlockSpec((te, d), lambda i, j: (j, 0)),
                  pl.BlockSpec((tm, te), lambda i, j: (i, j)),
                  pl.BlockSpec((te, d), lambda i, j: (j, 0))],
        out_specs=pl.BlockSpec((tm, d), lambda i, j: (i, 0)),
        compiler_params=_cparams("parallel", "arbitrary"),
        name="peer_dense",
    )(xn, u, g, v)


def _final_kernel(h_ref, p_ref, g_ref, o_ref):
    o_ref[...] = _rms(h_ref[...] + p_ref[...], g_ref[...])


def final_norm(h2, peer, gain, *, tm):
    m, d = h2.shape
    tm = min(tm, m)
    return pl.pallas_call(
        _final_kernel,
        out_shape=jax.ShapeDtypeStruct((m, d), F32),
        grid=(m // tm,),
        in_specs=[pl.BlockSpec((tm, d), lambda i: (i, 0)),
                  pl.BlockSpec((tm, d), lambda i: (i, 0)),
                  pl.BlockSpec((1, d), lambda i: (0, 0))],
        out_specs=pl.BlockSpec((tm, d), lambda i: (i, 0)),
        compiler_params=_cparams("parallel"),
        name="final_norm",
    )(h2, peer, gain.reshape(1, d).astype(F32))


def _w_in_columns():
    src = {}
    off = 0
    for name, width in (("cq", MLA_Q_LORA), ("ckv", MLA_KV_LORA), ("kr", MLA_ROPE), ("gq", GLA_K_WIDTH),
                        ("gk", GLA_K_WIDTH), ("gv", GLA_V_WIDTH), ("lr", GLA_RANK), ("og", GLA_V_WIDTH)):
        src[name] = np.arange(off, off + width)
        off += width
    cols = np.full((IN_PAD,), -1, np.int64)
    for name, start in (("cq", COL_CQ), ("ckv", COL_CKV), ("gq", COL_GQ), ("gk", COL_GK),
                        ("gv", COL_GV), ("og", COL_OG), ("lr", COL_LR)):
        cols[start:start + len(src[name])] = src[name]
    half = MLA_ROPE // 2
    x1, x2 = src["kr"][:half], src["kr"][half:]
    cols[COL_KR:COL_KR + LANES] = np.concatenate([x1, x1, x2, x2])
    return cols


def _w_uq_columns():
    half = MLA_ROPE // 2
    nope = [h * MLA_QK + d for h in range(MLA_HEADS) for d in range(MLA_NOPE)]
    rope = []
    for j in range(MLA_HEADS // 2):
        a, b = 2 * j, 2 * j + 1
        for part in (0, 1):
            for h in (a, b):
                rope += [h * MLA_QK + MLA_NOPE + part * half + r for r in range(half)]
    return np.array(nope + rope, np.int64)


def _take_cols(w, cols):
    safe = np.where(cols < 0, 0, cols)
    out = jnp.take(w, jnp.asarray(safe, I32), axis=1)
    if (cols < 0).any():
        out = jnp.where(jnp.asarray(cols >= 0)[None, :], out, 0.0)
    return out


def _rope_tables(positions):
    half = MLA_ROPE // 2
    inv_freq = ROPE_THETA ** (-jnp.arange(0, MLA_ROPE, 2, dtype=F32) / MLA_ROPE)
    ang = positions.astype(F32).reshape(-1, 1) * inv_freq
    cos = jnp.cos(ang)
    sin = jnp.sin(ang)
    return jnp.tile(cos, (1, 4)), jnp.concatenate([-sin, -sin, sin, sin], axis=1)


def kernel(x, mem, positions, norm_mem, norm_mix, w_in, mla_q_norm, w_uq, mla_kv_norm, w_ukv, mla_out_norm,
           w_gate_up, b_gate, gla_out_norm, w_out, norm_cross, w_cq, w_ck, w_cv, w_co, norm_ffn, w_peer_q,
           peer_sub_keys, peer_u, peer_v, norm_final):
    batch, seq, d = x.shape
    mem_tokens = mem.shape[1]
    m = batch * seq
    h = x.reshape(m, d)
    cos_t, sin_t = _rope_tables(positions)

    mn_kv = norm_matmul(mem.reshape(batch * mem_tokens, d), norm_mem,
                        jnp.concatenate([w_ck[0], w_cv[0]], axis=1).astype(BF16),
                        k=d, tm=512, tn=1024, out_dtype=BF16, name="mem_kv_proj")

    for l in range(norm_mix.shape[0]):
        w_in_p = _take_cols(w_in[l], _w_in_columns()).astype(BF16)
        h_in = norm_matmul(h, norm_mix[l], w_in_p, k=d, tm=512, tn=1024, out_dtype=BF16, name="in_proj")

        w_uq_p = _take_cols(w_uq[l], _w_uq_columns()).astype(BF16)
        q_full = mla_q_proj(h_in, mla_q_norm[l], w_uq_p, cos_t, sin_t, tm=512)
        kv, k_rope = mla_kv_proj(h_in, mla_kv_norm[l], w_ukv[l].astype(BF16), cos_t, sin_t, tm=512)
        o_mla = mla_attention(q_full, kv, k_rope, batch=batch, seq=seq, blk=512)

        w_gate_pad = jnp.zeros((LANES, GLA_K_WIDTH), F32).at[:GLA_RANK].set(w_gate_up[l]).astype(BF16)
        o_gla = gla_mixer(h_in, w_gate_pad, b_gate[l], gla_out_norm[l], batch=batch, seq=seq, sb=1024)

        h = mixer_out_proj(o_mla, mla_out_norm[l], o_gla, w_out[l].astype(BF16), h, tm=512, tn=1024)

        qx = norm_matmul(h, norm_cross[l], w_cq[l].astype(BF16), k=d, tm=512, tn=1024, out_dtype=BF16,
                         name="xattn_q_proj")
        h = cross_attention(qx, mn_kv, w_co[l].astype(BF16), h, batch=batch, seq=seq,
                            mem_tokens=mem_tokens, tm=256)

        qp, hn = norm_matmul(h, norm_ffn[l], w_peer_q[l].astype(BF16), k=d, tm=512, tn=512, out_dtype=BF16,
                             emit_xn=True, name="peer_q_proj")
        keys = peer_sub_keys[l].reshape(PEER_HEADS * 2, PEER_KEYS, PEER_HALF).astype(BF16)
        eid, gate = peer_topk(qp, keys, tm=256)
        g = peer_gates(eid, gate, tm=128)
        peer = peer_dense(hn, peer_u[l].astype(BF16), g, peer_v[l].astype(BF16), tm=512, te=512)
        if l + 1 < norm_mix.shape[0]:
            h = h + peer
    out = final_norm(h, peer, norm_final, tm=256)
    return out.reshape(batch, seq, d)
```

```python
import functools
import math

import numpy as np
import jax
import jax.numpy as jnp
from jax import lax
from jax.experimental import pallas as pl
from jax.experimental.pallas import tpu as pltpu

F32 = jnp.float32
BF16 = jnp.bfloat16
I32 = jnp.int32

EPS = 1e-6
ROPE_THETA = 10000.0

MLA_HEADS = 16
MLA_Q_LORA = 1024
MLA_KV_LORA = 512
MLA_NOPE = 128
MLA_ROPE = 64
MLA_QK = MLA_NOPE + MLA_ROPE
MLA_V = 128
MLA_WIDTH = MLA_HEADS * MLA_V

GLA_HEADS = 4
GLA_HEAD_K = 256
GLA_HEAD_V = 512
GLA_K_WIDTH = GLA_HEADS * GLA_HEAD_K
GLA_V_WIDTH = GLA_HEADS * GLA_HEAD_V
GLA_RANK = 16
GLA_TAU = 16.0
GLA_CHUNK = 64

XATTN_HEADS = 4
XATTN_DIM = 256
XATTN_WIDTH = XATTN_HEADS * XATTN_DIM

PEER_HEADS = 8
PEER_KEYS = 128
PEER_HALF = 128
PEER_TOPK = 16
PEER_SLOTS = PEER_HEADS * PEER_TOPK
PEER_EXPERTS = PEER_KEYS * PEER_KEYS

LANES = 128
VMEM_LIMIT = 56 * 1024 * 1024

COL_CQ = 0
COL_CKV = COL_CQ + MLA_Q_LORA
COL_GQ = COL_CKV + MLA_KV_LORA
COL_GK = COL_GQ + GLA_K_WIDTH
COL_GV = COL_GK + GLA_K_WIDTH
COL_OG = COL_GV + GLA_V_WIDTH
COL_KR = COL_OG + GLA_V_WIDTH
COL_LR = COL_KR + LANES
IN_PAD = 8192

NEG = -1e30


def _cparams(*sem):
    return pltpu.CompilerParams(dimension_semantics=sem, vmem_limit_bytes=VMEM_LIMIT)


def _rms(x, g):
    ms = jnp.mean(x * x, axis=-1, keepdims=True)
    return x * lax.rsqrt(ms + EPS) * g


def _dot(a, b):
    return jnp.dot(a, b, preferred_element_type=F32)


def _dot_nt(a, b):
    return lax.dot_general(a, b, (((1,), (1,)), ((), ())), preferred_element_type=F32)


def _dot_tn(a, b):
    return lax.dot_general(a, b, (((0,), (0,)), ((), ())), preferred_element_type=F32)


def _norm_matmul_kernel(x_ref, g_ref, w_ref, o_ref, *rest, emit_xn):
    if emit_xn:
        xn_out_ref, xn_ref = rest
    else:
        (xn_ref,) = rest

    @pl.when(pl.program_id(1) == 0)
    def _():
        xn = _rms(x_ref[...].astype(F32), g_ref[...]).astype(BF16)
        xn_ref[...] = xn
        if emit_xn:
            xn_out_ref[...] = xn

    o_ref[...] = _dot(xn_ref[...], w_ref[...]).astype(o_ref.dtype)


def norm_matmul(x, gain, w, *, k, x_col_blk=0, tm, tn, out_dtype, emit_xn=False, name):
    m = x.shape[0]
    n = w.shape[1]
    tm = min(tm, m)
    tn = min(tn, n)
    out_shape = [jax.ShapeDtypeStruct((m, n), out_dtype)]
    out_specs = [pl.BlockSpec((tm, tn), lambda i, j: (i, j))]
    if emit_xn:
        out_shape.append(jax.ShapeDtypeStruct((m, k), BF16))
        out_specs.append(pl.BlockSpec((tm, k), lambda i, j: (i, 0)))
    res = pl.pallas_call(
        functools.partial(_norm_matmul_kernel, emit_xn=emit_xn),
        out_shape=out_shape,
        grid=(m // tm, n // tn),
        in_specs=[pl.BlockSpec((tm, k), lambda i, j: (i, x_col_blk)),
                  pl.BlockSpec((1, k), lambda i, j: (0, 0)),
                  pl.BlockSpec((k, tn), lambda i, j: (0, j))],
        out_specs=out_specs,
        scratch_shapes=[pltpu.VMEM((tm, k), BF16)],
        compiler_params=_cparams("parallel", "arbitrary"),
        name=name,
    )(x, gain.reshape(1, k).astype(F32), w)
    return res if emit_xn else res[0]


def _rope_pair(r, cosv, sinv):
    return r * cosv + pltpu.roll(r, 2 * 32, axis=1) * sinv


def _qproj_kernel(c_ref, g_ref, w_ref, cos_ref, sin_ref, o_ref, *, scale):
    xn = _rms(c_ref[...].astype(F32), g_ref[...]).astype(BF16)
    q = _dot(xn, w_ref[...])
    cosv = cos_ref[...]
    sinv = sin_ref[...]
    lane = lax.broadcasted_iota(I32, (1, LANES), 1)
    rope_base = MLA_HEADS * MLA_NOPE
    for j in range(MLA_HEADS // 2):
        r = q[:, rope_base + j * LANES: rope_base + (j + 1) * LANES]
        r = _rope_pair(r, cosv, sinv) * scale
        for p in range(2):
            h = 2 * j + p
            own = ((lane // 32) % 2) == p
            o_ref[:, h * 256: h * 256 + 128] = (q[:, h * 128:(h + 1) * 128] * scale).astype(o_ref.dtype)
            o_ref[:, h * 256 + 128: h * 256 + 256] = jnp.where(own, r, 0.0).astype(o_ref.dtype)


def mla_q_proj(h_in, gain, w_uq_perm, cos_t, sin_t, *, tm):
    m = h_in.shape[0]
    tm = min(tm, m)
    return pl.pallas_call(
        functools.partial(_qproj_kernel, scale=MLA_QK ** -0.5 * math.log2(math.e)),
        out_shape=jax.ShapeDtypeStruct((m, MLA_HEADS * 256), BF16),
        grid=(m // tm,),
        in_specs=[pl.BlockSpec((tm, MLA_Q_LORA), lambda i: (i, COL_CQ // MLA_Q_LORA)),
                  pl.BlockSpec((1, MLA_Q_LORA), lambda i: (0, 0)),
                  pl.BlockSpec(w_uq_perm.shape, lambda i: (0, 0)),
                  pl.BlockSpec((tm, LANES), lambda i: (i, 0)),
                  pl.BlockSpec((tm, LANES), lambda i: (i, 0))],
        out_specs=pl.BlockSpec((tm, MLA_HEADS * 256), lambda i: (i, 0)),
        compiler_params=_cparams("parallel"),
        name="mla_q_proj",
    )(h_in, gain.reshape(1, -1).astype(F32), w_uq_perm, cos_t, sin_t)


def _kvproj_kernel(c_ref, g_ref, w_ref, kr_ref, cos_ref, sin_ref, kv_ref, kro_ref):
    xn = _rms(c_ref[...].astype(F32), g_ref[...]).astype(BF16)
    kv_ref[...] = _dot(xn, w_ref[...]).astype(kv_ref.dtype)
    kro_ref[...] = _rope_pair(kr_ref[...].astype(F32), cos_ref[...], sin_ref[...]).astype(kro_ref.dtype)


def mla_kv_proj(h_in, gain, w_ukv, cos_t, sin_t, *, tm):
    m = h_in.shape[0]
    tm = min(tm, m)
    n = w_ukv.shape[1]
    return pl.pallas_call(
        _kvproj_kernel,
        out_shape=[jax.ShapeDtypeStruct((m, n), BF16), jax.ShapeDtypeStruct((m, LANES), BF16)],
        grid=(m // tm,),
        in_specs=[pl.BlockSpec((tm, MLA_KV_LORA), lambda i: (i, COL_CKV // MLA_KV_LORA)),
                  pl.BlockSpec((1, MLA_KV_LORA), lambda i: (0, 0)),
                  pl.BlockSpec(w_ukv.shape, lambda i: (0, 0)),
                  pl.BlockSpec((tm, LANES), lambda i: (i, COL_KR // LANES)),
                  pl.BlockSpec((tm, LANES), lambda i: (i, 0)),
                  pl.BlockSpec((tm, LANES), lambda i: (i, 0))],
        out_specs=[pl.BlockSpec((tm, n), lambda i: (i, 0)),
                   pl.BlockSpec((tm, LANES), lambda i: (i, 0))],
        compiler_params=_cparams("parallel"),
        name="mla_kv_proj",
    )(h_in, gain.reshape(1, -1).astype(F32), w_ukv, h_in, cos_t, sin_t)


ATTN_UNROLL = 4


def _mla_attn_kernel(q_ref, kv_ref, kr_ref, o_ref, s0_scr, s1_scr, m_scr, acc_scr, *, blk):
    qi = pl.program_id(2)
    q = q_ref[...]
    ones = jnp.ones((blk, LANES), BF16)
    m_scr[...] = jnp.full(m_scr.shape, NEG, F32)
    acc_scr[...] = jnp.zeros(acc_scr.shape, F32)

    def scores(j, s_scr):
        start = pl.multiple_of(j * blk, blk)
        k = jnp.concatenate([kv_ref[pl.ds(start, blk), :MLA_NOPE], kr_ref[pl.ds(start, blk), :]], axis=1)
        s_scr[...] = _dot_nt(q, k)

    def finish(j, s_scr, masked):
        start = pl.multiple_of(j * blk, blk)
        s = s_scr[...]
        if masked:
            row = lax.broadcasted_iota(I32, (blk, blk), 0)
            col = lax.broadcasted_iota(I32, (blk, blk), 1)
            s = jnp.where(col <= row, s, NEG)
        m = m_scr[...]
        m_new = jnp.maximum(m, jnp.max(s, axis=1, keepdims=True))
        alpha = jnp.exp2(m - m_new)
        p = jnp.exp2(s - jnp.concatenate([m_new] * (blk // LANES), axis=1)).astype(BF16)
        m_scr[...] = m_new
        v = jnp.concatenate([kv_ref[pl.ds(start, blk), MLA_NOPE:], ones], axis=1)
        acc_scr[...] = jnp.concatenate([alpha, alpha], axis=1) * acc_scr[...] + _dot(p, v)

    s_scr = (s0_scr, s1_scr)

    def run(j0, n):
        for u in range(n):
            scores(j0 + u + 1, s_scr[(u + 1) % 2])
            finish(j0 + u, s_scr[u % 2], False)

    scores(0, s0_scr)

    def body(t, carry):
        run(t * ATTN_UNROLL, ATTN_UNROLL)
        return carry

    lax.fori_loop(0, qi // ATTN_UNROLL, body, 0)
    done = (qi // ATTN_UNROLL) * ATTN_UNROLL
    n = ATTN_UNROLL // 2
    while n >= 1:
        @pl.when((qi & n) != 0)
        def _(done=done, n=n):
            run(done, n)

        done = done + (qi & n)
        n //= 2

    @pl.when(qi % 2 == 0)
    def _():
        finish(qi, s0_scr, True)

    @pl.when(qi % 2 == 1)
    def _():
        finish(qi, s1_scr, True)

    acc = acc_scr[...]
    o_ref[...] = (acc[:, :MLA_V] / acc[:, MLA_V:]).astype(o_ref.dtype)


def mla_attention(q_full, kv, k_rope, *, batch, seq, blk):
    blk = min(blk, seq)
    nq = seq // blk
    m = batch * seq
    return pl.pallas_call(
        functools.partial(_mla_attn_kernel, blk=blk),
        out_shape=jax.ShapeDtypeStruct((m, MLA_WIDTH), BF16),
        grid=(batch, MLA_HEADS, nq),
        in_specs=[pl.BlockSpec((blk, 256), lambda b, h, i: (b * nq + i, h)),
                  pl.BlockSpec((seq, 256), lambda b, h, i: (b, h)),
                  pl.BlockSpec((seq, LANES), lambda b, h, i: (b, 0))],
        out_specs=pl.BlockSpec((blk, MLA_V), lambda b, h, i: (b * nq + i, h)),
        scratch_shapes=[pltpu.VMEM((blk, blk), F32), pltpu.VMEM((blk, blk), F32),
                        pltpu.VMEM((blk, LANES), F32), pltpu.VMEM((blk, 2 * MLA_V), F32)],
        compiler_params=_cparams("parallel", "parallel", "arbitrary"),
        name="mla_attention",
    )(q_full, kv, k_rope)


def _gla_kernel(q_ref, k_ref, v_ref, og_ref, lr_ref, wg_ref, bg_ref, gn_ref, o_ref, st_ref, *, sb, c):
    @pl.when(pl.program_id(2) == 0)
    def _():
        st_ref[...] = jnp.zeros_like(st_ref)

    z = _dot(lr_ref[...], wg_ref[...]) + bg_ref[...]
    log_a = (jnp.minimum(z, 0.0) - jnp.log(1.0 + jnp.exp(-jnp.abs(z)))) * (1.0 / GLA_TAU)
    rows = lax.broadcasted_iota(I32, (c, c), 0)
    cols = lax.broadcasted_iota(I32, (c, c), 1)
    causal = cols <= rows
    tril = causal.astype(F32)
    gn = gn_ref[...]
    scale = GLA_HEAD_K ** -0.5

    for n in range(sb // c):
        sl = slice(n * c, (n + 1) * c)
        g = log_a[sl, :]
        b = jnp.dot(tril, g, preferred_element_type=F32, precision=lax.Precision.HIGHEST)
        b_last = b[c - 1:c, :]
        q = q_ref[sl, :].astype(F32)
        k = k_ref[sl, :].astype(F32)
        v = v_ref[sl, :]
        qe = (q * scale * jnp.exp(b)).astype(BF16)
        ke = (k * jnp.exp(-b)).astype(BF16)
        kd = (k * jnp.exp(b_last - b)).astype(BF16)
        att = jnp.where(causal, _dot_nt(qe, ke), 0.0).astype(BF16)
        st = st_ref[...]
        o = _dot(att, v) + _dot_nt(qe, st.astype(BF16))
        st_ref[...] = st * jnp.exp(b_last) + _dot_tn(v, kd)
        og = og_ref[sl, :].astype(F32)
        o = _rms(o, gn) * (og * (1.0 / (1.0 + jnp.exp(-og))))
        o_ref[sl, :] = o.astype(o_ref.dtype)


def gla_mixer(h_in, w_gate_pad, b_gate, gla_out_norm, *, batch, seq, sb):
    sb = min(sb, seq)
    nsb = seq // sb
    m = batch * seq
    return pl.pallas_call(
        functools.partial(_gla_kernel, sb=sb, c=GLA_CHUNK),
        out_shape=jax.ShapeDtypeStruct((m, GLA_V_WIDTH), BF16),
        grid=(batch, GLA_HEADS, nsb),
        in_specs=[pl.BlockSpec((sb, GLA_HEAD_K), lambda b, h, s: (b * nsb + s, COL_GQ // GLA_HEAD_K + h)),
                  pl.BlockSpec((sb, GLA_HEAD_K), lambda b, h, s: (b * nsb + s, COL_GK // GLA_HEAD_K + h)),
                  pl.BlockSpec((sb, GLA_HEAD_V), lambda b, h, s: (b * nsb + s, COL_GV // GLA_HEAD_V + h)),
                  pl.BlockSpec((sb, GLA_HEAD_V), lambda b, h, s: (b * nsb + s, COL_OG // GLA_HEAD_V + h)),
                  pl.BlockSpec((sb, LANES), lambda b, h, s: (b * nsb + s, COL_LR // LANES)),
                  pl.BlockSpec((LANES, GLA_HEAD_K), lambda b, h, s: (0, h)),
                  pl.BlockSpec((1, GLA_HEAD_K), lambda b, h, s: (0, h)),
                  pl.BlockSpec((1, GLA_HEAD_V), lambda b, h, s: (0, 0))],
        out_specs=pl.BlockSpec((sb, GLA_HEAD_V), lambda b, h, s: (b * nsb + s, h)),
        scratch_shapes=[pltpu.VMEM((GLA_HEAD_V, GLA_HEAD_K), F32)],
        compiler_params=_cparams("parallel", "parallel", "arbitrary"),
        name="gla_mixer",
    )(h_in, h_in, h_in, h_in, h_in, w_gate_pad, b_gate.reshape(1, -1).astype(F32),
      gla_out_norm.reshape(1, -1).astype(F32))


def _out_proj_kernel(om_ref, g_ref, og_ref, w_ref, x_ref, o_ref, mix_ref):
    @pl.when(pl.program_id(1) == 0)
    def _():
        mix_ref[:, :MLA_WIDTH] = _rms(om_ref[...].astype(F32), g_ref[...]).astype(BF16)
        mix_ref[:, MLA_WIDTH:] = og_ref[...]

    o_ref[...] = x_ref[...] + _dot(mix_ref[...], w_ref[...])


def mixer_out_proj(o_mla, mla_out_norm, o_gla, w_out, x, *, tm, tn):
    m, d = x.shape
    tm = min(tm, m)
    tn = min(tn, d)
    kmix = MLA_WIDTH + GLA_V_WIDTH
    return pl.pallas_call(
        _out_proj_kernel,
        out_shape=jax.ShapeDtypeStruct((m, d), F32),
        grid=(m // tm, d // tn),
        in_specs=[pl.BlockSpec((tm, MLA_WIDTH), lambda i, j: (i, 0)),
                  pl.BlockSpec((1, MLA_WIDTH), lambda i, j: (0, 0)),
                  pl.BlockSpec((tm, GLA_V_WIDTH), lambda i, j: (i, 0)),
                  pl.BlockSpec((kmix, tn), lambda i, j: (0, j)),
                  pl.BlockSpec((tm, tn), lambda i, j: (i, j))],
        out_specs=pl.BlockSpec((tm, tn), lambda i, j: (i, j)),
        scratch_shapes=[pltpu.VMEM((tm, kmix), BF16)],
        compiler_params=_cparams("parallel", "arbitrary"),
        name="mixer_out_proj",
    )(o_mla, mla_out_norm.reshape(1, -1).astype(F32), o_gla, w_out, x)


def _xattn_kernel(q_ref, kv_ref, w_ref, h_ref, o_ref):
    scale = XATTN_DIM ** -0.5
    outs = []
    for h in range(XATTN_HEADS):
        q = q_ref[:, h * XATTN_DIM:(h + 1) * XATTN_DIM]
        k = kv_ref[:, h * XATTN_DIM:(h + 1) * XATTN_DIM]
        v = kv_ref[:, XATTN_WIDTH + h * XATTN_DIM: XATTN_WIDTH + (h + 1) * XATTN_DIM]
        s = _dot_nt(q, k) * scale
        p = jnp.exp(s - jnp.max(s, axis=1, keepdims=True))
        p = p / jnp.sum(p, axis=1, keepdims=True)
        outs.append(_dot(p.astype(BF16), v).astype(BF16))
    o = jnp.concatenate(outs, axis=1)
    o_ref[...] = h_ref[...] + _dot(o, w_ref[...])


def cross_attention(qx, kvm, w_co, h1, *, batch, seq, mem_tokens, tm):
    m, d = h1.shape
    tm = min(tm, seq)
    per_b = seq // tm
    return pl.pallas_call(
        _xattn_kernel,
        out_shape=jax.ShapeDtypeStruct((m, d), F32),
        grid=(m // tm,),
        in_specs=[pl.BlockSpec((tm, XATTN_WIDTH), lambda i: (i, 0)),
                  pl.BlockSpec((mem_tokens, 2 * XATTN_WIDTH), lambda i: (i // per_b, 0)),
                  pl.BlockSpec(w_co.shape, lambda i: (0, 0)),
                  pl.BlockSpec((tm, d), lambda i: (i, 0))],
        out_specs=pl.BlockSpec((tm, d), lambda i: (i, 0)),
        compiler_params=_cparams("parallel"),
        name="cross_attention",
    )(qx, kvm, w_co, h1)


INT_MIN = -2 ** 31


def _order_key(x):
    b = lax.bitcast_convert_type(x, I32)
    return b ^ (lax.shift_right_arithmetic(b, 31) & 0x7FFFFFFF)


def _order_key_inv(k):
    return lax.bitcast_convert_type(k ^ (lax.shift_right_arithmetic(k, 31) & 0x7FFFFFFF), F32)


def _pack_keys(x, bits):
    low = (1 << bits) - 1
    row = lax.broadcasted_iota(I32, x.shape, 0)
    return (_order_key(x) & ~low) | (low - row)


def _top16_rows(s, bits):
    low = (1 << bits) - 1
    kp = _pack_keys(s, bits)
    vals, idxs = [], []
    for _ in range(PEER_TOPK):
        mx = jnp.max(kp, axis=0, keepdims=True)
        vals.append(_order_key_inv(mx & ~low))
        idxs.append(low - (mx & low))
        kp = jnp.where(kp == mx, INT_MIN, kp)
    return vals, idxs


_PAIRS = [(a, b) for a in range(PEER_TOPK) for b in range(PEER_TOPK) if (a + 1) * (b + 1) <= PEER_TOPK]
_PAIR_ROWS = -(-len(_PAIRS) // 8) * 8


def _peer_topk_kernel(q_ref, keys_ref, eid_ref, gate_ref, eid_t, gate_t, *, tm):
    def head(h, carry):
        tops = []
        for p in range(2):
            col = pl.multiple_of((2 * h + p) * PEER_HALF, PEER_HALF)
            qh = q_ref[:, pl.ds(col, PEER_HALF)]
            s = _dot_nt(keys_ref[2 * h + p], qh)
            tops.append(_top16_rows(s, 7))
        (s1, i1), (s2, i2) = tops
        cand_s = [s1[a] + s2[b] for a, b in _PAIRS]
        cand_e = [i1[a] * PEER_KEYS + i2[b] for a, b in _PAIRS]
        pad = _PAIR_ROWS - len(_PAIRS)
        cs = jnp.concatenate(cand_s + [jnp.full((pad, tm), -jnp.inf, F32)], axis=0)
        ce = jnp.concatenate(cand_e + [jnp.zeros((pad, tm), I32)], axis=0)
        low = 63
        kp = _pack_keys(cs, 6)
        sel_s, sel_e = [], []
        for _ in range(PEER_TOPK):
            mx = jnp.max(kp, axis=0, keepdims=True)
            hit = kp == mx
            sel_s.append(_order_key_inv(mx & ~low))
            sel_e.append(jnp.max(jnp.where(hit, ce, -1), axis=0, keepdims=True))
            kp = jnp.where(hit, INT_MIN, kp)
        top_s = jnp.concatenate(sel_s, axis=0)
        top_e = jnp.concatenate(sel_e, axis=0)
        w = jnp.exp(top_s - top_s[0:1, :])
        gate = w / jnp.sum(w, axis=0, keepdims=True)
        row = pl.multiple_of(h * PEER_TOPK, PEER_TOPK)
        eid_t[pl.ds(row, PEER_TOPK), :] = top_e
        gate_t[pl.ds(row, PEER_TOPK), :] = gate
        return carry

    lax.fori_loop(0, PEER_HEADS, head, 0)
    eid_ref[...] = eid_t[...].T
    gate_ref[...] = gate_t[...].T


def peer_topk(qp, keys, *, tm):
    m = qp.shape[0]
    tm = min(tm, m)
    return pl.pallas_call(
        functools.partial(_peer_topk_kernel, tm=tm),
        out_shape=[jax.ShapeDtypeStruct((m, PEER_SLOTS), I32), jax.ShapeDtypeStruct((m, PEER_SLOTS), F32)],
        grid=(m // tm,),
        in_specs=[pl.BlockSpec((tm, qp.shape[1]), lambda i: (i, 0)),
                  pl.BlockSpec(keys.shape, lambda i: (0, 0, 0))],
        out_specs=[pl.BlockSpec((tm, PEER_SLOTS), lambda i: (i, 0)),
                   pl.BlockSpec((tm, PEER_SLOTS), lambda i: (i, 0))],
        scratch_shapes=[pltpu.VMEM((PEER_SLOTS, tm), I32), pltpu.VMEM((PEER_SLOTS, tm), F32)],
        compiler_params=_cparams("parallel"),
        name="peer_topk",
    )(qp, keys)


GATE_UNROLL = 8


def _peer_gates_kernel(eid_ref, gate_ref, o_ref, g3_ref, *, tm, pitch):
    sub = lax.broadcasted_iota(I32, (PEER_KEYS, PEER_SLOTS), 0)

    def token(t, carry):
        e = eid_ref[pl.ds(t, 1), :]
        g = gate_ref[pl.ds(t, 1), :]
        hi = lax.shift_right_logical(e, 7)
        lo = jnp.bitwise_and(e, PEER_KEYS - 1)
        a1 = jnp.where(sub == hi, g, 0.0).astype(BF16)
        a2 = jnp.where(sub == lo, 1.0, 0.0).astype(BF16)
        g3_ref[pl.ds(t, PEER_KEYS, stride=pitch), :] = _dot_nt(a1, a2)
        return carry

    lax.fori_loop(0, tm, token, 0, unroll=GATE_UNROLL)
    for e1 in range(PEER_KEYS):
        o_ref[:, e1 * PEER_KEYS:(e1 + 1) * PEER_KEYS] = g3_ref[e1 * pitch:e1 * pitch + tm, :].astype(o_ref.dtype)


def peer_gates(eid, gate, *, tm):
    m = eid.shape[0]
    tm = min(tm, m)
    pitch = tm + 8
    return pl.pallas_call(
        functools.partial(_peer_gates_kernel, tm=tm, pitch=pitch),
        out_shape=jax.ShapeDtypeStruct((m, PEER_EXPERTS), BF16),
        grid=(m // tm,),
        in_specs=[pl.BlockSpec((tm, PEER_SLOTS), lambda i: (i, 0)),
                  pl.BlockSpec((tm, PEER_SLOTS), lambda i: (i, 0))],
        out_specs=pl.BlockSpec((tm, PEER_EXPERTS), lambda i: (i, 0)),
        scratch_shapes=[pltpu.VMEM((PEER_KEYS * pitch, PEER_KEYS), F32)],
        compiler_params=_cparams("parallel"),
        name="peer_gates",
    )(eid, gate)


def _peer_dense_kernel(x_ref, u_ref, g_ref, v_ref, h_ref, gn_ref, o_ref, *, final_norm):
    j = pl.program_id(1)

    @pl.when(j == 0)
    def _():
        o_ref[...] = jnp.zeros_like(o_ref)

    s = _dot_nt(x_ref[...], u_ref[...])
    a = 0.5 * s * (1.0 + lax.erf(s * (2.0 ** -0.5)))
    a = (a * g_ref[...].astype(F32)).astype(BF16)
    o_ref[...] += _dot(a, v_ref[...])

    @pl.when(j == pl.num_programs(1) - 1)
    def _():
        y = h_ref[...] + o_ref[...]
        o_ref[...] = _rms(y, gn_ref[...]) if final_norm else y


def peer_dense(xn, u, g, v, h, gain, *, final_norm, tm, te):
    m, d = xn.shape
    e = u.shape[0]
    tm = min(tm, m)
    once = pl.Buffered(1)
    return pl.pallas_call(
        functools.partial(_peer_dense_kernel, final_norm=final_norm),
        out_shape=jax.ShapeDtypeStruct((m, d), F32),
        grid=(m // tm, e // te),
        in_specs=[pl.BlockSpec((tm, d), lambda i, j: (i, 0), pipeline_mode=once),
                  pl.BlockSpec((te, d), lambda i, j: (j, 0)),
                  pl.BlockSpec((tm, te), lambda i, j: (i, j)),
                  pl.BlockSpec((te, d), lambda i, j: (j, 0)),
                  pl.BlockSpec((tm, d), lambda i, j: (i, 0), pipeline_mode=once),
                  pl.BlockSpec((1, d), lambda i, j: (0, 0))],
        out_specs=pl.BlockSpec((tm, d), lambda i, j: (i, 0)),
        compiler_params=_cparams("parallel", "arbitrary"),
        name="peer_dense",
    )(xn, u, g, v, h, gain.reshape(1, d).astype(F32))


def _w_in_columns():
    src = {}
    off = 0
    for name, width in (("cq", MLA_Q_LORA), ("ckv", MLA_KV_LORA), ("kr", MLA_ROPE), ("gq", GLA_K_WIDTH),
                        ("gk", GLA_K_WIDTH), ("gv", GLA_V_WIDTH), ("lr", GLA_RANK), ("og", GLA_V_WIDTH)):
        src[name] = np.arange(off, off + width)
        off += width
    cols = np.full((IN_PAD,), -1, np.int64)
    for name, start in (("cq", COL_CQ), ("ckv", COL_CKV), ("gq", COL_GQ), ("gk", COL_GK),
                        ("gv", COL_GV), ("og", COL_OG), ("lr", COL_LR)):
        cols[start:start + len(src[name])] = src[name]
    half = MLA_ROPE // 2
    x1, x2 = src["kr"][:half], src["kr"][half:]
    cols[COL_KR:COL_KR + LANES] = np.concatenate([x1, x1, x2, x2])
    return cols


def _w_uq_columns():
    half = MLA_ROPE // 2
    nope = [h * MLA_QK + d for h in range(MLA_HEADS) for d in range(MLA_NOPE)]
    rope = []
    for j in range(MLA_HEADS // 2):
        a, b = 2 * j, 2 * j + 1
        for part in (0, 1):
            for h in (a, b):
                rope += [h * MLA_QK + MLA_NOPE + part * half + r for r in range(half)]
    return np.array(nope + rope, np.int64)


def _take_cols(w, cols, dtype):
    pieces = []
    start = 0
    for i in range(1, len(cols) + 1):
        if i == len(cols) or cols[i] != (cols[i - 1] + 1 if cols[i - 1] >= 0 else -1):
            if cols[start] < 0:
                pieces.append(jnp.zeros((w.shape[0], i - start), dtype))
            else:
                pieces.append(w[:, int(cols[start]):int(cols[start]) + i - start].astype(dtype))
            start = i
    return jnp.concatenate(pieces, axis=1)


def _rope_tables(positions):
    half = MLA_ROPE // 2
    inv_freq = ROPE_THETA ** (-jnp.arange(0, MLA_ROPE, 2, dtype=F32) / MLA_ROPE)
    ang = positions.astype(F32).reshape(-1, 1) * inv_freq
    cos = jnp.cos(ang)
    sin = jnp.sin(ang)
    return jnp.tile(cos, (1, 4)), jnp.concatenate([-sin, -sin, sin, sin], axis=1)


def kernel(x, mem, positions, norm_mem, norm_mix, w_in, mla_q_norm, w_uq, mla_kv_norm, w_ukv, mla_out_norm,
           w_gate_up, b_gate, gla_out_norm, w_out, norm_cross, w_cq, w_ck, w_cv, w_co, norm_ffn, w_peer_q,
           peer_sub_keys, peer_u, peer_v, norm_final):
    batch, seq, d = x.shape
    mem_tokens = mem.shape[1]
    m = batch * seq
    h = x.reshape(m, d)
    cos_t, sin_t = _rope_tables(positions)

    mn_kv = norm_matmul(mem.reshape(batch * mem_tokens, d), norm_mem,
                        jnp.concatenate([w_ck[0], w_cv[0]], axis=1).astype(BF16),
                        k=d, tm=512, tn=1024, out_dtype=BF16, name="mem_kv_proj")

    for l in range(norm_mix.shape[0]):
        w_in_p = _take_cols(w_in[l], _w_in_columns(), BF16)
        h_in = norm_matmul(h, norm_mix[l], w_in_p, k=d, tm=512, tn=1024, out_dtype=BF16, name="in_proj")

        w_uq_p = _take_cols(w_uq[l], _w_uq_columns(), BF16)
        q_full = mla_q_proj(h_in, mla_q_norm[l], w_uq_p, cos_t, sin_t, tm=512)
        kv, k_rope = mla_kv_proj(h_in, mla_kv_norm[l], w_ukv[l].astype(BF16), cos_t, sin_t, tm=512)
        o_mla = mla_attention(q_full, kv, k_rope, batch=batch, seq=seq, blk=512)

        w_gate_pad = jnp.zeros((LANES, GLA_K_WIDTH), F32).at[:GLA_RANK].set(w_gate_up[l]).astype(BF16)
        o_gla = gla_mixer(h_in, w_gate_pad, b_gate[l], gla_out_norm[l], batch=batch, seq=seq, sb=1024)

        h = mixer_out_proj(o_mla, mla_out_norm[l], o_gla, w_out[l].astype(BF16), h, tm=512, tn=1024)

        qx = norm_matmul(h, norm_cross[l], w_cq[l].astype(BF16), k=d, tm=512, tn=1024, out_dtype=BF16,
                         name="xattn_q_proj")
        h = cross_attention(qx, mn_kv, w_co[l].astype(BF16), h, batch=batch, seq=seq,
                            mem_tokens=mem_tokens, tm=256)

        qp, hn = norm_matmul(h, norm_ffn[l], w_peer_q[l].astype(BF16), k=d, tm=512, tn=512, out_dtype=BF16,
                             emit_xn=True, name="peer_q_proj")
        keys = peer_sub_keys[l].reshape(PEER_HEADS * 2, PEER_KEYS, PEER_HALF).astype(BF16)
        eid, gate = peer_topk(qp, keys, tm=256)
        g = peer_gates(eid, gate, tm=128)
        h = peer_dense(hn, peer_u[l].astype(BF16), g, peer_v[l].astype(BF16), h, norm_final,
                       final_norm=(l + 1 == norm_mix.shape[0]), tm=512, te=512)
    return h.reshape(batch, seq, d)
```

```python
import functools
import math

import numpy as np
import jax
import jax.numpy as jnp
from jax import lax
from jax.experimental import pallas as pl
from jax.experimental.pallas import tpu as pltpu

F32 = jnp.float32
BF16 = jnp.bfloat16
F8 = jnp.float8_e4m3fn
I32 = jnp.int32
FP8_AMAX = 256.0

EPS = 1e-6
ROPE_THETA = 10000.0

MLA_HEADS = 16
MLA_Q_LORA = 1024
MLA_KV_LORA = 512
MLA_NOPE = 128
MLA_ROPE = 64
MLA_QK = MLA_NOPE + MLA_ROPE
MLA_V = 128
MLA_WIDTH = MLA_HEADS * MLA_V

GLA_HEADS = 4
GLA_HEAD_K = 256
GLA_HEAD_V = 512
GLA_K_WIDTH = GLA_HEADS * GLA_HEAD_K
GLA_V_WIDTH = GLA_HEADS * GLA_HEAD_V
GLA_RANK = 16
GLA_TAU = 16.0
GLA_CHUNK = 64

XATTN_HEADS = 4
XATTN_DIM = 256
XATTN_WIDTH = XATTN_HEADS * XATTN_DIM

PEER_HEADS = 8
PEER_KEYS = 128
PEER_HALF = 128
PEER_TOPK = 16
PEER_SLOTS = PEER_HEADS * PEER_TOPK
PEER_EXPERTS = PEER_KEYS * PEER_KEYS

LANES = 128
VMEM_LIMIT = 56 * 1024 * 1024

COL_CQ = 0
COL_CKV = COL_CQ + MLA_Q_LORA
COL_GQ = COL_CKV + MLA_KV_LORA
COL_GK = COL_GQ + GLA_K_WIDTH
COL_GV = COL_GK + GLA_K_WIDTH
COL_OG = COL_GV + GLA_V_WIDTH
COL_KR = COL_OG + GLA_V_WIDTH
COL_LR = COL_KR + LANES
IN_PAD = 8192

NEG = -1e30


def _cparams(*sem):
    return pltpu.CompilerParams(dimension_semantics=sem, vmem_limit_bytes=VMEM_LIMIT)


def _rms(x, g):
    ms = jnp.mean(x * x, axis=-1, keepdims=True)
    return x * lax.rsqrt(ms + EPS) * g


def _dot(a, b):
    return jnp.dot(a, b, preferred_element_type=F32)


def _dot_nt(a, b):
    return lax.dot_general(a, b, (((1,), (1,)), ((), ())), preferred_element_type=F32)


def _dot_tn(a, b):
    return lax.dot_general(a, b, (((0,), (0,)), ((), ())), preferred_element_type=F32)


def _quantize_rows(x):
    amax = jnp.max(jnp.abs(x), axis=-1, keepdims=True)
    scale = jnp.where(amax > 0.0, amax * (1.0 / FP8_AMAX), 1.0)
    return (x * (1.0 / scale)).astype(F8), scale


def _norm_matmul_kernel(x_ref, g_ref, w_ref, o_ref, *rest, emit_xq):
    if emit_xq:
        xq_ref, sx_ref, xn_ref = rest
    else:
        (xn_ref,) = rest

    @pl.when(pl.program_id(1) == 0)
    def _():
        xn = _rms(x_ref[...].astype(F32), g_ref[...])
        xn_ref[...] = xn.astype(BF16)
        if emit_xq:
            xq_ref[...], sx_ref[...] = _quantize_rows(xn)

    o_ref[...] = _dot(xn_ref[...], w_ref[...]).astype(o_ref.dtype)


def norm_matmul(x, gain, w, *, k, x_col_blk=0, tm, tn, out_dtype, emit_xq=False, name):
    m = x.shape[0]
    n = w.shape[1]
    tm = min(tm, m)
    tn = min(tn, n)
    out_shape = [jax.ShapeDtypeStruct((m, n), out_dtype)]
    out_specs = [pl.BlockSpec((tm, tn), lambda i, j: (i, j))]
    if emit_xq:
        out_shape += [jax.ShapeDtypeStruct((m, k), F8), jax.ShapeDtypeStruct((m, 1), F32)]
        out_specs += [pl.BlockSpec((tm, k), lambda i, j: (i, 0)), pl.BlockSpec((tm, 1), lambda i, j: (i, 0))]
    res = pl.pallas_call(
        functools.partial(_norm_matmul_kernel, emit_xq=emit_xq),
        out_shape=out_shape,
        grid=(m // tm, n // tn),
        in_specs=[pl.BlockSpec((tm, k), lambda i, j: (i, x_col_blk)),
                  pl.BlockSpec((1, k), lambda i, j: (0, 0)),
                  pl.BlockSpec((k, tn), lambda i, j: (0, j))],
        out_specs=out_specs,
        scratch_shapes=[pltpu.VMEM((tm, k), BF16)],
        compiler_params=_cparams("parallel", "arbitrary"),
        name=name,
    )(x, gain.reshape(1, k).astype(F32), w)
    return res if emit_xq else res[0]


def _quantize_kernel(w_ref, q_ref, s_ref):
    q_ref[...], s_ref[...] = _quantize_rows(w_ref[...])


def quantize_rows(w, *, tr, name):
    r, d = w.shape
    return pl.pallas_call(
        _quantize_kernel,
        out_shape=[jax.ShapeDtypeStruct((r, d), F8), jax.ShapeDtypeStruct((r, 1), F32)],
        grid=(r // tr,),
        in_specs=[pl.BlockSpec((tr, d), lambda i: (i, 0))],
        out_specs=[pl.BlockSpec((tr, d), lambda i: (i, 0)), pl.BlockSpec((tr, 1), lambda i: (i, 0))],
        compiler_params=_cparams("parallel"),
        name=name,
    )(w)


def _rope_pair(r, cosv, sinv):
    return r * cosv + pltpu.roll(r, 2 * 32, axis=1) * sinv


def _qproj_kernel(c_ref, g_ref, w_ref, cos_ref, sin_ref, o_ref, *, scale):
    xn = _rms(c_ref[...].astype(F32), g_ref[...]).astype(BF16)
    q = _dot(xn, w_ref[...])
    cosv = cos_ref[...]
    sinv = sin_ref[...]
    lane = lax.broadcasted_iota(I32, (1, LANES), 1)
    rope_base = MLA_HEADS * MLA_NOPE
    for j in range(MLA_HEADS // 2):
        r = q[:, rope_base + j * LANES: rope_base + (j + 1) * LANES]
        r = _rope_pair(r, cosv, sinv) * scale
        for p in range(2):
            h = 2 * j + p
            own = ((lane // 32) % 2) == p
            o_ref[:, h * 256: h * 256 + 128] = (q[:, h * 128:(h + 1) * 128] * scale).astype(o_ref.dtype)
            o_ref[:, h * 256 + 128: h * 256 + 256] = jnp.where(own, r, 0.0).astype(o_ref.dtype)


def mla_q_proj(h_in, gain, w_uq_perm, cos_t, sin_t, *, tm):
    m = h_in.shape[0]
    tm = min(tm, m)
    return pl.pallas_call(
        functools.partial(_qproj_kernel, scale=MLA_QK ** -0.5 * math.log2(math.e)),
        out_shape=jax.ShapeDtypeStruct((m, MLA_HEADS * 256), BF16),
        grid=(m // tm,),
        in_specs=[pl.BlockSpec((tm, MLA_Q_LORA), lambda i: (i, COL_CQ // MLA_Q_LORA)),
                  pl.BlockSpec((1, MLA_Q_LORA), lambda i: (0, 0)),
                  pl.BlockSpec(w_uq_perm.shape, lambda i: (0, 0)),
                  pl.BlockSpec((tm, LANES), lambda i: (i, 0)),
                  pl.BlockSpec((tm, LANES), lambda i: (i, 0))],
        out_specs=pl.BlockSpec((tm, MLA_HEADS * 256), lambda i: (i, 0)),
        compiler_params=_cparams("parallel"),
        name="mla_q_proj",
    )(h_in, gain.reshape(1, -1).astype(F32), w_uq_perm, cos_t, sin_t)


def _kvproj_kernel(c_ref, g_ref, w_ref, kr_ref, cos_ref, sin_ref, kv_ref, kro_ref):
    xn = _rms(c_ref[...].astype(F32), g_ref[...]).astype(BF16)
    kv_ref[...] = _dot(xn, w_ref[...]).astype(kv_ref.dtype)
    kro_ref[...] = _rope_pair(kr_ref[...].astype(F32), cos_ref[...], sin_ref[...]).astype(kro_ref.dtype)


def mla_kv_proj(h_in, gain, w_ukv, cos_t, sin_t, *, tm):
    m = h_in.shape[0]
    tm = min(tm, m)
    n = w_ukv.shape[1]
    return pl.pallas_call(
        _kvproj_kernel,
        out_shape=[jax.ShapeDtypeStruct((m, n), BF16), jax.ShapeDtypeStruct((m, LANES), BF16)],
        grid=(m // tm,),
        in_specs=[pl.BlockSpec((tm, MLA_KV_LORA), lambda i: (i, COL_CKV // MLA_KV_LORA)),
                  pl.BlockSpec((1, MLA_KV_LORA), lambda i: (0, 0)),
                  pl.BlockSpec(w_ukv.shape, lambda i: (0, 0)),
                  pl.BlockSpec((tm, LANES), lambda i: (i, COL_KR // LANES)),
                  pl.BlockSpec((tm, LANES), lambda i: (i, 0)),
                  pl.BlockSpec((tm, LANES), lambda i: (i, 0))],
        out_specs=[pl.BlockSpec((tm, n), lambda i: (i, 0)),
                   pl.BlockSpec((tm, LANES), lambda i: (i, 0))],
        compiler_params=_cparams("parallel"),
        name="mla_kv_proj",
    )(h_in, gain.reshape(1, -1).astype(F32), w_ukv, h_in, cos_t, sin_t)


ATTN_UNROLL = 4


def _mla_attn_kernel(q_ref, kv_ref, kr_ref, o_ref, s0_scr, s1_scr, m_scr, acc_scr, *, blk):
    qi = pl.program_id(2)
    q = q_ref[...]
    ones = jnp.ones((blk, LANES), BF16)
    m_scr[...] = jnp.full(m_scr.shape, NEG, F32)
    acc_scr[...] = jnp.zeros(acc_scr.shape, F32)

    def scores(j, s_scr):
        start = pl.multiple_of(j * blk, blk)
        k = jnp.concatenate([kv_ref[pl.ds(start, blk), :MLA_NOPE], kr_ref[pl.ds(start, blk), :]], axis=1)
        s_scr[...] = _dot_nt(q, k)

    def finish(j, s_scr, masked):
        start = pl.multiple_of(j * blk, blk)
        s = s_scr[...]
        if masked:
            row = lax.broadcasted_iota(I32, (blk, blk), 0)
            col = lax.broadcasted_iota(I32, (blk, blk), 1)
            s = jnp.where(col <= row, s, NEG)
        m = m_scr[...]
        m_new = jnp.maximum(m, jnp.max(s, axis=1, keepdims=True))
        alpha = jnp.exp2(m - m_new)
        p = jnp.exp2(s - jnp.concatenate([m_new] * (blk // LANES), axis=1)).astype(BF16)
        m_scr[...] = m_new
        v = jnp.concatenate([kv_ref[pl.ds(start, blk), MLA_NOPE:], ones], axis=1)
        acc_scr[...] = jnp.concatenate([alpha, alpha], axis=1) * acc_scr[...] + _dot(p, v)

    s_scr = (s0_scr, s1_scr)

    def run(j0, n):
        for u in range(n):
            scores(j0 + u + 1, s_scr[(u + 1) % 2])
            finish(j0 + u, s_scr[u % 2], False)

    scores(0, s0_scr)

    def body(t, carry):
        run(t * ATTN_UNROLL, ATTN_UNROLL)
        return carry

    lax.fori_loop(0, qi // ATTN_UNROLL, body, 0)
    done = (qi // ATTN_UNROLL) * ATTN_UNROLL
    n = ATTN_UNROLL // 2
    while n >= 1:
        @pl.when((qi & n) != 0)
        def _(done=done, n=n):
            run(done, n)

        done = done + (qi & n)
        n //= 2

    @pl.when(qi % 2 == 0)
    def _():
        finish(qi, s0_scr, True)

    @pl.when(qi % 2 == 1)
    def _():
        finish(qi, s1_scr, True)

    acc = acc_scr[...]
    o_ref[...] = (acc[:, :MLA_V] / acc[:, MLA_V:]).astype(o_ref.dtype)


def mla_attention(q_full, kv, k_rope, *, batch, seq, blk):
    blk = min(blk, seq)
    nq = seq // blk
    m = batch * seq
    return pl.pallas_call(
        functools.partial(_mla_attn_kernel, blk=blk),
        out_shape=jax.ShapeDtypeStruct((m, MLA_WIDTH), BF16),
        grid=(batch, MLA_HEADS, nq),
        in_specs=[pl.BlockSpec((blk, 256), lambda b, h, i: (b * nq + i, h)),
                  pl.BlockSpec((seq, 256), lambda b, h, i: (b, h)),
                  pl.BlockSpec((seq, LANES), lambda b, h, i: (b, 0))],
        out_specs=pl.BlockSpec((blk, MLA_V), lambda b, h, i: (b * nq + i, h)),
        scratch_shapes=[pltpu.VMEM((blk, blk), F32), pltpu.VMEM((blk, blk), F32),
                        pltpu.VMEM((blk, LANES), F32), pltpu.VMEM((blk, 2 * MLA_V), F32)],
        compiler_params=_cparams("parallel", "parallel", "arbitrary"),
        name="mla_attention",
    )(q_full, kv, k_rope)


def _gla_kernel(q_ref, k_ref, v_ref, og_ref, lr_ref, wg_ref, bg_ref, gn_ref, o_ref, st_ref, *, sb, c):
    @pl.when(pl.program_id(2) == 0)
    def _():
        st_ref[...] = jnp.zeros_like(st_ref)

    z = _dot(lr_ref[...], wg_ref[...]) + bg_ref[...]
    log_a = (jnp.minimum(z, 0.0) - jnp.log(1.0 + jnp.exp(-jnp.abs(z)))) * (1.0 / GLA_TAU)
    rows = lax.broadcasted_iota(I32, (c, c), 0)
    cols = lax.broadcasted_iota(I32, (c, c), 1)
    causal = cols <= rows
    tril = causal.astype(F32)
    gn = gn_ref[...]
    scale = GLA_HEAD_K ** -0.5

    for n in range(sb // c):
        sl = slice(n * c, (n + 1) * c)
        g = log_a[sl, :]
        b = jnp.dot(tril, g, preferred_element_type=F32, precision=lax.Precision.HIGHEST)
        b_last = b[c - 1:c, :]
        q = q_ref[sl, :].astype(F32)
        k = k_ref[sl, :].astype(F32)
        v = v_ref[sl, :]
        qe = (q * scale * jnp.exp(b)).astype(BF16)
        ke = (k * jnp.exp(-b)).astype(BF16)
        kd = (k * jnp.exp(b_last - b)).astype(BF16)
        att = jnp.where(causal, _dot_nt(qe, ke), 0.0).astype(BF16)
        st = st_ref[...]
        o = _dot(att, v) + _dot_nt(qe, st.astype(BF16))
        st_ref[...] = st * jnp.exp(b_last) + _dot_tn(v, kd)
        og = og_ref[sl, :].astype(F32)
        o = _rms(o, gn) * (og * (1.0 / (1.0 + jnp.exp(-og))))
        o_ref[sl, :] = o.astype(o_ref.dtype)


def gla_mixer(h_in, w_gate_pad, b_gate, gla_out_norm, *, batch, seq, sb):
    sb = min(sb, seq)
    nsb = seq // sb
    m = batch * seq
    return pl.pallas_call(
        functools.partial(_gla_kernel, sb=sb, c=GLA_CHUNK),
        out_shape=jax.ShapeDtypeStruct((m, GLA_V_WIDTH), BF16),
        grid=(batch, GLA_HEADS, nsb),
        in_specs=[pl.BlockSpec((sb, GLA_HEAD_K), lambda b, h, s: (b * nsb + s, COL_GQ // GLA_HEAD_K + h)),
                  pl.BlockSpec((sb, GLA_HEAD_K), lambda b, h, s: (b * nsb + s, COL_GK // GLA_HEAD_K + h)),
                  pl.BlockSpec((sb, GLA_HEAD_V), lambda b, h, s: (b * nsb + s, COL_GV // GLA_HEAD_V + h)),
                  pl.BlockSpec((sb, GLA_HEAD_V), lambda b, h, s: (b * nsb + s, COL_OG // GLA_HEAD_V + h)),
                  pl.BlockSpec((sb, LANES), lambda b, h, s: (b * nsb + s, COL_LR // LANES)),
                  pl.BlockSpec((LANES, GLA_HEAD_K), lambda b, h, s: (0, h)),
                  pl.BlockSpec((1, GLA_HEAD_K), lambda b, h, s: (0, h)),
                  pl.BlockSpec((1, GLA_HEAD_V), lambda b, h, s: (0, 0))],
        out_specs=pl.BlockSpec((sb, GLA_HEAD_V), lambda b, h, s: (b * nsb + s, h)),
        scratch_shapes=[pltpu.VMEM((GLA_HEAD_V, GLA_HEAD_K), F32)],
        compiler_params=_cparams("parallel", "parallel", "arbitrary"),
        name="gla_mixer",
    )(h_in, h_in, h_in, h_in, h_in, w_gate_pad, b_gate.reshape(1, -1).astype(F32),
      gla_out_norm.reshape(1, -1).astype(F32))


def _out_proj_kernel(om_ref, g_ref, og_ref, w_ref, x_ref, o_ref, mix_ref):
    @pl.when(pl.program_id(1) == 0)
    def _():
        mix_ref[:, :MLA_WIDTH] = _rms(om_ref[...].astype(F32), g_ref[...]).astype(BF16)
        mix_ref[:, MLA_WIDTH:] = og_ref[...]

    o_ref[...] = x_ref[...] + _dot(mix_ref[...], w_ref[...])


def mixer_out_proj(o_mla, mla_out_norm, o_gla, w_out, x, *, tm, tn):
    m, d = x.shape
    tm = min(tm, m)
    tn = min(tn, d)
    kmix = MLA_WIDTH + GLA_V_WIDTH
    return pl.pallas_call(
        _out_proj_kernel,
        out_shape=jax.ShapeDtypeStruct((m, d), F32),
        grid=(m // tm, d // tn),
        in_specs=[pl.BlockSpec((tm, MLA_WIDTH), lambda i, j: (i, 0)),
                  pl.BlockSpec((1, MLA_WIDTH), lambda i, j: (0, 0)),
                  pl.BlockSpec((tm, GLA_V_WIDTH), lambda i, j: (i, 0)),
                  pl.BlockSpec((kmix, tn), lambda i, j: (0, j)),
                  pl.BlockSpec((tm, tn), lambda i, j: (i, j))],
        out_specs=pl.BlockSpec((tm, tn), lambda i, j: (i, j)),
        scratch_shapes=[pltpu.VMEM((tm, kmix), BF16)],
        compiler_params=_cparams("parallel", "arbitrary"),
        name="mixer_out_proj",
    )(o_mla, mla_out_norm.reshape(1, -1).astype(F32), o_gla, w_out, x)


def _xattn_kernel(q_ref, kv_ref, w_ref, h_ref, o_ref):
    scale = XATTN_DIM ** -0.5
    outs = []
    for h in range(XATTN_HEADS):
        q = q_ref[:, h * XATTN_DIM:(h + 1) * XATTN_DIM]
        k = kv_ref[:, h * XATTN_DIM:(h + 1) * XATTN_DIM]
        v = kv_ref[:, XATTN_WIDTH + h * XATTN_DIM: XATTN_WIDTH + (h + 1) * XATTN_DIM]
        s = _dot_nt(q, k) * scale
        p = jnp.exp(s - jnp.max(s, axis=1, keepdims=True))
        p = p / jnp.sum(p, axis=1, keepdims=True)
        outs.append(_dot(p.astype(BF16), v).astype(BF16))
    o = jnp.concatenate(outs, axis=1)
    o_ref[...] = h_ref[...] + _dot(o, w_ref[...])


def cross_attention(qx, kvm, w_co, h1, *, batch, seq, mem_tokens, tm):
    m, d = h1.shape
    tm = min(tm, seq)
    per_b = seq // tm
    return pl.pallas_call(
        _xattn_kernel,
        out_shape=jax.ShapeDtypeStruct((m, d), F32),
        grid=(m // tm,),
        in_specs=[pl.BlockSpec((tm, XATTN_WIDTH), lambda i: (i, 0)),
                  pl.BlockSpec((mem_tokens, 2 * XATTN_WIDTH), lambda i: (i // per_b, 0)),
                  pl.BlockSpec(w_co.shape, lambda i: (0, 0)),
                  pl.BlockSpec((tm, d), lambda i: (i, 0))],
        out_specs=pl.BlockSpec((tm, d), lambda i: (i, 0)),
        compiler_params=_cparams("parallel"),
        name="cross_attention",
    )(qx, kvm, w_co, h1)


INT_MIN = -2 ** 31


def _order_key(x):
    b = lax.bitcast_convert_type(x, I32)
    return b ^ (lax.shift_right_arithmetic(b, 31) & 0x7FFFFFFF)


def _order_key_inv(k):
    return lax.bitcast_convert_type(k ^ (lax.shift_right_arithmetic(k, 31) & 0x7FFFFFFF), F32)


def _pack_keys(x, bits):
    low = (1 << bits) - 1
    row = lax.broadcasted_iota(I32, x.shape, 0)
    return (_order_key(x) & ~low) | (low - row)


def _top16_rows(s, bits):
    low = (1 << bits) - 1
    kp = _pack_keys(s, bits)
    vals, idxs = [], []
    for _ in range(PEER_TOPK):
        mx = jnp.max(kp, axis=0, keepdims=True)
        vals.append(_order_key_inv(mx & ~low))
        idxs.append(low - (mx & low))
        kp = jnp.where(kp == mx, INT_MIN, kp)
    return vals, idxs


_PAIRS = [(a, b) for a in range(PEER_TOPK) for b in range(PEER_TOPK) if (a + 1) * (b + 1) <= PEER_TOPK]
_PAIR_ROWS = -(-len(_PAIRS) // 8) * 8


def _peer_topk_kernel(q_ref, keys_ref, eid_ref, gate_ref, eid_t, gate_t, *, tm):
    def head(h, carry):
        tops = []
        for p in range(2):
            col = pl.multiple_of((2 * h + p) * PEER_HALF, PEER_HALF)
            qh = q_ref[:, pl.ds(col, PEER_HALF)]
            s = _dot_nt(keys_ref[2 * h + p], qh)
            tops.append(_top16_rows(s, 7))
        (s1, i1), (s2, i2) = tops
        cand_s = [s1[a] + s2[b] for a, b in _PAIRS]
        cand_e = [i1[a] * PEER_KEYS + i2[b] for a, b in _PAIRS]
        pad = _PAIR_ROWS - len(_PAIRS)
        cs = jnp.concatenate(cand_s + [jnp.full((pad, tm), -jnp.inf, F32)], axis=0)
        ce = jnp.concatenate(cand_e + [jnp.zeros((pad, tm), I32)], axis=0)
        low = 63
        kp = _pack_keys(cs, 6)
        sel_s, sel_e = [], []
        for _ in range(PEER_TOPK):
            mx = jnp.max(kp, axis=0, keepdims=True)
            hit = kp == mx
            sel_s.append(_order_key_inv(mx & ~low))
            sel_e.append(jnp.max(jnp.where(hit, ce, -1), axis=0, keepdims=True))
            kp = jnp.where(hit, INT_MIN, kp)
        top_s = jnp.concatenate(sel_s, axis=0)
        top_e = jnp.concatenate(sel_e, axis=0)
        w = jnp.exp(top_s - top_s[0:1, :])
        gate = w / jnp.sum(w, axis=0, keepdims=True)
        row = pl.multiple_of(h * PEER_TOPK, PEER_TOPK)
        eid_t[pl.ds(row, PEER_TOPK), :] = top_e
        gate_t[pl.ds(row, PEER_TOPK), :] = gate
        return carry

    lax.fori_loop(0, PEER_HEADS, head, 0)
    eid_ref[...] = eid_t[...].T
    gate_ref[...] = gate_t[...].T


def peer_topk(qp, keys, *, tm):
    m = qp.shape[0]
    tm = min(tm, m)
    return pl.pallas_call(
        functools.partial(_peer_topk_kernel, tm=tm),
        out_shape=[jax.ShapeDtypeStruct((m, PEER_SLOTS), I32), jax.ShapeDtypeStruct((m, PEER_SLOTS), F32)],
        grid=(m // tm,),
        in_specs=[pl.BlockSpec((tm, qp.shape[1]), lambda i: (i, 0)),
                  pl.BlockSpec(keys.shape, lambda i: (0, 0, 0))],
        out_specs=[pl.BlockSpec((tm, PEER_SLOTS), lambda i: (i, 0)),
                   pl.BlockSpec((tm, PEER_SLOTS), lambda i: (i, 0))],
        scratch_shapes=[pltpu.VMEM((PEER_SLOTS, tm), I32), pltpu.VMEM((PEER_SLOTS, tm), F32)],
        compiler_params=_cparams("parallel"),
        name="peer_topk",
    )(qp, keys)


GATE_UNROLL = 8


def _peer_gates_kernel(eid_ref, gate_ref, o_ref, g3_ref, *, tm, pitch):
    sub = lax.broadcasted_iota(I32, (PEER_KEYS, PEER_SLOTS), 0)

    def token(t, carry):
        e = eid_ref[pl.ds(t, 1), :]
        g = gate_ref[pl.ds(t, 1), :]
        hi = lax.shift_right_logical(e, 7)
        lo = jnp.bitwise_and(e, PEER_KEYS - 1)
        a1 = jnp.where(sub == hi, g, 0.0).astype(BF16)
        a2 = jnp.where(sub == lo, 1.0, 0.0).astype(BF16)
        g3_ref[pl.ds(t, PEER_KEYS, stride=pitch), :] = _dot_nt(a1, a2)
        return carry

    lax.fori_loop(0, tm, token, 0, unroll=GATE_UNROLL)
    for e1 in range(PEER_KEYS):
        o_ref[:, e1 * PEER_KEYS:(e1 + 1) * PEER_KEYS] = g3_ref[e1 * pitch:e1 * pitch + tm, :].astype(o_ref.dtype)


def peer_gates(eid, gate, *, tm):
    m = eid.shape[0]
    tm = min(tm, m)
    pitch = tm + 8
    return pl.pallas_call(
        functools.partial(_peer_gates_kernel, tm=tm, pitch=pitch),
        out_shape=jax.ShapeDtypeStruct((m, PEER_EXPERTS), BF16),
        grid=(m // tm,),
        in_specs=[pl.BlockSpec((tm, PEER_SLOTS), lambda i: (i, 0)),
                  pl.BlockSpec((tm, PEER_SLOTS), lambda i: (i, 0))],
        out_specs=pl.BlockSpec((tm, PEER_EXPERTS), lambda i: (i, 0)),
        scratch_shapes=[pltpu.VMEM((PEER_KEYS * pitch, PEER_KEYS), F32)],
        compiler_params=_cparams("parallel"),
        name="peer_gates",
    )(eid, gate)


def _peer_dense_kernel(x_ref, sx_ref, u_ref, su_ref, g_ref, v_ref, sv_ref, h_ref, gn_ref, o_ref, *, final_norm):
    j = pl.program_id(1)

    @pl.when(j == 0)
    def _():
        o_ref[...] = jnp.zeros_like(o_ref)

    s = _dot_nt(x_ref[...], u_ref[...]) * sx_ref[...] * su_ref[...]
    a = 0.5 * s * (1.0 + lax.erf(s * (2.0 ** -0.5)))
    aq, sa = _quantize_rows(a * g_ref[...].astype(F32) * sv_ref[...])
    o_ref[...] += sa * _dot(aq, v_ref[...])

    @pl.when(j == pl.num_programs(1) - 1)
    def _():
        y = h_ref[...] + o_ref[...]
        o_ref[...] = _rms(y, gn_ref[...]) if final_norm else y


def peer_dense(xq, sx, uq, su, g, vq, sv, h, gain, *, final_norm, tm, te):
    m, d = xq.shape
    e = uq.shape[0]
    tm = min(tm, m)
    once = pl.Buffered(1)
    return pl.pallas_call(
        functools.partial(_peer_dense_kernel, final_norm=final_norm),
        out_shape=jax.ShapeDtypeStruct((m, d), F32),
        grid=(m // tm, e // te),
        in_specs=[pl.BlockSpec((tm, d), lambda i, j: (i, 0)),
                  pl.BlockSpec((tm, 1), lambda i, j: (i, 0)),
                  pl.BlockSpec((te, d), lambda i, j: (j, 0)),
                  pl.BlockSpec((1, te), lambda i, j: (0, j)),
                  pl.BlockSpec((tm, te), lambda i, j: (i, j)),
                  pl.BlockSpec((te, d), lambda i, j: (j, 0)),
                  pl.BlockSpec((1, te), lambda i, j: (0, j)),
                  pl.BlockSpec((tm, d), lambda i, j: (i, 0), pipeline_mode=once),
                  pl.BlockSpec((1, d), lambda i, j: (0, 0))],
        out_specs=pl.BlockSpec((tm, d), lambda i, j: (i, 0)),
        compiler_params=_cparams("parallel", "arbitrary"),
        name="peer_dense",
    )(xq, sx, uq, su.reshape(1, e), g, vq, sv.reshape(1, e), h, gain.reshape(1, d).astype(F32))


def _w_in_columns():
    src = {}
    off = 0
    for name, width in (("cq", MLA_Q_LORA), ("ckv", MLA_KV_LORA), ("kr", MLA_ROPE), ("gq", GLA_K_WIDTH),
                        ("gk", GLA_K_WIDTH), ("gv", GLA_V_WIDTH), ("lr", GLA_RANK), ("og", GLA_V_WIDTH)):
        src[name] = np.arange(off, off + width)
        off += width
    cols = np.full((IN_PAD,), -1, np.int64)
    for name, start in (("cq", COL_CQ), ("ckv", COL_CKV), ("gq", COL_GQ), ("gk", COL_GK),
                        ("gv", COL_GV), ("og", COL_OG), ("lr", COL_LR)):
        cols[start:start + len(src[name])] = src[name]
    half = MLA_ROPE // 2
    x1, x2 = src["kr"][:half], src["kr"][half:]
    cols[COL_KR:COL_KR + LANES] = np.concatenate([x1, x1, x2, x2])
    return cols


def _w_uq_columns():
    half = MLA_ROPE // 2
    nope = [h * MLA_QK + d for h in range(MLA_HEADS) for d in range(MLA_NOPE)]
    rope = []
    for j in range(MLA_HEADS // 2):
        a, b = 2 * j, 2 * j + 1
        for part in (0, 1):
            for h in (a, b):
                rope += [h * MLA_QK + MLA_NOPE + part * half + r for r in range(half)]
    return np.array(nope + rope, np.int64)


def _take_cols(w, cols, dtype):
    pieces = []
    start = 0
    for i in range(1, len(cols) + 1):
        if i == len(cols) or cols[i] != (cols[i - 1] + 1 if cols[i - 1] >= 0 else -1):
            if cols[start] < 0:
                pieces.append(jnp.zeros((w.shape[0], i - start), dtype))
            else:
                pieces.append(w[:, int(cols[start]):int(cols[start]) + i - start].astype(dtype))
            start = i
    return jnp.concatenate(pieces, axis=1)


def _rope_tables(positions):
    half = MLA_ROPE // 2
    inv_freq = ROPE_THETA ** (-jnp.arange(0, MLA_ROPE, 2, dtype=F32) / MLA_ROPE)
    ang = positions.astype(F32).reshape(-1, 1) * inv_freq
    cos = jnp.cos(ang)
    sin = jnp.sin(ang)
    return jnp.tile(cos, (1, 4)), jnp.concatenate([-sin, -sin, sin, sin], axis=1)


def kernel(x, mem, positions, norm_mem, norm_mix, w_in, mla_q_norm, w_uq, mla_kv_norm, w_ukv, mla_out_norm,
           w_gate_up, b_gate, gla_out_norm, w_out, norm_cross, w_cq, w_ck, w_cv, w_co, norm_ffn, w_peer_q,
           peer_sub_keys, peer_u, peer_v, norm_final):
    batch, seq, d = x.shape
    mem_tokens = mem.shape[1]
    m = batch * seq
    h = x.reshape(m, d)
    cos_t, sin_t = _rope_tables(positions)

    mn_kv = norm_matmul(mem.reshape(batch * mem_tokens, d), norm_mem,
                        jnp.concatenate([w_ck[0], w_cv[0]], axis=1).astype(BF16),
                        k=d, tm=512, tn=1024, out_dtype=BF16, name="mem_kv_proj")

    for l in range(norm_mix.shape[0]):
        w_in_p = _take_cols(w_in[l], _w_in_columns(), BF16)
        h_in = norm_matmul(h, norm_mix[l], w_in_p, k=d, tm=512, tn=1024, out_dtype=BF16, name="in_proj")

        w_uq_p = _take_cols(w_uq[l], _w_uq_columns(), BF16)
        q_full = mla_q_proj(h_in, mla_q_norm[l], w_uq_p, cos_t, sin_t, tm=512)
        kv, k_rope = mla_kv_proj(h_in, mla_kv_norm[l], w_ukv[l].astype(BF16), cos_t, sin_t, tm=512)
        o_mla = mla_attention(q_full, kv, k_rope, batch=batch, seq=seq, blk=512)

        w_gate_pad = jnp.zeros((LANES, GLA_K_WIDTH), F32).at[:GLA_RANK].set(w_gate_up[l]).astype(BF16)
        o_gla = gla_mixer(h_in, w_gate_pad, b_gate[l], gla_out_norm[l], batch=batch, seq=seq, sb=1024)

        h = mixer_out_proj(o_mla, mla_out_norm[l], o_gla, w_out[l].astype(BF16), h, tm=512, tn=1024)

        qx = norm_matmul(h, norm_cross[l], w_cq[l].astype(BF16), k=d, tm=512, tn=1024, out_dtype=BF16,
                         name="xattn_q_proj")
        h = cross_attention(qx, mn_kv, w_co[l].astype(BF16), h, batch=batch, seq=seq,
                            mem_tokens=mem_tokens, tm=256)

        qp, xq, sx = norm_matmul(h, norm_ffn[l], w_peer_q[l].astype(BF16), k=d, tm=512, tn=512, out_dtype=BF16,
                                 emit_xq=True, name="peer_q_proj")
        keys = peer_sub_keys[l].reshape(PEER_HEADS * 2, PEER_KEYS, PEER_HALF).astype(BF16)
        eid, gate = peer_topk(qp, keys, tm=256)
        g = peer_gates(eid, gate, tm=128)
        uq, su = quantize_rows(peer_u[l], tr=512, name="peer_u_quant")
        vq, sv = quantize_rows(peer_v[l], tr=512, name="peer_v_quant")
        h = peer_dense(xq, sx, uq, su, g, vq, sv, h, norm_final,
                       final_norm=(l + 1 == norm_mix.shape[0]), tm=512, te=512)
    return h.reshape(batch, seq, d)
```

```python
import functools
import math

import numpy as np
import jax
import jax.numpy as jnp
from jax import lax
from jax.experimental import pallas as pl
from jax.experimental.pallas import tpu as pltpu

F32 = jnp.float32
BF16 = jnp.bfloat16
F8 = jnp.float8_e4m3fn
I32 = jnp.int32
FP8_AMAX = 256.0

EPS = 1e-6
ROPE_THETA = 10000.0

MLA_HEADS = 16
MLA_Q_LORA = 1024
MLA_KV_LORA = 512
MLA_NOPE = 128
MLA_ROPE = 64
MLA_QK = MLA_NOPE + MLA_ROPE
MLA_V = 128
MLA_WIDTH = MLA_HEADS * MLA_V

GLA_HEADS = 4
GLA_HEAD_K = 256
GLA_HEAD_V = 512
GLA_K_WIDTH = GLA_HEADS * GLA_HEAD_K
GLA_V_WIDTH = GLA_HEADS * GLA_HEAD_V
GLA_RANK = 16
GLA_TAU = 16.0
GLA_CHUNK = 64

XATTN_HEADS = 4
XATTN_DIM = 256
XATTN_WIDTH = XATTN_HEADS * XATTN_DIM

PEER_HEADS = 8
PEER_KEYS = 128
PEER_HALF = 128
PEER_TOPK = 16
PEER_SLOTS = PEER_HEADS * PEER_TOPK
PEER_EXPERTS = PEER_KEYS * PEER_KEYS

LANES = 128
VMEM_LIMIT = 56 * 1024 * 1024

COL_CQ = 0
COL_CKV = COL_CQ + MLA_Q_LORA
COL_GQ = COL_CKV + MLA_KV_LORA
COL_GK = COL_GQ + GLA_K_WIDTH
COL_GV = COL_GK + GLA_K_WIDTH
COL_OG = COL_GV + GLA_V_WIDTH
COL_KR = COL_OG + GLA_V_WIDTH
COL_LR = COL_KR + LANES
IN_PAD = 8192

NEG = -1e30


def _cparams(*sem):
    return pltpu.CompilerParams(dimension_semantics=sem, vmem_limit_bytes=VMEM_LIMIT)


def _rms(x, g):
    ms = jnp.mean(x * x, axis=-1, keepdims=True)
    return x * lax.rsqrt(ms + EPS) * g


def _dot(a, b):
    return jnp.dot(a, b, preferred_element_type=F32)


def _dot_nt(a, b):
    return lax.dot_general(a, b, (((1,), (1,)), ((), ())), preferred_element_type=F32)


def _dot_tn(a, b):
    return lax.dot_general(a, b, (((0,), (0,)), ((), ())), preferred_element_type=F32)


def _quantize_rows(x):
    amax = jnp.max(jnp.abs(x), axis=-1, keepdims=True)
    scale = jnp.where(amax > 0.0, amax * (1.0 / FP8_AMAX), 1.0)
    return (x * (1.0 / scale)).astype(F8), scale


def _norm_matmul_kernel(x_ref, g_ref, w_ref, o_ref, *rest, emit_xq):
    if emit_xq:
        xq_ref, sx_ref, xn_ref = rest
    else:
        (xn_ref,) = rest

    @pl.when(pl.program_id(1) == 0)
    def _():
        xn = _rms(x_ref[...].astype(F32), g_ref[...])
        xn_ref[...] = xn.astype(BF16)
        if emit_xq:
            xq_ref[...], sx_ref[...] = _quantize_rows(xn)

    o_ref[...] = _dot(xn_ref[...], w_ref[...]).astype(o_ref.dtype)


def norm_matmul(x, gain, w, *, k, x_col_blk=0, tm, tn, out_dtype, emit_xq=False, name):
    m = x.shape[0]
    n = w.shape[1]
    tm = min(tm, m)
    tn = min(tn, n)
    out_shape = [jax.ShapeDtypeStruct((m, n), out_dtype)]
    out_specs = [pl.BlockSpec((tm, tn), lambda i, j: (i, j))]
    if emit_xq:
        out_shape += [jax.ShapeDtypeStruct((m, k), F8), jax.ShapeDtypeStruct((m, 1), F32)]
        out_specs += [pl.BlockSpec((tm, k), lambda i, j: (i, 0)), pl.BlockSpec((tm, 1), lambda i, j: (i, 0))]
    res = pl.pallas_call(
        functools.partial(_norm_matmul_kernel, emit_xq=emit_xq),
        out_shape=out_shape,
        grid=(m // tm, n // tn),
        in_specs=[pl.BlockSpec((tm, k), lambda i, j: (i, x_col_blk)),
                  pl.BlockSpec((1, k), lambda i, j: (0, 0)),
                  pl.BlockSpec((k, tn), lambda i, j: (0, j))],
        out_specs=out_specs,
        scratch_shapes=[pltpu.VMEM((tm, k), BF16)],
        compiler_params=_cparams("parallel", "arbitrary"),
        name=name,
    )(x, gain.reshape(1, k).astype(F32), w)
    return res if emit_xq else res[0]


def _quantize_kernel(w_ref, q_ref, s_ref):
    q_ref[...], s_ref[...] = _quantize_rows(w_ref[...])


def quantize_rows(w, *, tr, name):
    r, d = w.shape
    return pl.pallas_call(
        _quantize_kernel,
        out_shape=[jax.ShapeDtypeStruct((r, d), F8), jax.ShapeDtypeStruct((r, 1), F32)],
        grid=(r // tr,),
        in_specs=[pl.BlockSpec((tr, d), lambda i: (i, 0))],
        out_specs=[pl.BlockSpec((tr, d), lambda i: (i, 0)), pl.BlockSpec((tr, 1), lambda i: (i, 0))],
        compiler_params=_cparams("parallel"),
        name=name,
    )(w)


def _rope_pair(r, cosv, sinv):
    return r * cosv + pltpu.roll(r, 2 * 32, axis=1) * sinv


def _qproj_kernel(c_ref, g_ref, w_ref, cos_ref, sin_ref, o_ref, *, scale):
    xn = _rms(c_ref[...].astype(F32), g_ref[...]).astype(BF16)
    q = _dot(xn, w_ref[...])
    cosv = cos_ref[...]
    sinv = sin_ref[...]
    lane = lax.broadcasted_iota(I32, (1, LANES), 1)
    rope_base = MLA_HEADS * MLA_NOPE
    for j in range(MLA_HEADS // 2):
        r = q[:, rope_base + j * LANES: rope_base + (j + 1) * LANES]
        r = _rope_pair(r, cosv, sinv) * scale
        for p in range(2):
            h = 2 * j + p
            own = ((lane // 32) % 2) == p
            o_ref[:, h * 256: h * 256 + 128] = (q[:, h * 128:(h + 1) * 128] * scale).astype(o_ref.dtype)
            o_ref[:, h * 256 + 128: h * 256 + 256] = jnp.where(own, r, 0.0).astype(o_ref.dtype)


def mla_q_proj(h_in, gain, w_uq_perm, cos_t, sin_t, *, tm):
    m = h_in.shape[0]
    tm = min(tm, m)
    return pl.pallas_call(
        functools.partial(_qproj_kernel, scale=MLA_QK ** -0.5 * math.log2(math.e)),
        out_shape=jax.ShapeDtypeStruct((m, MLA_HEADS * 256), BF16),
        grid=(m // tm,),
        in_specs=[pl.BlockSpec((tm, MLA_Q_LORA), lambda i: (i, COL_CQ // MLA_Q_LORA)),
                  pl.BlockSpec((1, MLA_Q_LORA), lambda i: (0, 0)),
                  pl.BlockSpec(w_uq_perm.shape, lambda i: (0, 0)),
                  pl.BlockSpec((tm, LANES), lambda i: (i, 0)),
                  pl.BlockSpec((tm, LANES), lambda i: (i, 0))],
        out_specs=pl.BlockSpec((tm, MLA_HEADS * 256), lambda i: (i, 0)),
        compiler_params=_cparams("parallel"),
        name="mla_q_proj",
    )(h_in, gain.reshape(1, -1).astype(F32), w_uq_perm, cos_t, sin_t)


def _kvproj_kernel(c_ref, g_ref, w_ref, kr_ref, cos_ref, sin_ref, kv_ref, kro_ref):
    xn = _rms(c_ref[...].astype(F32), g_ref[...]).astype(BF16)
    kv_ref[...] = _dot(xn, w_ref[...]).astype(kv_ref.dtype)
    kro_ref[...] = _rope_pair(kr_ref[...].astype(F32), cos_ref[...], sin_ref[...]).astype(kro_ref.dtype)


def mla_kv_proj(h_in, gain, w_ukv, cos_t, sin_t, *, tm):
    m = h_in.shape[0]
    tm = min(tm, m)
    n = w_ukv.shape[1]
    return pl.pallas_call(
        _kvproj_kernel,
        out_shape=[jax.ShapeDtypeStruct((m, n), BF16), jax.ShapeDtypeStruct((m, LANES), BF16)],
        grid=(m // tm,),
        in_specs=[pl.BlockSpec((tm, MLA_KV_LORA), lambda i: (i, COL_CKV // MLA_KV_LORA)),
                  pl.BlockSpec((1, MLA_KV_LORA), lambda i: (0, 0)),
                  pl.BlockSpec(w_ukv.shape, lambda i: (0, 0)),
                  pl.BlockSpec((tm, LANES), lambda i: (i, COL_KR // LANES)),
                  pl.BlockSpec((tm, LANES), lambda i: (i, 0)),
                  pl.BlockSpec((tm, LANES), lambda i: (i, 0))],
        out_specs=[pl.BlockSpec((tm, n), lambda i: (i, 0)),
                   pl.BlockSpec((tm, LANES), lambda i: (i, 0))],
        compiler_params=_cparams("parallel"),
        name="mla_kv_proj",
    )(h_in, gain.reshape(1, -1).astype(F32), w_ukv, h_in, cos_t, sin_t)


ATTN_UNROLL = 8


def _mla_attn_kernel(q_ref, kv_ref, kr_ref, o_ref, s0_scr, s1_scr, m_scr, acc_scr, *, blk):
    qi = pl.program_id(2)
    q = q_ref[...]
    ones = jnp.ones((blk, LANES), BF16)
    m_scr[...] = jnp.full(m_scr.shape, NEG, F32)
    acc_scr[...] = jnp.zeros(acc_scr.shape, F32)

    def scores(j, s_scr):
        start = pl.multiple_of(j * blk, blk)
        k = jnp.concatenate([kv_ref[pl.ds(start, blk), :MLA_NOPE], kr_ref[pl.ds(start, blk), :]], axis=1)
        s_scr[...] = _dot_nt(q, k)

    def finish(j, s_scr, masked):
        start = pl.multiple_of(j * blk, blk)
        s = s_scr[...]
        if masked:
            row = lax.broadcasted_iota(I32, (blk, blk), 0)
            col = lax.broadcasted_iota(I32, (blk, blk), 1)
            s = jnp.where(col <= row, s, NEG)
        m = m_scr[...]
        m_new = jnp.maximum(m, jnp.max(s, axis=1, keepdims=True))
        alpha = jnp.exp2(m - m_new)
        p = jnp.exp2(s - jnp.concatenate([m_new] * (blk // LANES), axis=1)).astype(BF16)
        m_scr[...] = m_new
        v = jnp.concatenate([kv_ref[pl.ds(start, blk), MLA_NOPE:], ones], axis=1)
        acc_scr[...] = jnp.concatenate([alpha, alpha], axis=1) * acc_scr[...] + _dot(p, v)

    s_scr = (s0_scr, s1_scr)

    def run(j0, n):
        for u in range(n):
            scores(j0 + u + 1, s_scr[(u + 1) % 2])
            finish(j0 + u, s_scr[u % 2], False)

    scores(0, s0_scr)

    def body(t, carry):
        run(t * ATTN_UNROLL, ATTN_UNROLL)
        return carry

    lax.fori_loop(0, qi // ATTN_UNROLL, body, 0)
    done = (qi // ATTN_UNROLL) * ATTN_UNROLL
    n = ATTN_UNROLL // 2
    while n >= 1:
        @pl.when((qi & n) != 0)
        def _(done=done, n=n):
            run(done, n)

        done = done + (qi & n)
        n //= 2

    @pl.when(qi % 2 == 0)
    def _():
        finish(qi, s0_scr, True)

    @pl.when(qi % 2 == 1)
    def _():
        finish(qi, s1_scr, True)

    acc = acc_scr[...]
    o_ref[...] = (acc[:, :MLA_V] / acc[:, MLA_V:]).astype(o_ref.dtype)


def mla_attention(q_full, kv, k_rope, *, batch, seq, blk):
    blk = min(blk, seq)
    nq = seq // blk
    m = batch * seq
    return pl.pallas_call(
        functools.partial(_mla_attn_kernel, blk=blk),
        out_shape=jax.ShapeDtypeStruct((m, MLA_WIDTH), BF16),
        grid=(batch, MLA_HEADS, nq),
        in_specs=[pl.BlockSpec((blk, 256), lambda b, h, i: (b * nq + i, h)),
                  pl.BlockSpec((seq, 256), lambda b, h, i: (b, h)),
                  pl.BlockSpec((seq, LANES), lambda b, h, i: (b, 0))],
        out_specs=pl.BlockSpec((blk, MLA_V), lambda b, h, i: (b * nq + i, h)),
        scratch_shapes=[pltpu.VMEM((blk, blk), F32), pltpu.VMEM((blk, blk), F32),
                        pltpu.VMEM((blk, LANES), F32), pltpu.VMEM((blk, 2 * MLA_V), F32)],
        compiler_params=_cparams("parallel", "parallel", "arbitrary"),
        name="mla_attention",
    )(q_full, kv, k_rope)


def _gla_kernel(q_ref, k_ref, v_ref, og_ref, lr_ref, wg_ref, bg_ref, gn_ref, o_ref, st_ref, *, sb, c):
    @pl.when(pl.program_id(2) == 0)
    def _():
        st_ref[...] = jnp.zeros_like(st_ref)

    z = _dot(lr_ref[...], wg_ref[...]) + bg_ref[...]
    log_a = (jnp.minimum(z, 0.0) - jnp.log(1.0 + jnp.exp(-jnp.abs(z)))) * (1.0 / GLA_TAU)
    rows = lax.broadcasted_iota(I32, (c, c), 0)
    cols = lax.broadcasted_iota(I32, (c, c), 1)
    causal = cols <= rows
    tril = causal.astype(F32)
    gn = gn_ref[...]
    scale = GLA_HEAD_K ** -0.5

    for n in range(sb // c):
        sl = slice(n * c, (n + 1) * c)
        g = log_a[sl, :]
        b = jnp.dot(tril, g, preferred_element_type=F32, precision=lax.Precision.HIGHEST)
        b_last = b[c - 1:c, :]
        q = q_ref[sl, :].astype(F32)
        k = k_ref[sl, :].astype(F32)
        v = v_ref[sl, :]
        qe = (q * scale * jnp.exp(b)).astype(BF16)
        ke = (k * jnp.exp(-b)).astype(BF16)
        kd = (k * jnp.exp(b_last - b)).astype(BF16)
        att = jnp.where(causal, _dot_nt(qe, ke), 0.0).astype(BF16)
        st = st_ref[...]
        o = _dot(att, v) + _dot_nt(qe, st.astype(BF16))
        st_ref[...] = st * jnp.exp(b_last) + _dot_tn(v, kd)
        og = og_ref[sl, :].astype(F32)
        o = _rms(o, gn) * (og * (1.0 / (1.0 + jnp.exp(-og))))
        o_ref[sl, :] = o.astype(o_ref.dtype)


def gla_mixer(h_in, w_gate_pad, b_gate, gla_out_norm, *, batch, seq, sb):
    sb = min(sb, seq)
    nsb = seq // sb
    m = batch * seq
    return pl.pallas_call(
        functools.partial(_gla_kernel, sb=sb, c=GLA_CHUNK),
        out_shape=jax.ShapeDtypeStruct((m, GLA_V_WIDTH), BF16),
        grid=(batch, GLA_HEADS, nsb),
        in_specs=[pl.BlockSpec((sb, GLA_HEAD_K), lambda b, h, s: (b * nsb + s, COL_GQ // GLA_HEAD_K + h)),
                  pl.BlockSpec((sb, GLA_HEAD_K), lambda b, h, s: (b * nsb + s, COL_GK // GLA_HEAD_K + h)),
                  pl.BlockSpec((sb, GLA_HEAD_V), lambda b, h, s: (b * nsb + s, COL_GV // GLA_HEAD_V + h)),
                  pl.BlockSpec((sb, GLA_HEAD_V), lambda b, h, s: (b * nsb + s, COL_OG // GLA_HEAD_V + h)),
                  pl.BlockSpec((sb, LANES), lambda b, h, s: (b * nsb + s, COL_LR // LANES)),
                  pl.BlockSpec((LANES, GLA_HEAD_K), lambda b, h, s: (0, h)),
                  pl.BlockSpec((1, GLA_HEAD_K), lambda b, h, s: (0, h)),
                  pl.BlockSpec((1, GLA_HEAD_V), lambda b, h, s: (0, 0))],
        out_specs=pl.BlockSpec((sb, GLA_HEAD_V), lambda b, h, s: (b * nsb + s, h)),
        scratch_shapes=[pltpu.VMEM((GLA_HEAD_V, GLA_HEAD_K), F32)],
        compiler_params=_cparams("parallel", "parallel", "arbitrary"),
        name="gla_mixer",
    )(h_in, h_in, h_in, h_in, h_in, w_gate_pad, b_gate.reshape(1, -1).astype(F32),
      gla_out_norm.reshape(1, -1).astype(F32))


def _out_proj_kernel(om_ref, g_ref, og_ref, w_ref, x_ref, o_ref, mix_ref):
    @pl.when(pl.program_id(1) == 0)
    def _():
        mix_ref[:, :MLA_WIDTH] = _rms(om_ref[...].astype(F32), g_ref[...]).astype(BF16)
        mix_ref[:, MLA_WIDTH:] = og_ref[...]

    o_ref[...] = x_ref[...] + _dot(mix_ref[...], w_ref[...])


def mixer_out_proj(o_mla, mla_out_norm, o_gla, w_out, x, *, tm, tn):
    m, d = x.shape
    tm = min(tm, m)
    tn = min(tn, d)
    kmix = MLA_WIDTH + GLA_V_WIDTH
    return pl.pallas_call(
        _out_proj_kernel,
        out_shape=jax.ShapeDtypeStruct((m, d), F32),
        grid=(m // tm, d // tn),
        in_specs=[pl.BlockSpec((tm, MLA_WIDTH), lambda i, j: (i, 0)),
                  pl.BlockSpec((1, MLA_WIDTH), lambda i, j: (0, 0)),
                  pl.BlockSpec((tm, GLA_V_WIDTH), lambda i, j: (i, 0)),
                  pl.BlockSpec((kmix, tn), lambda i, j: (0, j)),
                  pl.BlockSpec((tm, tn), lambda i, j: (i, j))],
        out_specs=pl.BlockSpec((tm, tn), lambda i, j: (i, j)),
        scratch_shapes=[pltpu.VMEM((tm, kmix), BF16)],
        compiler_params=_cparams("parallel", "arbitrary"),
        name="mixer_out_proj",
    )(o_mla, mla_out_norm.reshape(1, -1).astype(F32), o_gla, w_out, x)


def _xattn_kernel(q_ref, kv_ref, w_ref, h_ref, o_ref):
    scale = XATTN_DIM ** -0.5
    outs = []
    for h in range(XATTN_HEADS):
        q = q_ref[:, h * XATTN_DIM:(h + 1) * XATTN_DIM]
        k = kv_ref[:, h * XATTN_DIM:(h + 1) * XATTN_DIM]
        v = kv_ref[:, XATTN_WIDTH + h * XATTN_DIM: XATTN_WIDTH + (h + 1) * XATTN_DIM]
        s = _dot_nt(q, k) * scale
        p = jnp.exp(s - jnp.max(s, axis=1, keepdims=True))
        p = p / jnp.sum(p, axis=1, keepdims=True)
        outs.append(_dot(p.astype(BF16), v).astype(BF16))
    o = jnp.concatenate(outs, axis=1)
    o_ref[...] = h_ref[...] + _dot(o, w_ref[...])


def cross_attention(qx, kvm, w_co, h1, *, batch, seq, mem_tokens, tm):
    m, d = h1.shape
    tm = min(tm, seq)
    per_b = seq // tm
    return pl.pallas_call(
        _xattn_kernel,
        out_shape=jax.ShapeDtypeStruct((m, d), F32),
        grid=(m // tm,),
        in_specs=[pl.BlockSpec((tm, XATTN_WIDTH), lambda i: (i, 0)),
                  pl.BlockSpec((mem_tokens, 2 * XATTN_WIDTH), lambda i: (i // per_b, 0)),
                  pl.BlockSpec(w_co.shape, lambda i: (0, 0)),
                  pl.BlockSpec((tm, d), lambda i: (i, 0))],
        out_specs=pl.BlockSpec((tm, d), lambda i: (i, 0)),
        compiler_params=_cparams("parallel"),
        name="cross_attention",
    )(qx, kvm, w_co, h1)


INT_MIN = -2 ** 31


def _order_key(x):
    b = lax.bitcast_convert_type(x, I32)
    return b ^ (lax.shift_right_arithmetic(b, 31) & 0x7FFFFFFF)


def _order_key_inv(k):
    return lax.bitcast_convert_type(k ^ (lax.shift_right_arithmetic(k, 31) & 0x7FFFFFFF), F32)


def _pack_keys(x, bits):
    low = (1 << bits) - 1
    row = lax.broadcasted_iota(I32, x.shape, 0)
    return (_order_key(x) & ~low) | (low - row)


def _top16_rows(s, bits):
    low = (1 << bits) - 1
    kp = _pack_keys(s, bits)
    vals, idxs = [], []
    for _ in range(PEER_TOPK):
        mx = jnp.max(kp, axis=0, keepdims=True)
        vals.append(_order_key_inv(mx & ~low))
        idxs.append(low - (mx & low))
        kp = jnp.where(kp == mx, INT_MIN, kp)
    return vals, idxs


_PAIRS = [(a, b) for a in range(PEER_TOPK) for b in range(PEER_TOPK) if (a + 1) * (b + 1) <= PEER_TOPK]
_PAIR_ROWS = -(-len(_PAIRS) // 8) * 8


def _peer_topk_kernel(q_ref, keys_ref, eid_ref, gate_ref, eid_t, gate_t, *, tm):
    def head(h, carry):
        tops = []
        for p in range(2):
            col = pl.multiple_of((2 * h + p) * PEER_HALF, PEER_HALF)
            qh = q_ref[:, pl.ds(col, PEER_HALF)]
            s = _dot_nt(keys_ref[2 * h + p], qh)
            tops.append(_top16_rows(s, 7))
        (s1, i1), (s2, i2) = tops
        cand_s = [s1[a] + s2[b] for a, b in _PAIRS]
        cand_e = [i1[a] * PEER_KEYS + i2[b] for a, b in _PAIRS]
        pad = _PAIR_ROWS - len(_PAIRS)
        cs = jnp.concatenate(cand_s + [jnp.full((pad, tm), -jnp.inf, F32)], axis=0)
        ce = jnp.concatenate(cand_e + [jnp.zeros((pad, tm), I32)], axis=0)
        low = 63
        kp = _pack_keys(cs, 6)
        sel_s, sel_e = [], []
        for _ in range(PEER_TOPK):
            mx = jnp.max(kp, axis=0, keepdims=True)
            hit = kp == mx
            sel_s.append(_order_key_inv(mx & ~low))
            sel_e.append(jnp.max(jnp.where(hit, ce, -1), axis=0, keepdims=True))
            kp = jnp.where(hit, INT_MIN, kp)
        top_s = jnp.concatenate(sel_s, axis=0)
        top_e = jnp.concatenate(sel_e, axis=0)
        w = jnp.exp(top_s - top_s[0:1, :])
        gate = w / jnp.sum(w, axis=0, keepdims=True)
        row = pl.multiple_of(h * PEER_TOPK, PEER_TOPK)
        eid_t[pl.ds(row, PEER_TOPK), :] = top_e
        gate_t[pl.ds(row, PEER_TOPK), :] = gate
        return carry

    lax.fori_loop(0, PEER_HEADS, head, 0)
    eid_ref[...] = eid_t[...].T
    gate_ref[...] = gate_t[...].T


def peer_topk(qp, keys, *, tm):
    m = qp.shape[0]
    tm = min(tm, m)
    return pl.pallas_call(
        functools.partial(_peer_topk_kernel, tm=tm),
        out_shape=[jax.ShapeDtypeStruct((m, PEER_SLOTS), I32), jax.ShapeDtypeStruct((m, PEER_SLOTS), F32)],
        grid=(m // tm,),
        in_specs=[pl.BlockSpec((tm, qp.shape[1]), lambda i: (i, 0)),
                  pl.BlockSpec(keys.shape, lambda i: (0, 0, 0))],
        out_specs=[pl.BlockSpec((tm, PEER_SLOTS), lambda i: (i, 0)),
                   pl.BlockSpec((tm, PEER_SLOTS), lambda i: (i, 0))],
        scratch_shapes=[pltpu.VMEM((PEER_SLOTS, tm), I32), pltpu.VMEM((PEER_SLOTS, tm), F32)],
        compiler_params=_cparams("parallel"),
        name="peer_topk",
    )(qp, keys)


GATE_UNROLL = 32


def _peer_gates_kernel(eid_ref, gate_ref, o_ref, g3_ref, *, tm, pitch):
    sub = lax.broadcasted_iota(I32, (PEER_KEYS, PEER_SLOTS), 0)

    def token(t, carry):
        e = eid_ref[pl.ds(t, 1), :]
        g = gate_ref[pl.ds(t, 1), :]
        hi = lax.shift_right_logical(e, 7)
        lo = jnp.bitwise_and(e, PEER_KEYS - 1)
        a1 = jnp.where(sub == hi, g, 0.0).astype(BF16)
        a2 = jnp.where(sub == lo, 1.0, 0.0).astype(BF16)
        g3_ref[pl.ds(t, PEER_KEYS, stride=pitch), :] = _dot_nt(a1, a2)
        return carry

    lax.fori_loop(0, tm, token, 0, unroll=GATE_UNROLL)
    for e1 in range(PEER_KEYS):
        o_ref[:, e1 * PEER_KEYS:(e1 + 1) * PEER_KEYS] = g3_ref[e1 * pitch:e1 * pitch + tm, :].astype(o_ref.dtype)


def peer_gates(eid, gate, *, tm):
    m = eid.shape[0]
    tm = min(tm, m)
    pitch = tm + 8
    return pl.pallas_call(
        functools.partial(_peer_gates_kernel, tm=tm, pitch=pitch),
        out_shape=jax.ShapeDtypeStruct((m, PEER_EXPERTS), BF16),
        grid=(m // tm,),
        in_specs=[pl.BlockSpec((tm, PEER_SLOTS), lambda i: (i, 0)),
                  pl.BlockSpec((tm, PEER_SLOTS), lambda i: (i, 0))],
        out_specs=pl.BlockSpec((tm, PEER_EXPERTS), lambda i: (i, 0)),
        scratch_shapes=[pltpu.VMEM((PEER_KEYS * pitch, PEER_KEYS), F32)],
        compiler_params=_cparams("parallel"),
        name="peer_gates",
    )(eid, gate)


def _peer_dense_kernel(x_ref, sx_ref, u_ref, su_ref, g_ref, v_ref, sv_ref, h_ref, gn_ref, o_ref, aq_scr, sa_scr,
                       *, final_norm, n_blocks):
    j = pl.program_id(1)
    nj = n_blocks

    def step(p, do_values, do_gates):
        if do_gates:
            s = _dot_nt(x_ref[...], u_ref[...])
        if do_values:
            pv = _dot(aq_scr[1 - p], v_ref[...])
        if do_gates:
            s = s * sx_ref[...] * su_ref[...]
            a = 0.5 * s * (1.0 + lax.erf(s * (2.0 ** -0.5)))
            aq_scr[p], sa_scr[p] = _quantize_rows(a * g_ref[...].astype(F32) * sv_ref[...])
        if do_values:
            o_ref[...] += sa_scr[1 - p] * pv

    @pl.when(j == 0)
    def _():
        o_ref[...] = jnp.zeros_like(o_ref)
        step(0, False, True)

    for p in range(2):
        @pl.when(jnp.logical_and(jnp.logical_and(j > 0, j < nj), j % 2 == p))
        def _(p=p):
            step(p, True, True)

    @pl.when(j == nj)
    def _():
        step(nj % 2, True, False)
        y = h_ref[...] + o_ref[...]
        o_ref[...] = _rms(y, gn_ref[...]) if final_norm else y


def peer_dense(xq, sx, uq, su, g, vq, sv, h, gain, *, final_norm, tm, te):
    m, d = xq.shape
    e = uq.shape[0]
    tm = min(tm, m)
    nj = e // te
    once = pl.Buffered(1)

    def cur(j):
        return jnp.minimum(j, nj - 1)

    def prev(j):
        return jnp.maximum(j - 1, 0)

    return pl.pallas_call(
        functools.partial(_peer_dense_kernel, final_norm=final_norm, n_blocks=nj),
        out_shape=jax.ShapeDtypeStruct((m, d), F32),
        grid=(m // tm, nj + 1),
        in_specs=[pl.BlockSpec((tm, d), lambda i, j: (i, 0)),
                  pl.BlockSpec((tm, 1), lambda i, j: (i, 0)),
                  pl.BlockSpec((te, d), lambda i, j: (cur(j), 0)),
                  pl.BlockSpec((1, te), lambda i, j: (0, cur(j))),
                  pl.BlockSpec((tm, te), lambda i, j: (i, cur(j))),
                  pl.BlockSpec((te, d), lambda i, j: (prev(j), 0)),
                  pl.BlockSpec((1, te), lambda i, j: (0, cur(j))),
                  pl.BlockSpec((tm, d), lambda i, j: (i, 0), pipeline_mode=once),
                  pl.BlockSpec((1, d), lambda i, j: (0, 0))],
        out_specs=pl.BlockSpec((tm, d), lambda i, j: (i, 0)),
        scratch_shapes=[pltpu.VMEM((2, tm, te), F8), pltpu.VMEM((2, tm, 1), F32)],
        compiler_params=_cparams("parallel", "arbitrary"),
        name="peer_dense",
    )(xq, sx, uq, su.reshape(1, e), g, vq, sv.reshape(1, e), h, gain.reshape(1, d).astype(F32))


def _w_in_columns():
    src = {}
    off = 0
    for name, width in (("cq", MLA_Q_LORA), ("ckv", MLA_KV_LORA), ("kr", MLA_ROPE), ("gq", GLA_K_WIDTH),
                        ("gk", GLA_K_WIDTH), ("gv", GLA_V_WIDTH), ("lr", GLA_RANK), ("og", GLA_V_WIDTH)):
        src[name] = np.arange(off, off + width)
        off += width
    cols = np.full((IN_PAD,), -1, np.int64)
    for name, start in (("cq", COL_CQ), ("ckv", COL_CKV), ("gq", COL_GQ), ("gk", COL_GK),
                        ("gv", COL_GV), ("og", COL_OG), ("lr", COL_LR)):
        cols[start:start + len(src[name])] = src[name]
    half = MLA_ROPE // 2
    x1, x2 = src["kr"][:half], src["kr"][half:]
    cols[COL_KR:COL_KR + LANES] = np.concatenate([x1, x1, x2, x2])
    return cols


def _w_uq_columns():
    half = MLA_ROPE // 2
    nope = [h * MLA_QK + d for h in range(MLA_HEADS) for d in range(MLA_NOPE)]
    rope = []
    for j in range(MLA_HEADS // 2):
        a, b = 2 * j, 2 * j + 1
        for part in (0, 1):
            for h in (a, b):
                rope += [h * MLA_QK + MLA_NOPE + part * half + r for r in range(half)]
    return np.array(nope + rope, np.int64)


def _take_cols(w, cols, dtype):
    pieces = []
    start = 0
    for i in range(1, len(cols) + 1):
        if i == len(cols) or cols[i] != (cols[i - 1] + 1 if cols[i - 1] >= 0 else -1):
            if cols[start] < 0:
                pieces.append(jnp.zeros((w.shape[0], i - start), dtype))
            else:
                pieces.append(w[:, int(cols[start]):int(cols[start]) + i - start].astype(dtype))
            start = i
    return jnp.concatenate(pieces, axis=1)


def _rope_tables(positions):
    half = MLA_ROPE // 2
    inv_freq = ROPE_THETA ** (-jnp.arange(0, MLA_ROPE, 2, dtype=F32) / MLA_ROPE)
    ang = positions.astype(F32).reshape(-1, 1) * inv_freq
    cos = jnp.cos(ang)
    sin = jnp.sin(ang)
    return jnp.tile(cos, (1, 4)), jnp.concatenate([-sin, -sin, sin, sin], axis=1)


def kernel(x, mem, positions, norm_mem, norm_mix, w_in, mla_q_norm, w_uq, mla_kv_norm, w_ukv, mla_out_norm,
           w_gate_up, b_gate, gla_out_norm, w_out, norm_cross, w_cq, w_ck, w_cv, w_co, norm_ffn, w_peer_q,
           peer_sub_keys, peer_u, peer_v, norm_final):
    batch, seq, d = x.shape
    mem_tokens = mem.shape[1]
    m = batch * seq
    h = x.reshape(m, d)
    cos_t, sin_t = _rope_tables(positions)

    mn_kv = norm_matmul(mem.reshape(batch * mem_tokens, d), norm_mem,
                        jnp.concatenate([w_ck[0], w_cv[0]], axis=1).astype(BF16),
                        k=d, tm=512, tn=1024, out_dtype=BF16, name="mem_kv_proj")

    for l in range(norm_mix.shape[0]):
        w_in_p = _take_cols(w_in[l], _w_in_columns(), BF16)
        h_in = norm_matmul(h, norm_mix[l], w_in_p, k=d, tm=512, tn=1024, out_dtype=BF16, name="in_proj")

        w_uq_p = _take_cols(w_uq[l], _w_uq_columns(), BF16)
        q_full = mla_q_proj(h_in, mla_q_norm[l], w_uq_p, cos_t, sin_t, tm=512)
        kv, k_rope = mla_kv_proj(h_in, mla_kv_norm[l], w_ukv[l].astype(BF16), cos_t, sin_t, tm=512)
        o_mla = mla_attention(q_full, kv, k_rope, batch=batch, seq=seq, blk=512)

        w_gate_pad = jnp.zeros((LANES, GLA_K_WIDTH), F32).at[:GLA_RANK].set(w_gate_up[l]).astype(BF16)
        o_gla = gla_mixer(h_in, w_gate_pad, b_gate[l], gla_out_norm[l], batch=batch, seq=seq, sb=1024)

        h = mixer_out_proj(o_mla, mla_out_norm[l], o_gla, w_out[l].astype(BF16), h, tm=512, tn=1024)

        qx = norm_matmul(h, norm_cross[l], w_cq[l].astype(BF16), k=d, tm=512, tn=1024, out_dtype=BF16,
                         name="xattn_q_proj")
        h = cross_attention(qx, mn_kv, w_co[l].astype(BF16), h, batch=batch, seq=seq,
                            mem_tokens=mem_tokens, tm=256)

        qp, xq, sx = norm_matmul(h, norm_ffn[l], w_peer_q[l].astype(BF16), k=d, tm=512, tn=512, out_dtype=BF16,
                                 emit_xq=True, name="peer_q_proj")
        keys = peer_sub_keys[l].reshape(PEER_HEADS * 2, PEER_KEYS, PEER_HALF).astype(BF16)
        eid, gate = peer_topk(qp, keys, tm=256)
        g = peer_gates(eid, gate, tm=128)
        uq, su = quantize_rows(peer_u[l], tr=512, name="peer_u_quant")
        vq, sv = quantize_rows(peer_v[l], tr=512, name="peer_v_quant")
        h = peer_dense(xq, sx, uq, su, g, vq, sv, h, norm_final,
                       final_norm=(l + 1 == norm_mix.shape[0]), tm=512, te=512)
    return h.reshape(batch, seq, d)
```

```python
import functools
import math

import numpy as np
import jax
import jax.numpy as jnp
from jax import lax
from jax.experimental import pallas as pl
from jax.experimental.pallas import tpu as pltpu

F32 = jnp.float32
BF16 = jnp.bfloat16
F8 = jnp.float8_e4m3fn
I32 = jnp.int32
FP8_AMAX = 256.0

EPS = 1e-6
ROPE_THETA = 10000.0

MLA_HEADS = 16
MLA_Q_LORA = 1024
MLA_KV_LORA = 512
MLA_NOPE = 128
MLA_ROPE = 64
MLA_QK = MLA_NOPE + MLA_ROPE
MLA_V = 128
MLA_WIDTH = MLA_HEADS * MLA_V

GLA_HEADS = 4
GLA_HEAD_K = 256
GLA_HEAD_V = 512
GLA_K_WIDTH = GLA_HEADS * GLA_HEAD_K
GLA_V_WIDTH = GLA_HEADS * GLA_HEAD_V
GLA_RANK = 16
GLA_TAU = 16.0
GLA_CHUNK = 64

XATTN_HEADS = 4
XATTN_DIM = 256
XATTN_WIDTH = XATTN_HEADS * XATTN_DIM

PEER_HEADS = 8
PEER_KEYS = 128
PEER_HALF = 128
PEER_TOPK = 16
PEER_SLOTS = PEER_HEADS * PEER_TOPK
PEER_EXPERTS = PEER_KEYS * PEER_KEYS

LANES = 128
VMEM_LIMIT = 56 * 1024 * 1024

COL_CQ = 0
COL_CKV = COL_CQ + MLA_Q_LORA
COL_GQ = COL_CKV + MLA_KV_LORA
COL_GK = COL_GQ + GLA_K_WIDTH
COL_GV = COL_GK + GLA_K_WIDTH
COL_OG = COL_GV + GLA_V_WIDTH
COL_KR = COL_OG + GLA_V_WIDTH
COL_LR = COL_KR + LANES
IN_PAD = 8192

NEG = -1e30


def _cparams(*sem):
    return pltpu.CompilerParams(dimension_semantics=sem, vmem_limit_bytes=VMEM_LIMIT)


def _rms(x, g):
    ms = jnp.mean(x * x, axis=-1, keepdims=True)
    return x * lax.rsqrt(ms + EPS) * g


def _dot(a, b):
    return jnp.dot(a, b, preferred_element_type=F32)


def _dot_nt(a, b):
    return lax.dot_general(a, b, (((1,), (1,)), ((), ())), preferred_element_type=F32)


def _dot_tn(a, b):
    return lax.dot_general(a, b, (((0,), (0,)), ((), ())), preferred_element_type=F32)


def _quantize_rows(x):
    amax = jnp.max(jnp.abs(x), axis=-1, keepdims=True)
    scale = jnp.where(amax > 0.0, amax * (1.0 / FP8_AMAX), 1.0)
    return (x * (1.0 / scale)).astype(F8), scale


def _norm_matmul_kernel(x_ref, g_ref, w_ref, o_ref, *rest, emit_xq, whole_n):
    if emit_xq:
        xq_ref, sx_ref, xn_ref = rest
    else:
        (xn_ref,) = rest

    def normalise():
        xn = _rms(x_ref[...].astype(F32), g_ref[...])
        xn_ref[...] = xn.astype(BF16)
        if emit_xq:
            xq_ref[...], sx_ref[...] = _quantize_rows(xn)

    if whole_n:
        normalise()
    else:
        pl.when(pl.program_id(1) == 0)(normalise)
    o_ref[...] = _dot(xn_ref[...], w_ref[...]).astype(o_ref.dtype)


def norm_matmul(x, gain, w, *, k, x_col_blk=0, tm, tn, out_dtype, emit_xq=False, name):
    m = x.shape[0]
    n = w.shape[1]
    tm = min(tm, m)
    tn = min(tn, n)
    out_shape = [jax.ShapeDtypeStruct((m, n), out_dtype)]
    out_specs = [pl.BlockSpec((tm, tn), lambda i, j: (i, j))]
    if emit_xq:
        out_shape += [jax.ShapeDtypeStruct((m, k), F8), jax.ShapeDtypeStruct((m, 1), F32)]
        out_specs += [pl.BlockSpec((tm, k), lambda i, j: (i, 0)), pl.BlockSpec((tm, 1), lambda i, j: (i, 0))]
    whole_n = tn == n
    w_mode = pl.Buffered(1) if whole_n else None
    res = pl.pallas_call(
        functools.partial(_norm_matmul_kernel, emit_xq=emit_xq, whole_n=whole_n),
        out_shape=out_shape,
        grid=(m // tm, n // tn),
        in_specs=[pl.BlockSpec((tm, k), lambda i, j: (i, x_col_blk)),
                  pl.BlockSpec((1, k), lambda i, j: (0, 0)),
                  pl.BlockSpec((k, tn), lambda i, j: (0, j), pipeline_mode=w_mode)],
        out_specs=out_specs,
        scratch_shapes=[pltpu.VMEM((tm, k), BF16)],
        compiler_params=_cparams("parallel", "arbitrary"),
        name=name,
    )(x, gain.reshape(1, k).astype(F32), w)
    return res if emit_xq else res[0]


def _quantize_kernel(w_ref, q_ref, s_ref):
    q_ref[...], s_ref[...] = _quantize_rows(w_ref[...])


def quantize_rows(w, *, tr, name):
    r, d = w.shape
    return pl.pallas_call(
        _quantize_kernel,
        out_shape=[jax.ShapeDtypeStruct((r, d), F8), jax.ShapeDtypeStruct((r, 1), F32)],
        grid=(r // tr,),
        in_specs=[pl.BlockSpec((tr, d), lambda i: (i, 0))],
        out_specs=[pl.BlockSpec((tr, d), lambda i: (i, 0)), pl.BlockSpec((tr, 1), lambda i: (i, 0))],
        compiler_params=_cparams("parallel"),
        name=name,
    )(w)


def _rope_pair(r, cosv, sinv):
    return r * cosv + pltpu.roll(r, 2 * 32, axis=1) * sinv


def _qproj_kernel(c_ref, g_ref, w_ref, cos_ref, sin_ref, o_ref, *, scale):
    xn = _rms(c_ref[...].astype(F32), g_ref[...]).astype(BF16)
    q = _dot(xn, w_ref[...])
    cosv = cos_ref[...]
    sinv = sin_ref[...]
    lane = lax.broadcasted_iota(I32, (1, LANES), 1)
    rope_base = MLA_HEADS * MLA_NOPE
    for j in range(MLA_HEADS // 2):
        r = q[:, rope_base + j * LANES: rope_base + (j + 1) * LANES]
        r = _rope_pair(r, cosv, sinv) * scale
        for p in range(2):
            h = 2 * j + p
            own = ((lane // 32) % 2) == p
            o_ref[:, h * 256: h * 256 + 128] = (q[:, h * 128:(h + 1) * 128] * scale).astype(o_ref.dtype)
            o_ref[:, h * 256 + 128: h * 256 + 256] = jnp.where(own, r, 0.0).astype(o_ref.dtype)


def mla_q_proj(h_in, gain, w_uq_perm, cos_t, sin_t, *, tm):
    m = h_in.shape[0]
    tm = min(tm, m)
    return pl.pallas_call(
        functools.partial(_qproj_kernel, scale=MLA_QK ** -0.5 * math.log2(math.e)),
        out_shape=jax.ShapeDtypeStruct((m, MLA_HEADS * 256), BF16),
        grid=(m // tm,),
        in_specs=[pl.BlockSpec((tm, MLA_Q_LORA), lambda i: (i, COL_CQ // MLA_Q_LORA)),
                  pl.BlockSpec((1, MLA_Q_LORA), lambda i: (0, 0)),
                  pl.BlockSpec(w_uq_perm.shape, lambda i: (0, 0)),
                  pl.BlockSpec((tm, LANES), lambda i: (i, 0)),
                  pl.BlockSpec((tm, LANES), lambda i: (i, 0))],
        out_specs=pl.BlockSpec((tm, MLA_HEADS * 256), lambda i: (i, 0)),
        compiler_params=_cparams("parallel"),
        name="mla_q_proj",
    )(h_in, gain.reshape(1, -1).astype(F32), w_uq_perm, cos_t, sin_t)


def _kvproj_kernel(c_ref, g_ref, w_ref, kr_ref, cos_ref, sin_ref, kv_ref, kro_ref):
    xn = _rms(c_ref[...].astype(F32), g_ref[...]).astype(BF16)
    kv_ref[...] = _dot(xn, w_ref[...]).astype(kv_ref.dtype)
    kro_ref[...] = _rope_pair(kr_ref[...].astype(F32), cos_ref[...], sin_ref[...]).astype(kro_ref.dtype)


def mla_kv_proj(h_in, gain, w_ukv, cos_t, sin_t, *, tm):
    m = h_in.shape[0]
    tm = min(tm, m)
    n = w_ukv.shape[1]
    return pl.pallas_call(
        _kvproj_kernel,
        out_shape=[jax.ShapeDtypeStruct((m, n), BF16), jax.ShapeDtypeStruct((m, LANES), BF16)],
        grid=(m // tm,),
        in_specs=[pl.BlockSpec((tm, MLA_KV_LORA), lambda i: (i, COL_CKV // MLA_KV_LORA)),
                  pl.BlockSpec((1, MLA_KV_LORA), lambda i: (0, 0)),
                  pl.BlockSpec(w_ukv.shape, lambda i: (0, 0)),
                  pl.BlockSpec((tm, LANES), lambda i: (i, COL_KR // LANES)),
                  pl.BlockSpec((tm, LANES), lambda i: (i, 0)),
                  pl.BlockSpec((tm, LANES), lambda i: (i, 0))],
        out_specs=[pl.BlockSpec((tm, n), lambda i: (i, 0)),
                   pl.BlockSpec((tm, LANES), lambda i: (i, 0))],
        compiler_params=_cparams("parallel"),
        name="mla_kv_proj",
    )(h_in, gain.reshape(1, -1).astype(F32), w_ukv, h_in, cos_t, sin_t)


ATTN_UNROLL = 8


def _mla_attn_kernel(q_ref, kv_ref, kr_ref, o_ref, s0_scr, s1_scr, m_scr, acc_scr, *, blk):
    def query_block(qi, carry):
        _mla_attn_query_block(qi, q_ref, kv_ref, kr_ref, o_ref, s0_scr, s1_scr, m_scr, acc_scr, blk=blk)
        return carry

    lax.fori_loop(0, q_ref.shape[0] // blk, query_block, 0)


def _mla_attn_query_block(qi, q_ref, kv_ref, kr_ref, o_ref, s0_scr, s1_scr, m_scr, acc_scr, *, blk):
    q_rows = pl.ds(pl.multiple_of(qi * blk, blk), blk)
    ones = jnp.ones((blk, LANES), BF16)
    m_scr[...] = jnp.full(m_scr.shape, NEG, F32)
    acc_scr[...] = jnp.zeros(acc_scr.shape, F32)

    def scores(j, s_scr):
        start = pl.multiple_of(j * blk, blk)
        k = jnp.concatenate([kv_ref[pl.ds(start, blk), :MLA_NOPE], kr_ref[pl.ds(start, blk), :]], axis=1)
        s_scr[...] = _dot_nt(q_ref[q_rows, :], k)

    def finish(j, s_scr, masked):
        start = pl.multiple_of(j * blk, blk)
        s = s_scr[...]
        if masked:
            row = lax.broadcasted_iota(I32, (blk, blk), 0)
            col = lax.broadcasted_iota(I32, (blk, blk), 1)
            s = jnp.where(col <= row, s, NEG)
        m = m_scr[...]
        m_new = jnp.maximum(m, jnp.max(s, axis=1, keepdims=True))
        alpha = jnp.exp2(m - m_new)
        p = jnp.exp2(s - jnp.concatenate([m_new] * (blk // LANES), axis=1)).astype(BF16)
        m_scr[...] = m_new
        v = jnp.concatenate([kv_ref[pl.ds(start, blk), MLA_NOPE:], ones], axis=1)
        acc_scr[...] = jnp.concatenate([alpha, alpha], axis=1) * acc_scr[...] + _dot(p, v)

    s_scr = (s0_scr, s1_scr)

    def run(j0, n):
        for u in range(n):
            scores(j0 + u + 1, s_scr[(u + 1) % 2])
            finish(j0 + u, s_scr[u % 2], False)

    scores(0, s0_scr)

    def body(t, carry):
        run(t * ATTN_UNROLL, ATTN_UNROLL)
        return carry

    lax.fori_loop(0, qi // ATTN_UNROLL, body, 0)
    done = (qi // ATTN_UNROLL) * ATTN_UNROLL
    n = ATTN_UNROLL // 2
    while n >= 1:
        @pl.when((qi & n) != 0)
        def _(done=done, n=n):
            run(done, n)

        done = done + (qi & n)
        n //= 2

    @pl.when(qi % 2 == 0)
    def _():
        finish(qi, s0_scr, True)

    @pl.when(qi % 2 == 1)
    def _():
        finish(qi, s1_scr, True)

    acc = acc_scr[...]
    o_ref[q_rows, :] = (acc[:, :MLA_V] / acc[:, MLA_V:]).astype(o_ref.dtype)


def mla_attention(q_full, kv, k_rope, *, batch, seq, blk):
    blk = min(blk, seq)
    m = batch * seq
    return pl.pallas_call(
        functools.partial(_mla_attn_kernel, blk=blk),
        out_shape=jax.ShapeDtypeStruct((m, MLA_WIDTH), BF16),
        grid=(batch, MLA_HEADS),
        in_specs=[pl.BlockSpec((seq, 256), lambda b, h: (b, h)),
                  pl.BlockSpec((seq, 256), lambda b, h: (b, h)),
                  pl.BlockSpec((seq, LANES), lambda b, h: (b, 0))],
        out_specs=pl.BlockSpec((seq, MLA_V), lambda b, h: (b, h)),
        scratch_shapes=[pltpu.VMEM((blk, blk), F32), pltpu.VMEM((blk, blk), F32),
                        pltpu.VMEM((blk, LANES), F32), pltpu.VMEM((blk, 2 * MLA_V), F32)],
        compiler_params=_cparams("parallel", "parallel"),
        name="mla_attention",
    )(q_full, kv, k_rope)


def _gla_kernel(q_ref, k_ref, v_ref, og_ref, lr_ref, wg_ref, bg_ref, gn_ref, o_ref, st_ref, *, sb, c):
    @pl.when(pl.program_id(2) == 0)
    def _():
        st_ref[...] = jnp.zeros_like(st_ref)

    z = _dot(lr_ref[...], wg_ref[...]) + bg_ref[...]
    log_a = (jnp.minimum(z, 0.0) - jnp.log(1.0 + jnp.exp(-jnp.abs(z)))) * (1.0 / GLA_TAU)
    rows = lax.broadcasted_iota(I32, (c, c), 0)
    cols = lax.broadcasted_iota(I32, (c, c), 1)
    causal = cols <= rows
    tril = causal.astype(F32)
    gn = gn_ref[...]
    scale = GLA_HEAD_K ** -0.5

    for n in range(sb // c):
        sl = slice(n * c, (n + 1) * c)
        g = log_a[sl, :]
        b = jnp.dot(tril, g, preferred_element_type=F32, precision=lax.Precision.HIGHEST)
        b_last = b[c - 1:c, :]
        q = q_ref[sl, :].astype(F32)
        k = k_ref[sl, :].astype(F32)
        v = v_ref[sl, :]
        qe = (q * scale * jnp.exp(b)).astype(BF16)
        ke = (k * jnp.exp(-b)).astype(BF16)
        kd = (k * jnp.exp(b_last - b)).astype(BF16)
        att = jnp.where(causal, _dot_nt(qe, ke), 0.0).astype(BF16)
        st = st_ref[...]
        o = _dot(att, v) + _dot_nt(qe, st.astype(BF16))
        st_ref[...] = st * jnp.exp(b_last) + _dot_tn(v, kd)
        og = og_ref[sl, :].astype(F32)
        o = _rms(o, gn) * (og * (1.0 / (1.0 + jnp.exp(-og))))
        o_ref[sl, :] = o.astype(o_ref.dtype)


def gla_mixer(h_in, w_gate_pad, b_gate, gla_out_norm, *, batch, seq, sb):
    sb = min(sb, seq)
    nsb = seq // sb
    m = batch * seq
    return pl.pallas_call(
        functools.partial(_gla_kernel, sb=sb, c=GLA_CHUNK),
        out_shape=jax.ShapeDtypeStruct((m, GLA_V_WIDTH), BF16),
        grid=(batch, GLA_HEADS, nsb),
        in_specs=[pl.BlockSpec((sb, GLA_HEAD_K), lambda b, h, s: (b * nsb + s, COL_GQ // GLA_HEAD_K + h)),
                  pl.BlockSpec((sb, GLA_HEAD_K), lambda b, h, s: (b * nsb + s, COL_GK // GLA_HEAD_K + h)),
                  pl.BlockSpec((sb, GLA_HEAD_V), lambda b, h, s: (b * nsb + s, COL_GV // GLA_HEAD_V + h)),
                  pl.BlockSpec((sb, GLA_HEAD_V), lambda b, h, s: (b * nsb + s, COL_OG // GLA_HEAD_V + h)),
                  pl.BlockSpec((sb, LANES), lambda b, h, s: (b * nsb + s, COL_LR // LANES)),
                  pl.BlockSpec((LANES, GLA_HEAD_K), lambda b, h, s: (0, h)),
                  pl.BlockSpec((1, GLA_HEAD_K), lambda b, h, s: (0, h)),
                  pl.BlockSpec((1, GLA_HEAD_V), lambda b, h, s: (0, 0))],
        out_specs=pl.BlockSpec((sb, GLA_HEAD_V), lambda b, h, s: (b * nsb + s, h)),
        scratch_shapes=[pltpu.VMEM((GLA_HEAD_V, GLA_HEAD_K), F32)],
        compiler_params=_cparams("parallel", "parallel", "arbitrary"),
        name="gla_mixer",
    )(h_in, h_in, h_in, h_in, h_in, w_gate_pad, b_gate.reshape(1, -1).astype(F32),
      gla_out_norm.reshape(1, -1).astype(F32))


def _out_proj_kernel(om_ref, g_ref, og_ref, w_ref, x_ref, o_ref, mix_ref):
    @pl.when(pl.program_id(1) == 0)
    def _():
        mix_ref[:, :MLA_WIDTH] = _rms(om_ref[...].astype(F32), g_ref[...]).astype(BF16)
        mix_ref[:, MLA_WIDTH:] = og_ref[...]

    o_ref[...] = x_ref[...] + _dot(mix_ref[...], w_ref[...])


def mixer_out_proj(o_mla, mla_out_norm, o_gla, w_out, x, *, tm, tn):
    m, d = x.shape
    tm = min(tm, m)
    tn = min(tn, d)
    kmix = MLA_WIDTH + GLA_V_WIDTH
    return pl.pallas_call(
        _out_proj_kernel,
        out_shape=jax.ShapeDtypeStruct((m, d), F32),
        grid=(m // tm, d // tn),
        in_specs=[pl.BlockSpec((tm, MLA_WIDTH), lambda i, j: (i, 0)),
                  pl.BlockSpec((1, MLA_WIDTH), lambda i, j: (0, 0)),
                  pl.BlockSpec((tm, GLA_V_WIDTH), lambda i, j: (i, 0)),
                  pl.BlockSpec((kmix, tn), lambda i, j: (0, j)),
                  pl.BlockSpec((tm, tn), lambda i, j: (i, j))],
        out_specs=pl.BlockSpec((tm, tn), lambda i, j: (i, j)),
        scratch_shapes=[pltpu.VMEM((tm, kmix), BF16)],
        compiler_params=_cparams("parallel", "arbitrary"),
        name="mixer_out_proj",
    )(o_mla, mla_out_norm.reshape(1, -1).astype(F32), o_gla, w_out, x)


def _xattn_kernel(q_ref, kv_ref, w_ref, h_ref, o_ref):
    scale = XATTN_DIM ** -0.5
    outs = []
    for h in range(XATTN_HEADS):
        q = q_ref[:, h * XATTN_DIM:(h + 1) * XATTN_DIM]
        k = kv_ref[:, h * XATTN_DIM:(h + 1) * XATTN_DIM]
        v = kv_ref[:, XATTN_WIDTH + h * XATTN_DIM: XATTN_WIDTH + (h + 1) * XATTN_DIM]
        s = _dot_nt(q, k) * scale
        p = jnp.exp(s - jnp.max(s, axis=1, keepdims=True))
        p = p / jnp.sum(p, axis=1, keepdims=True)
        outs.append(_dot(p.astype(BF16), v).astype(BF16))
    o = jnp.concatenate(outs, axis=1)
    o_ref[...] = h_ref[...] + _dot(o, w_ref[...])


def cross_attention(qx, kvm, w_co, h1, *, batch, seq, mem_tokens, tm):
    m, d = h1.shape
    tm = min(tm, seq)
    per_b = seq // tm
    return pl.pallas_call(
        _xattn_kernel,
        out_shape=jax.ShapeDtypeStruct((m, d), F32),
        grid=(m // tm,),
        in_specs=[pl.BlockSpec((tm, XATTN_WIDTH), lambda i: (i, 0)),
                  pl.BlockSpec((mem_tokens, 2 * XATTN_WIDTH), lambda i: (i // per_b, 0)),
                  pl.BlockSpec(w_co.shape, lambda i: (0, 0)),
                  pl.BlockSpec((tm, d), lambda i: (i, 0))],
        out_specs=pl.BlockSpec((tm, d), lambda i: (i, 0)),
        compiler_params=_cparams("parallel"),
        name="cross_attention",
    )(qx, kvm, w_co, h1)


INT_MIN = -2 ** 31


def _order_key(x):
    b = lax.bitcast_convert_type(x, I32)
    return b ^ (lax.shift_right_arithmetic(b, 31) & 0x7FFFFFFF)


def _order_key_inv(k):
    return lax.bitcast_convert_type(k ^ (lax.shift_right_arithmetic(k, 31) & 0x7FFFFFFF), F32)


def _pack_keys(x, bits):
    low = (1 << bits) - 1
    row = lax.broadcasted_iota(I32, x.shape, 0)
    return (_order_key(x) & ~low) | (low - row)


def _top16_rows(s, bits):
    low = (1 << bits) - 1
    kp = _pack_keys(s, bits)
    vals, idxs = [], []
    for _ in range(PEER_TOPK):
        mx = jnp.max(kp, axis=0, keepdims=True)
        vals.append(_order_key_inv(mx & ~low))
        idxs.append(low - (mx & low))
        kp = jnp.where(kp == mx, INT_MIN, kp)
    return vals, idxs


_PAIRS = [(a, b) for a in range(PEER_TOPK) for b in range(PEER_TOPK) if (a + 1) * (b + 1) <= PEER_TOPK]
_PAIR_ROWS = -(-len(_PAIRS) // 8) * 8


def _peer_topk_kernel(q_ref, keys_ref, eid_ref, gate_ref, eid_t, gate_t, *, tm):
    def head(h, carry):
        tops = []
        for p in range(2):
            col = pl.multiple_of((2 * h + p) * PEER_HALF, PEER_HALF)
            qh = q_ref[:, pl.ds(col, PEER_HALF)]
            s = _dot_nt(keys_ref[2 * h + p], qh)
            tops.append(_top16_rows(s, 7))
        (s1, i1), (s2, i2) = tops
        cand_s = [s1[a] + s2[b] for a, b in _PAIRS]
        cand_e = [i1[a] * PEER_KEYS + i2[b] for a, b in _PAIRS]
        pad = _PAIR_ROWS - len(_PAIRS)
        cs = jnp.concatenate(cand_s + [jnp.full((pad, tm), -jnp.inf, F32)], axis=0)
        ce = jnp.concatenate(cand_e + [jnp.zeros((pad, tm), I32)], axis=0)
        low = 63
        kp = _pack_keys(cs, 6)
        sel_s, sel_e = [], []
        for _ in range(PEER_TOPK):
            mx = jnp.max(kp, axis=0, keepdims=True)
            hit = kp == mx
            sel_s.append(_order_key_inv(mx & ~low))
            sel_e.append(jnp.max(jnp.where(hit, ce, -1), axis=0, keepdims=True))
            kp = jnp.where(hit, INT_MIN, kp)
        top_s = jnp.concatenate(sel_s, axis=0)
        top_e = jnp.concatenate(sel_e, axis=0)
        w = jnp.exp(top_s - top_s[0:1, :])
        gate = w / jnp.sum(w, axis=0, keepdims=True)
        row = pl.multiple_of(h * PEER_TOPK, PEER_TOPK)
        eid_t[pl.ds(row, PEER_TOPK), :] = top_e
        gate_t[pl.ds(row, PEER_TOPK), :] = gate
        return carry

    lax.fori_loop(0, PEER_HEADS, head, 0)
    eid_ref[...] = eid_t[...].T
    gate_ref[...] = gate_t[...].T


def peer_topk(qp, keys, *, tm):
    m = qp.shape[0]
    tm = min(tm, m)
    return pl.pallas_call(
        functools.partial(_peer_topk_kernel, tm=tm),
        out_shape=[jax.ShapeDtypeStruct((m, PEER_SLOTS), I32), jax.ShapeDtypeStruct((m, PEER_SLOTS), F32)],
        grid=(m // tm,),
        in_specs=[pl.BlockSpec((tm, qp.shape[1]), lambda i: (i, 0)),
                  pl.BlockSpec(keys.shape, lambda i: (0, 0, 0))],
        out_specs=[pl.BlockSpec((tm, PEER_SLOTS), lambda i: (i, 0)),
                   pl.BlockSpec((tm, PEER_SLOTS), lambda i: (i, 0))],
        scratch_shapes=[pltpu.VMEM((PEER_SLOTS, tm), I32), pltpu.VMEM((PEER_SLOTS, tm), F32)],
        compiler_params=_cparams("parallel"),
        name="peer_topk",
    )(qp, keys)


GATE_UNROLL = 32


def _peer_gates_kernel(eid_ref, gate_ref, o_ref, g3_ref, *, tm, pitch):
    sub = lax.broadcasted_iota(I32, (PEER_KEYS, PEER_SLOTS), 0)

    def token(t, carry):
        e = eid_ref[pl.ds(t, 1), :]
        g = gate_ref[pl.ds(t, 1), :]
        hi = lax.shift_right_logical(e, 7)
        lo = jnp.bitwise_and(e, PEER_KEYS - 1)
        a1 = jnp.where(sub == hi, g, 0.0).astype(BF16)
        a2 = jnp.where(sub == lo, 1.0, 0.0).astype(BF16)
        g3_ref[pl.ds(t, PEER_KEYS, stride=pitch), :] = _dot_nt(a1, a2)
        return carry

    lax.fori_loop(0, tm, token, 0, unroll=GATE_UNROLL)
    for e1 in range(PEER_KEYS):
        o_ref[:, e1 * PEER_KEYS:(e1 + 1) * PEER_KEYS] = g3_ref[e1 * pitch:e1 * pitch + tm, :].astype(o_ref.dtype)


def peer_gates(eid, gate, *, tm):
    m = eid.shape[0]
    tm = min(tm, m)
    pitch = tm + 8
    return pl.pallas_call(
        functools.partial(_peer_gates_kernel, tm=tm, pitch=pitch),
        out_shape=jax.ShapeDtypeStruct((m, PEER_EXPERTS), BF16),
        grid=(m // tm,),
        in_specs=[pl.BlockSpec((tm, PEER_SLOTS), lambda i: (i, 0)),
                  pl.BlockSpec((tm, PEER_SLOTS), lambda i: (i, 0))],
        out_specs=pl.BlockSpec((tm, PEER_EXPERTS), lambda i: (i, 0)),
        scratch_shapes=[pltpu.VMEM((PEER_KEYS * pitch, PEER_KEYS), F32)],
        compiler_params=_cparams("parallel"),
        name="peer_gates",
    )(eid, gate)


def _peer_dense_kernel(x_ref, sx_ref, u_ref, su_ref, g_ref, v_ref, sv_ref, h_ref, gn_ref, o_ref, aq_scr, sa_scr,
                       *, final_norm, n_blocks):
    j = pl.program_id(1)
    nj = n_blocks

    def step(p, do_values, do_gates):
        if do_gates:
            s = _dot_nt(x_ref[...], u_ref[...])
        if do_values:
            pv = _dot(aq_scr[1 - p], v_ref[...])
        if do_gates:
            s = s * sx_ref[...] * su_ref[...]
            a = 0.5 * s * (1.0 + lax.erf(s * (2.0 ** -0.5)))
            aq_scr[p], sa_scr[p] = _quantize_rows(a * g_ref[...].astype(F32) * sv_ref[...])
        if do_values:
            o_ref[...] += sa_scr[1 - p] * pv

    @pl.when(j == 0)
    def _():
        o_ref[...] = jnp.zeros_like(o_ref)
        step(0, False, True)

    for p in range(2):
        @pl.when(jnp.logical_and(jnp.logical_and(j > 0, j < nj), j % 2 == p))
        def _(p=p):
            step(p, True, True)

    @pl.when(j == nj)
    def _():
        step(nj % 2, True, False)
        y = h_ref[...] + o_ref[...]
        o_ref[...] = _rms(y, gn_ref[...]) if final_norm else y


def peer_dense(xq, sx, uq, su, g, vq, sv, h, gain, *, final_norm, tm, te):
    m, d = xq.shape
    e = uq.shape[0]
    tm = min(tm, m)
    nj = e // te
    once = pl.Buffered(1)

    def cur(j):
        return jnp.minimum(j, nj - 1)

    def prev(j):
        return jnp.maximum(j - 1, 0)

    return pl.pallas_call(
        functools.partial(_peer_dense_kernel, final_norm=final_norm, n_blocks=nj),
        out_shape=jax.ShapeDtypeStruct((m, d), F32),
        grid=(m // tm, nj + 1),
        in_specs=[pl.BlockSpec((tm, d), lambda i, j: (i, 0)),
                  pl.BlockSpec((tm, 1), lambda i, j: (i, 0)),
                  pl.BlockSpec((te, d), lambda i, j: (cur(j), 0)),
                  pl.BlockSpec((1, te), lambda i, j: (0, cur(j))),
                  pl.BlockSpec((tm, te), lambda i, j: (i, cur(j))),
                  pl.BlockSpec((te, d), lambda i, j: (prev(j), 0)),
                  pl.BlockSpec((1, te), lambda i, j: (0, cur(j))),
                  pl.BlockSpec((tm, d), lambda i, j: (i, 0), pipeline_mode=once),
                  pl.BlockSpec((1, d), lambda i, j: (0, 0))],
        out_specs=pl.BlockSpec((tm, d), lambda i, j: (i, 0)),
        scratch_shapes=[pltpu.VMEM((2, tm, te), F8), pltpu.VMEM((2, tm, 1), F32)],
        compiler_params=_cparams("parallel", "arbitrary"),
        name="peer_dense",
    )(xq, sx, uq, su.reshape(1, e), g, vq, sv.reshape(1, e), h, gain.reshape(1, d).astype(F32))


def _w_in_columns():
    src = {}
    off = 0
    for name, width in (("cq", MLA_Q_LORA), ("ckv", MLA_KV_LORA), ("kr", MLA_ROPE), ("gq", GLA_K_WIDTH),
                        ("gk", GLA_K_WIDTH), ("gv", GLA_V_WIDTH), ("lr", GLA_RANK), ("og", GLA_V_WIDTH)):
        src[name] = np.arange(off, off + width)
        off += width
    cols = np.full((IN_PAD,), -1, np.int64)
    for name, start in (("cq", COL_CQ), ("ckv", COL_CKV), ("gq", COL_GQ), ("gk", COL_GK),
                        ("gv", COL_GV), ("og", COL_OG), ("lr", COL_LR)):
        cols[start:start + len(src[name])] = src[name]
    half = MLA_ROPE // 2
    x1, x2 = src["kr"][:half], src["kr"][half:]
    cols[COL_KR:COL_KR + LANES] = np.concatenate([x1, x1, x2, x2])
    return cols


def _w_uq_columns():
    half = MLA_ROPE // 2
    nope = [h * MLA_QK + d for h in range(MLA_HEADS) for d in range(MLA_NOPE)]
    rope = []
    for j in range(MLA_HEADS // 2):
        a, b = 2 * j, 2 * j + 1
        for part in (0, 1):
            for h in (a, b):
                rope += [h * MLA_QK + MLA_NOPE + part * half + r for r in range(half)]
    return np.array(nope + rope, np.int64)


def _take_cols(w, cols, dtype):
    pieces = []
    start = 0
    for i in range(1, len(cols) + 1):
        if i == len(cols) or cols[i] != (cols[i - 1] + 1 if cols[i - 1] >= 0 else -1):
            if cols[start] < 0:
                pieces.append(jnp.zeros((w.shape[0], i - start), dtype))
            else:
                pieces.append(w[:, int(cols[start]):int(cols[start]) + i - start].astype(dtype))
            start = i
    return jnp.concatenate(pieces, axis=1)


def _rope_tables(positions):
    half = MLA_ROPE // 2
    inv_freq = ROPE_THETA ** (-jnp.arange(0, MLA_ROPE, 2, dtype=F32) / MLA_ROPE)
    ang = positions.astype(F32).reshape(-1, 1) * inv_freq
    cos = jnp.cos(ang)
    sin = jnp.sin(ang)
    return jnp.tile(cos, (1, 4)), jnp.concatenate([-sin, -sin, sin, sin], axis=1)


def kernel(x, mem, positions, norm_mem, norm_mix, w_in, mla_q_norm, w_uq, mla_kv_norm, w_ukv, mla_out_norm,
           w_gate_up, b_gate, gla_out_norm, w_out, norm_cross, w_cq, w_ck, w_cv, w_co, norm_ffn, w_peer_q,
           peer_sub_keys, peer_u, peer_v, norm_final):
    batch, seq, d = x.shape
    mem_tokens = mem.shape[1]
    m = batch * seq
    h = x.reshape(m, d)
    cos_t, sin_t = _rope_tables(positions)

    mn_kv = norm_matmul(mem.reshape(batch * mem_tokens, d), norm_mem,
                        jnp.concatenate([w_ck[0], w_cv[0]], axis=1).astype(BF16),
                        k=d, tm=512, tn=1024, out_dtype=BF16, name="mem_kv_proj")

    for l in range(norm_mix.shape[0]):
        w_in_p = _take_cols(w_in[l], _w_in_columns(), BF16)
        h_in = norm_matmul(h, norm_mix[l], w_in_p, k=d, tm=512, tn=1024, out_dtype=BF16, name="in_proj")

        w_uq_p = _take_cols(w_uq[l], _w_uq_columns(), BF16)
        q_full = mla_q_proj(h_in, mla_q_norm[l], w_uq_p, cos_t, sin_t, tm=512)
        kv, k_rope = mla_kv_proj(h_in, mla_kv_norm[l], w_ukv[l].astype(BF16), cos_t, sin_t, tm=512)
        o_mla = mla_attention(q_full, kv, k_rope, batch=batch, seq=seq, blk=512)

        w_gate_pad = jnp.zeros((LANES, GLA_K_WIDTH), F32).at[:GLA_RANK].set(w_gate_up[l]).astype(BF16)
        o_gla = gla_mixer(h_in, w_gate_pad, b_gate[l], gla_out_norm[l], batch=batch, seq=seq, sb=1024)

        h = mixer_out_proj(o_mla, mla_out_norm[l], o_gla, w_out[l].astype(BF16), h, tm=512, tn=1024)

        qx = norm_matmul(h, norm_cross[l], w_cq[l].astype(BF16), k=d, tm=512, tn=1024, out_dtype=BF16,
                         name="xattn_q_proj")
        h = cross_attention(qx, mn_kv, w_co[l].astype(BF16), h, batch=batch, seq=seq,
                            mem_tokens=mem_tokens, tm=256)

        qp, xq, sx = norm_matmul(h, norm_ffn[l], w_peer_q[l].astype(BF16), k=d, tm=256, tn=2048, out_dtype=BF16,
                                 emit_xq=True, name="peer_q_proj")
        keys = peer_sub_keys[l].reshape(PEER_HEADS * 2, PEER_KEYS, PEER_HALF).astype(BF16)
        eid, gate = peer_topk(qp, keys, tm=256)
        g = peer_gates(eid, gate, tm=128)
        uq, su = quantize_rows(peer_u[l], tr=512, name="peer_u_quant")
        vq, sv = quantize_rows(peer_v[l], tr=512, name="peer_v_quant")
        h = peer_dense(xq, sx, uq, su, g, vq, sv, h, norm_final,
                       final_norm=(l + 1 == norm_mix.shape[0]), tm=512, te=512)
    return h.reshape(batch, seq, d)
```

```python
import functools
import math

import numpy as np
import jax
import jax.numpy as jnp
from jax import lax
from jax.experimental import pallas as pl
from jax.experimental.pallas import tpu as pltpu

F32 = jnp.float32
BF16 = jnp.bfloat16
F8 = jnp.float8_e4m3fn
I32 = jnp.int32
FP8_AMAX = 256.0

EPS = 1e-6
ROPE_THETA = 10000.0

MLA_HEADS = 16
MLA_Q_LORA = 1024
MLA_KV_LORA = 512
MLA_NOPE = 128
MLA_ROPE = 64
MLA_QK = MLA_NOPE + MLA_ROPE
MLA_V = 128
MLA_WIDTH = MLA_HEADS * MLA_V

GLA_HEADS = 4
GLA_HEAD_K = 256
GLA_HEAD_V = 512
GLA_K_WIDTH = GLA_HEADS * GLA_HEAD_K
GLA_V_WIDTH = GLA_HEADS * GLA_HEAD_V
GLA_RANK = 16
GLA_TAU = 16.0
GLA_CHUNK = 64

XATTN_HEADS = 4
XATTN_DIM = 256
XATTN_WIDTH = XATTN_HEADS * XATTN_DIM

PEER_HEADS = 8
PEER_KEYS = 128
PEER_HALF = 128
PEER_TOPK = 16
PEER_SLOTS = PEER_HEADS * PEER_TOPK
PEER_EXPERTS = PEER_KEYS * PEER_KEYS

LANES = 128
VMEM_LIMIT = 56 * 1024 * 1024

COL_CQ = 0
COL_CKV = COL_CQ + MLA_Q_LORA
COL_GQ = COL_CKV + MLA_KV_LORA
COL_GK = COL_GQ + GLA_K_WIDTH
COL_GV = COL_GK + GLA_K_WIDTH
COL_OG = COL_GV + GLA_V_WIDTH
COL_KR = COL_OG + GLA_V_WIDTH
COL_LR = COL_KR + LANES
IN_PAD = 8192

NEG = -1e30


def _cparams(*sem):
    return pltpu.CompilerParams(dimension_semantics=sem, vmem_limit_bytes=VMEM_LIMIT)


def _rms(x, g):
    ms = jnp.mean(x * x, axis=-1, keepdims=True)
    return x * lax.rsqrt(ms + EPS) * g


def _dot(a, b):
    return jnp.dot(a, b, preferred_element_type=F32)


def _dot_nt(a, b):
    return lax.dot_general(a, b, (((1,), (1,)), ((), ())), preferred_element_type=F32)


def _dot_tn(a, b):
    return lax.dot_general(a, b, (((0,), (0,)), ((), ())), preferred_element_type=F32)


def _quantize_rows(x):
    amax = jnp.max(jnp.abs(x), axis=-1, keepdims=True)
    scale = jnp.where(amax > 0.0, amax * (1.0 / FP8_AMAX), 1.0)
    return (x * (1.0 / scale)).astype(F8), scale


def _norm_matmul_kernel(x_ref, g_ref, w_ref, o_ref, *rest, emit_xq, whole_n):
    if emit_xq:
        xq_ref, sx_ref, xn_ref = rest
    else:
        (xn_ref,) = rest

    def normalise():
        xn = _rms(x_ref[...].astype(F32), g_ref[...])
        xn_ref[...] = xn.astype(BF16)
        if emit_xq:
            xq_ref[...], sx_ref[...] = _quantize_rows(xn)

    if whole_n:
        normalise()
    else:
        pl.when(pl.program_id(1) == 0)(normalise)
    o_ref[...] = _dot(xn_ref[...], w_ref[...]).astype(o_ref.dtype)


def norm_matmul(x, gain, w, *, k, x_col_blk=0, tm, tn, out_dtype, emit_xq=False, name):
    m = x.shape[0]
    n = w.shape[1]
    tm = min(tm, m)
    tn = min(tn, n)
    out_shape = [jax.ShapeDtypeStruct((m, n), out_dtype)]
    out_specs = [pl.BlockSpec((tm, tn), lambda i, j: (i, j))]
    if emit_xq:
        out_shape += [jax.ShapeDtypeStruct((m, k), F8), jax.ShapeDtypeStruct((m, 1), F32)]
        out_specs += [pl.BlockSpec((tm, k), lambda i, j: (i, 0)), pl.BlockSpec((tm, 1), lambda i, j: (i, 0))]
    whole_n = tn == n
    w_mode = pl.Buffered(1) if whole_n else None
    res = pl.pallas_call(
        functools.partial(_norm_matmul_kernel, emit_xq=emit_xq, whole_n=whole_n),
        out_shape=out_shape,
        grid=(m // tm, n // tn),
        in_specs=[pl.BlockSpec((tm, k), lambda i, j: (i, x_col_blk)),
                  pl.BlockSpec((1, k), lambda i, j: (0, 0)),
                  pl.BlockSpec((k, tn), lambda i, j: (0, j), pipeline_mode=w_mode)],
        out_specs=out_specs,
        scratch_shapes=[pltpu.VMEM((tm, k), BF16)],
        compiler_params=_cparams("parallel", "arbitrary"),
        name=name,
    )(x, gain.reshape(1, k).astype(F32), w)
    return res if emit_xq else res[0]


def _quantize_kernel(w_ref, q_ref, s_ref):
    q_ref[...], s_ref[...] = _quantize_rows(w_ref[...])


def quantize_rows(w, *, tr, name):
    r, d = w.shape
    return pl.pallas_call(
        _quantize_kernel,
        out_shape=[jax.ShapeDtypeStruct((r, d), F8), jax.ShapeDtypeStruct((r, 1), F32)],
        grid=(r // tr,),
        in_specs=[pl.BlockSpec((tr, d), lambda i: (i, 0))],
        out_specs=[pl.BlockSpec((tr, d), lambda i: (i, 0)), pl.BlockSpec((tr, 1), lambda i: (i, 0))],
        compiler_params=_cparams("parallel"),
        name=name,
    )(w)


def _rope_pair(r, cosv, sinv):
    return r * cosv + pltpu.roll(r, 2 * 32, axis=1) * sinv


def _qproj_kernel(c_ref, g_ref, w_ref, cos_ref, sin_ref, o_ref, *, scale):
    xn = _rms(c_ref[...].astype(F32), g_ref[...]).astype(BF16)
    q = _dot(xn, w_ref[...])
    cosv = cos_ref[...]
    sinv = sin_ref[...]
    lane = lax.broadcasted_iota(I32, (1, LANES), 1)
    rope_base = MLA_HEADS * MLA_NOPE
    for j in range(MLA_HEADS // 2):
        r = q[:, rope_base + j * LANES: rope_base + (j + 1) * LANES]
        r = _rope_pair(r, cosv, sinv) * scale
        for p in range(2):
            h = 2 * j + p
            own = ((lane // 32) % 2) == p
            o_ref[:, h * 256: h * 256 + 128] = (q[:, h * 128:(h + 1) * 128] * scale).astype(o_ref.dtype)
            o_ref[:, h * 256 + 128: h * 256 + 256] = jnp.where(own, r, 0.0).astype(o_ref.dtype)


def mla_q_proj(h_in, gain, w_uq_perm, cos_t, sin_t, *, tm):
    m = h_in.shape[0]
    tm = min(tm, m)
    return pl.pallas_call(
        functools.partial(_qproj_kernel, scale=MLA_QK ** -0.5 * math.log2(math.e)),
        out_shape=jax.ShapeDtypeStruct((m, MLA_HEADS * 256), BF16),
        grid=(m // tm,),
        in_specs=[pl.BlockSpec((tm, MLA_Q_LORA), lambda i: (i, COL_CQ // MLA_Q_LORA)),
                  pl.BlockSpec((1, MLA_Q_LORA), lambda i: (0, 0)),
                  pl.BlockSpec(w_uq_perm.shape, lambda i: (0, 0)),
                  pl.BlockSpec((tm, LANES), lambda i: (i, 0)),
                  pl.BlockSpec((tm, LANES), lambda i: (i, 0))],
        out_specs=pl.BlockSpec((tm, MLA_HEADS * 256), lambda i: (i, 0)),
        compiler_params=_cparams("parallel"),
        name="mla_q_proj",
    )(h_in, gain.reshape(1, -1).astype(F32), w_uq_perm, cos_t, sin_t)


def _kvproj_kernel(c_ref, g_ref, w_ref, kr_ref, cos_ref, sin_ref, kv_ref, kro_ref):
    xn = _rms(c_ref[...].astype(F32), g_ref[...]).astype(BF16)
    kv_ref[...] = _dot(xn, w_ref[...]).astype(kv_ref.dtype)
    kro_ref[...] = _rope_pair(kr_ref[...].astype(F32), cos_ref[...], sin_ref[...]).astype(kro_ref.dtype)


def mla_kv_proj(h_in, gain, w_ukv, cos_t, sin_t, *, tm):
    m = h_in.shape[0]
    tm = min(tm, m)
    n = w_ukv.shape[1]
    return pl.pallas_call(
        _kvproj_kernel,
        out_shape=[jax.ShapeDtypeStruct((m, n), BF16), jax.ShapeDtypeStruct((m, LANES), BF16)],
        grid=(m // tm,),
        in_specs=[pl.BlockSpec((tm, MLA_KV_LORA), lambda i: (i, COL_CKV // MLA_KV_LORA)),
                  pl.BlockSpec((1, MLA_KV_LORA), lambda i: (0, 0)),
                  pl.BlockSpec(w_ukv.shape, lambda i: (0, 0)),
                  pl.BlockSpec((tm, LANES), lambda i: (i, COL_KR // LANES)),
                  pl.BlockSpec((tm, LANES), lambda i: (i, 0)),
                  pl.BlockSpec((tm, LANES), lambda i: (i, 0))],
        out_specs=[pl.BlockSpec((tm, n), lambda i: (i, 0)),
                   pl.BlockSpec((tm, LANES), lambda i: (i, 0))],
        compiler_params=_cparams("parallel"),
        name="mla_kv_proj",
    )(h_in, gain.reshape(1, -1).astype(F32), w_ukv, h_in, cos_t, sin_t)


ATTN_UNROLL = 8


def _mla_attn_kernel(qi_tab, kj_tab, q_ref, kv_ref, kr_ref, o_ref, s0_scr, s1_scr, m_scr, acc_scr,
                     *, blk, nq, n_items):
    ones = jnp.ones((blk, LANES), BF16)
    s_scr = (s0_scr, s1_scr)

    def scores(t, s_ref):
        q_rows = pl.ds(pl.multiple_of(qi_tab[t] * blk, blk), blk)
        k_rows = pl.ds(pl.multiple_of(kj_tab[t] * blk, blk), blk)
        k = jnp.concatenate([kv_ref[k_rows, :MLA_NOPE], kr_ref[k_rows, :]], axis=1)
        s_ref[...] = _dot_nt(q_ref[q_rows, :], k)

    def finish(t, s_ref, diagonal):
        qi = qi_tab[t]
        k_rows = pl.ds(pl.multiple_of(kj_tab[t] * blk, blk), blk)
        v = jnp.concatenate([kv_ref[k_rows, MLA_NOPE:], ones], axis=1)
        s = s_ref[...]
        if diagonal:
            row = lax.broadcasted_iota(I32, (blk, blk), 0)
            col = lax.broadcasted_iota(I32, (blk, blk), 1)
            s = jnp.where(col <= row, s, NEG)
            m_new = jnp.max(s, axis=1, keepdims=True) + jnp.zeros((blk, LANES), F32)
        else:
            m = m_scr[qi]
            m_new = jnp.maximum(m, jnp.max(s, axis=1, keepdims=True))
        p = jnp.exp2(s - jnp.concatenate([m_new] * (blk // LANES), axis=1)).astype(BF16)
        pv = _dot(p, v)
        if diagonal:
            acc_scr[qi] = pv
        else:
            alpha = jnp.exp2(m - m_new)
            acc_scr[qi] = jnp.concatenate([alpha, alpha], axis=1) * acc_scr[qi] + pv
        m_scr[qi] = m_new

    def run(t0, n, slot, diagonal):
        for u in range(n):
            scores(jnp.minimum(t0 + u + 1, n_items - 1), s_scr[(slot + u + 1) % 2])
            finish(t0 + u, s_scr[(slot + u) % 2], diagonal)

    def phase(t0, n, slot, diagonal):
        def body(i, carry):
            run(t0 + i * ATTN_UNROLL, ATTN_UNROLL, slot, diagonal)
            return carry

        lax.fori_loop(0, n // ATTN_UNROLL, body, 0)
        if n % ATTN_UNROLL:
            run(t0 + n - n % ATTN_UNROLL, n % ATTN_UNROLL, slot, diagonal)
        return (slot + n) % 2

    scores(0, s0_scr)
    slot = phase(0, nq, 0, True)
    phase(nq, n_items - nq, slot, False)

    def normalise(qi, carry):
        acc = acc_scr[qi]
        o_ref[pl.ds(pl.multiple_of(qi * blk, blk), blk), :] = (acc[:, :MLA_V] / acc[:, MLA_V:]).astype(o_ref.dtype)
        return carry

    lax.fori_loop(0, nq, normalise, 0)


def mla_attention(q_full, kv, k_rope, *, batch, seq, blk):
    blk = min(blk, seq)
    nq = seq // blk
    m = batch * seq
    pairs = [(i, i) for i in range(nq)] + [(i, i - d) for d in range(1, nq) for i in range(d, nq)]
    qi_tab = jnp.asarray([p[0] for p in pairs], I32)
    kj_tab = jnp.asarray([p[1] for p in pairs], I32)
    grid_spec = pltpu.PrefetchScalarGridSpec(
        num_scalar_prefetch=2,
        grid=(batch, MLA_HEADS),
        in_specs=[pl.BlockSpec((seq, 256), lambda b, h, *_: (b, h)),
                  pl.BlockSpec((seq, 256), lambda b, h, *_: (b, h)),
                  pl.BlockSpec((seq, LANES), lambda b, h, *_: (b, 0))],
        out_specs=pl.BlockSpec((seq, MLA_V), lambda b, h, *_: (b, h)),
        scratch_shapes=[pltpu.VMEM((blk, blk), F32), pltpu.VMEM((blk, blk), F32),
                        pltpu.VMEM((nq, blk, LANES), F32), pltpu.VMEM((nq, blk, 2 * MLA_V), F32)],
    )
    return pl.pallas_call(
        functools.partial(_mla_attn_kernel, blk=blk, nq=nq, n_items=len(pairs)),
        out_shape=jax.ShapeDtypeStruct((m, MLA_WIDTH), BF16),
        grid_spec=grid_spec,
        compiler_params=_cparams("parallel", "parallel"),
        name="mla_attention",
    )(qi_tab, kj_tab, q_full, kv, k_rope)


def _gla_kernel(q_ref, k_ref, v_ref, og_ref, lr_ref, wg_ref, bg_ref, gn_ref, o_ref, st_ref, *, sb, c):
    @pl.when(pl.program_id(2) == 0)
    def _():
        st_ref[...] = jnp.zeros_like(st_ref)

    z = _dot(lr_ref[...], wg_ref[...]) + bg_ref[...]
    log_a = (jnp.minimum(z, 0.0) - jnp.log(1.0 + jnp.exp(-jnp.abs(z)))) * (1.0 / GLA_TAU)
    rows = lax.broadcasted_iota(I32, (c, c), 0)
    cols = lax.broadcasted_iota(I32, (c, c), 1)
    causal = cols <= rows
    tril = causal.astype(F32)
    gn = gn_ref[...]
    scale = GLA_HEAD_K ** -0.5

    for n in range(sb // c):
        sl = slice(n * c, (n + 1) * c)
        g = log_a[sl, :]
        b = jnp.dot(tril, g, preferred_element_type=F32, precision=lax.Precision.HIGHEST)
        b_last = b[c - 1:c, :]
        q = q_ref[sl, :].astype(F32)
        k = k_ref[sl, :].astype(F32)
        v = v_ref[sl, :]
        qe = (q * scale * jnp.exp(b)).astype(BF16)
        ke = (k * jnp.exp(-b)).astype(BF16)
        kd = (k * jnp.exp(b_last - b)).astype(BF16)
        att = jnp.where(causal, _dot_nt(qe, ke), 0.0).astype(BF16)
        st = st_ref[...]
        o = _dot(att, v) + _dot_nt(qe, st.astype(BF16))
        st_ref[...] = st * jnp.exp(b_last) + _dot_tn(v, kd)
        og = og_ref[sl, :].astype(F32)
        o = _rms(o, gn) * (og * (1.0 / (1.0 + jnp.exp(-og))))
        o_ref[sl, :] = o.astype(o_ref.dtype)


def gla_mixer(h_in, w_gate_pad, b_gate, gla_out_norm, *, batch, seq, sb):
    sb = min(sb, seq)
    nsb = seq // sb
    m = batch * seq
    return pl.pallas_call(
        functools.partial(_gla_kernel, sb=sb, c=GLA_CHUNK),
        out_shape=jax.ShapeDtypeStruct((m, GLA_V_WIDTH), BF16),
        grid=(batch, GLA_HEADS, nsb),
        in_specs=[pl.BlockSpec((sb, GLA_HEAD_K), lambda b, h, s: (b * nsb + s, COL_GQ // GLA_HEAD_K + h)),
                  pl.BlockSpec((sb, GLA_HEAD_K), lambda b, h, s: (b * nsb + s, COL_GK // GLA_HEAD_K + h)),
                  pl.BlockSpec((sb, GLA_HEAD_V), lambda b, h, s: (b * nsb + s, COL_GV // GLA_HEAD_V + h)),
                  pl.BlockSpec((sb, GLA_HEAD_V), lambda b, h, s: (b * nsb + s, COL_OG // GLA_HEAD_V + h)),
                  pl.BlockSpec((sb, LANES), lambda b, h, s: (b * nsb + s, COL_LR // LANES)),
                  pl.BlockSpec((LANES, GLA_HEAD_K), lambda b, h, s: (0, h)),
                  pl.BlockSpec((1, GLA_HEAD_K), lambda b, h, s: (0, h)),
                  pl.BlockSpec((1, GLA_HEAD_V), lambda b, h, s: (0, 0))],
        out_specs=pl.BlockSpec((sb, GLA_HEAD_V), lambda b, h, s: (b * nsb + s, h)),
        scratch_shapes=[pltpu.VMEM((GLA_HEAD_V, GLA_HEAD_K), F32)],
        compiler_params=_cparams("parallel", "parallel", "arbitrary"),
        name="gla_mixer",
    )(h_in, h_in, h_in, h_in, h_in, w_gate_pad, b_gate.reshape(1, -1).astype(F32),
      gla_out_norm.reshape(1, -1).astype(F32))


def _out_proj_kernel(om_ref, g_ref, og_ref, w_ref, x_ref, o_ref, mix_ref):
    @pl.when(pl.program_id(1) == 0)
    def _():
        mix_ref[:, :MLA_WIDTH] = _rms(om_ref[...].astype(F32), g_ref[...]).astype(BF16)
        mix_ref[:, MLA_WIDTH:] = og_ref[...]

    o_ref[...] = x_ref[...] + _dot(mix_ref[...], w_ref[...])


def mixer_out_proj(o_mla, mla_out_norm, o_gla, w_out, x, *, tm, tn):
    m, d = x.shape
    tm = min(tm, m)
    tn = min(tn, d)
    kmix = MLA_WIDTH + GLA_V_WIDTH
    return pl.pallas_call(
        _out_proj_kernel,
        out_shape=jax.ShapeDtypeStruct((m, d), F32),
        grid=(m // tm, d // tn),
        in_specs=[pl.BlockSpec((tm, MLA_WIDTH), lambda i, j: (i, 0)),
                  pl.BlockSpec((1, MLA_WIDTH), lambda i, j: (0, 0)),
                  pl.BlockSpec((tm, GLA_V_WIDTH), lambda i, j: (i, 0)),
                  pl.BlockSpec((kmix, tn), lambda i, j: (0, j)),
                  pl.BlockSpec((tm, tn), lambda i, j: (i, j))],
        out_specs=pl.BlockSpec((tm, tn), lambda i, j: (i, j)),
        scratch_shapes=[pltpu.VMEM((tm, kmix), BF16)],
        compiler_params=_cparams("parallel", "arbitrary"),
        name="mixer_out_proj",
    )(o_mla, mla_out_norm.reshape(1, -1).astype(F32), o_gla, w_out, x)


def _xattn_kernel(q_ref, kv_ref, w_ref, h_ref, o_ref):
    scale = XATTN_DIM ** -0.5
    outs = []
    for h in range(XATTN_HEADS):
        q = q_ref[:, h * XATTN_DIM:(h + 1) * XATTN_DIM]
        k = kv_ref[:, h * XATTN_DIM:(h + 1) * XATTN_DIM]
        v = kv_ref[:, XATTN_WIDTH + h * XATTN_DIM: XATTN_WIDTH + (h + 1) * XATTN_DIM]
        s = _dot_nt(q, k) * scale
        p = jnp.exp(s - jnp.max(s, axis=1, keepdims=True))
        p = p / jnp.sum(p, axis=1, keepdims=True)
        outs.append(_dot(p.astype(BF16), v).astype(BF16))
    o = jnp.concatenate(outs, axis=1)
    o_ref[...] = h_ref[...] + _dot(o, w_ref[...])


def cross_attention(qx, kvm, w_co, h1, *, batch, seq, mem_tokens, tm):
    m, d = h1.shape
    tm = min(tm, seq)
    per_b = seq // tm
    return pl.pallas_call(
        _xattn_kernel,
        out_shape=jax.ShapeDtypeStruct((m, d), F32),
        grid=(m // tm,),
        in_specs=[pl.BlockSpec((tm, XATTN_WIDTH), lambda i: (i, 0)),
                  pl.BlockSpec((mem_tokens, 2 * XATTN_WIDTH), lambda i: (i // per_b, 0)),
                  pl.BlockSpec(w_co.shape, lambda i: (0, 0)),
                  pl.BlockSpec((tm, d), lambda i: (i, 0))],
        out_specs=pl.BlockSpec((tm, d), lambda i: (i, 0)),
        compiler_params=_cparams("parallel"),
        name="cross_attention",
    )(qx, kvm, w_co, h1)


INT_MIN = -2 ** 31


def _order_key(x):
    b = lax.bitcast_convert_type(x, I32)
    return b ^ (lax.shift_right_arithmetic(b, 31) & 0x7FFFFFFF)


def _order_key_inv(k):
    return lax.bitcast_convert_type(k ^ (lax.shift_right_arithmetic(k, 31) & 0x7FFFFFFF), F32)


def _pack_keys(x, bits):
    low = (1 << bits) - 1
    row = lax.broadcasted_iota(I32, x.shape, 0)
    return (_order_key(x) & ~low) | (low - row)


def _top16_rows(s, bits):
    low = (1 << bits) - 1
    kp = _pack_keys(s, bits)
    vals, idxs = [], []
    for _ in range(PEER_TOPK):
        mx = jnp.max(kp, axis=0, keepdims=True)
        vals.append(_order_key_inv(mx & ~low))
        idxs.append(low - (mx & low))
        kp = jnp.where(kp == mx, INT_MIN, kp)
    return vals, idxs


_PAIRS = [(a, b) for a in range(PEER_TOPK) for b in range(PEER_TOPK) if (a + 1) * (b + 1) <= PEER_TOPK]
_PAIR_ROWS = -(-len(_PAIRS) // 8) * 8


def _peer_topk_kernel(q_ref, keys_ref, eid_ref, gate_ref, eid_t, gate_t, *, tm):
    def head(h, carry):
        tops = []
        for p in range(2):
            col = pl.multiple_of((2 * h + p) * PEER_HALF, PEER_HALF)
            qh = q_ref[:, pl.ds(col, PEER_HALF)]
            s = _dot_nt(keys_ref[2 * h + p], qh)
            tops.append(_top16_rows(s, 7))
        (s1, i1), (s2, i2) = tops
        cand_s = [s1[a] + s2[b] for a, b in _PAIRS]
        cand_e = [i1[a] * PEER_KEYS + i2[b] for a, b in _PAIRS]
        pad = _PAIR_ROWS - len(_PAIRS)
        cs = jnp.concatenate(cand_s + [jnp.full((pad, tm), -jnp.inf, F32)], axis=0)
        ce = jnp.concatenate(cand_e + [jnp.zeros((pad, tm), I32)], axis=0)
        low = 63
        kp = _pack_keys(cs, 6)
        sel_s, sel_e = [], []
        for _ in range(PEER_TOPK):
            mx = jnp.max(kp, axis=0, keepdims=True)
            hit = kp == mx
            sel_s.append(_order_key_inv(mx & ~low))
            sel_e.append(jnp.max(jnp.where(hit, ce, -1), axis=0, keepdims=True))
            kp = jnp.where(hit, INT_MIN, kp)
        top_s = jnp.concatenate(sel_s, axis=0)
        top_e = jnp.concatenate(sel_e, axis=0)
        w = jnp.exp(top_s - top_s[0:1, :])
        gate = w / jnp.sum(w, axis=0, keepdims=True)
        row = pl.multiple_of(h * PEER_TOPK, PEER_TOPK)
        eid_t[pl.ds(row, PEER_TOPK), :] = top_e
        gate_t[pl.ds(row, PEER_TOPK), :] = gate
        return carry

    lax.fori_loop(0, PEER_HEADS, head, 0)
    eid_ref[...] = eid_t[...].T
    gate_ref[...] = gate_t[...].T


def peer_topk(qp, keys, *, tm):
    m = qp.shape[0]
    tm = min(tm, m)
    return pl.pallas_call(
        functools.partial(_peer_topk_kernel, tm=tm),
        out_shape=[jax.ShapeDtypeStruct((m, PEER_SLOTS), I32), jax.ShapeDtypeStruct((m, PEER_SLOTS), F32)],
        grid=(m // tm,),
        in_specs=[pl.BlockSpec((tm, qp.shape[1]), lambda i: (i, 0)),
                  pl.BlockSpec(keys.shape, lambda i: (0, 0, 0))],
        out_specs=[pl.BlockSpec((tm, PEER_SLOTS), lambda i: (i, 0)),
                   pl.BlockSpec((tm, PEER_SLOTS), lambda i: (i, 0))],
        scratch_shapes=[pltpu.VMEM((PEER_SLOTS, tm), I32), pltpu.VMEM((PEER_SLOTS, tm), F32)],
        compiler_params=_cparams("parallel"),
        name="peer_topk",
    )(qp, keys)


GATE_UNROLL = 32


def _peer_gates_kernel(eid_ref, gate_ref, o_ref, g3_ref, *, tm, pitch):
    sub = lax.broadcasted_iota(I32, (PEER_KEYS, PEER_SLOTS), 0)

    def token(t, carry):
        e = eid_ref[pl.ds(t, 1), :]
        g = gate_ref[pl.ds(t, 1), :]
        hi = lax.shift_right_logical(e, 7)
        lo = jnp.bitwise_and(e, PEER_KEYS - 1)
        a1 = jnp.where(sub == hi, g, 0.0).astype(BF16)
        a2 = jnp.where(sub == lo, 1.0, 0.0).astype(BF16)
        g3_ref[pl.ds(t, PEER_KEYS, stride=pitch), :] = _dot_nt(a1, a2)
        return carry

    lax.fori_loop(0, tm, token, 0, unroll=GATE_UNROLL)
    for e1 in range(PEER_KEYS):
        o_ref[:, e1 * PEER_KEYS:(e1 + 1) * PEER_KEYS] = g3_ref[e1 * pitch:e1 * pitch + tm, :].astype(o_ref.dtype)


def peer_gates(eid, gate, *, tm):
    m = eid.shape[0]
    tm = min(tm, m)
    pitch = tm + 8
    return pl.pallas_call(
        functools.partial(_peer_gates_kernel, tm=tm, pitch=pitch),
        out_shape=jax.ShapeDtypeStruct((m, PEER_EXPERTS), BF16),
        grid=(m // tm,),
        in_specs=[pl.BlockSpec((tm, PEER_SLOTS), lambda i: (i, 0)),
                  pl.BlockSpec((tm, PEER_SLOTS), lambda i: (i, 0))],
        out_specs=pl.BlockSpec((tm, PEER_EXPERTS), lambda i: (i, 0)),
        scratch_shapes=[pltpu.VMEM((PEER_KEYS * pitch, PEER_KEYS), F32)],
        compiler_params=_cparams("parallel"),
        name="peer_gates",
    )(eid, gate)


def _peer_dense_kernel(x_ref, sx_ref, u_ref, su_ref, g_ref, v_ref, sv_ref, h_ref, gn_ref, o_ref, aq_scr, sa_scr,
                       *, final_norm, n_blocks):
    j = pl.program_id(1)
    nj = n_blocks

    def step(p, do_values, do_gates):
        if do_gates:
            s = _dot_nt(x_ref[...], u_ref[...])
        if do_values:
            pv = _dot(aq_scr[1 - p], v_ref[...])
        if do_gates:
            s = s * sx_ref[...] * su_ref[...]
            a = 0.5 * s * (1.0 + lax.erf(s * (2.0 ** -0.5)))
            aq_scr[p], sa_scr[p] = _quantize_rows(a * g_ref[...].astype(F32) * sv_ref[...])
        if do_values:
            o_ref[...] += sa_scr[1 - p] * pv

    @pl.when(j == 0)
    def _():
        o_ref[...] = jnp.zeros_like(o_ref)
        step(0, False, True)

    for p in range(2):
        @pl.when(jnp.logical_and(jnp.logical_and(j > 0, j < nj), j % 2 == p))
        def _(p=p):
            step(p, True, True)

    @pl.when(j == nj)
    def _():
        step(nj % 2, True, False)
        y = h_ref[...] + o_ref[...]
        o_ref[...] = _rms(y, gn_ref[...]) if final_norm else y


def peer_dense(xq, sx, uq, su, g, vq, sv, h, gain, *, final_norm, tm, te):
    m, d = xq.shape
    e = uq.shape[0]
    tm = min(tm, m)
    nj = e // te
    once = pl.Buffered(1)

    def cur(j):
        return jnp.minimum(j, nj - 1)

    def prev(j):
        return jnp.maximum(j - 1, 0)

    return pl.pallas_call(
        functools.partial(_peer_dense_kernel, final_norm=final_norm, n_blocks=nj),
        out_shape=jax.ShapeDtypeStruct((m, d), F32),
        grid=(m // tm, nj + 1),
        in_specs=[pl.BlockSpec((tm, d), lambda i, j: (i, 0)),
                  pl.BlockSpec((tm, 1), lambda i, j: (i, 0)),
                  pl.BlockSpec((te, d), lambda i, j: (cur(j), 0)),
                  pl.BlockSpec((1, te), lambda i, j: (0, cur(j))),
                  pl.BlockSpec((tm, te), lambda i, j: (i, cur(j))),
                  pl.BlockSpec((te, d), lambda i, j: (prev(j), 0)),
                  pl.BlockSpec((1, te), lambda i, j: (0, cur(j))),
                  pl.BlockSpec((tm, d), lambda i, j: (i, 0), pipeline_mode=once),
                  pl.BlockSpec((1, d), lambda i, j: (0, 0))],
        out_specs=pl.BlockSpec((tm, d), lambda i, j: (i, 0)),
        scratch_shapes=[pltpu.VMEM((2, tm, te), F8), pltpu.VMEM((2, tm, 1), F32)],
        compiler_params=_cparams("parallel", "arbitrary"),
        name="peer_dense",
    )(xq, sx, uq, su.reshape(1, e), g, vq, sv.reshape(1, e), h, gain.reshape(1, d).astype(F32))


def _w_in_columns():
    src = {}
    off = 0
    for name, width in (("cq", MLA_Q_LORA), ("ckv", MLA_KV_LORA), ("kr", MLA_ROPE), ("gq", GLA_K_WIDTH),
                        ("gk", GLA_K_WIDTH), ("gv", GLA_V_WIDTH), ("lr", GLA_RANK), ("og", GLA_V_WIDTH)):
        src[name] = np.arange(off, off + width)
        off += width
    cols = np.full((IN_PAD,), -1, np.int64)
    for name, start in (("cq", COL_CQ), ("ckv", COL_CKV), ("gq", COL_GQ), ("gk", COL_GK),
                        ("gv", COL_GV), ("og", COL_OG), ("lr", COL_LR)):
        cols[start:start + len(src[name])] = src[name]
    half = MLA_ROPE // 2
    x1, x2 = src["kr"][:half], src["kr"][half:]
    cols[COL_KR:COL_KR + LANES] = np.concatenate([x1, x1, x2, x2])
    return cols


def _w_uq_columns():
    half = MLA_ROPE // 2
    nope = [h * MLA_QK + d for h in range(MLA_HEADS) for d in range(MLA_NOPE)]
    rope = []
    for j in range(MLA_HEADS // 2):
        a, b = 2 * j, 2 * j + 1
        for part in (0, 1):
            for h in (a, b):
                rope += [h * MLA_QK + MLA_NOPE + part * half + r for r in range(half)]
    return np.array(nope + rope, np.int64)


def _column_runs(cols):
    runs = []
    start = 0
    for i in range(1, len(cols) + 1):
        if i == len(cols) or cols[i] != (cols[i - 1] + 1 if cols[i - 1] >= 0 else -1):
            runs.append((int(cols[start]), i - start))
            start = i
    return runs


def _take_cols(w, cols, dtype):
    pieces = [jnp.zeros((w.shape[0], n), dtype) if src < 0 else w[:, src:src + n].astype(dtype)
              for src, n in _column_runs(cols)]
    return jnp.concatenate(pieces, axis=1)


def _layout_cols_kernel(w_ref, o_ref, *, runs):
    dst = 0
    pending = []
    for src, n in runs:
        pending.append(jnp.zeros((w_ref.shape[0], n), o_ref.dtype) if src < 0
                       else w_ref[:, src:src + n].astype(o_ref.dtype))
        width = sum(p.shape[1] for p in pending)
        if (dst + width) % LANES == 0:
            o_ref[:, dst:dst + width] = pending[0] if len(pending) == 1 else jnp.concatenate(pending, axis=1)
            dst += width
            pending = []
    assert not pending


def layout_cols(w, cols, dtype, *, tr, name):
    r, c = w.shape
    return pl.pallas_call(
        functools.partial(_layout_cols_kernel, runs=_column_runs(cols)),
        out_shape=jax.ShapeDtypeStruct((r, len(cols)), dtype),
        grid=(r // tr,),
        in_specs=[pl.BlockSpec((tr, c), lambda i: (i, 0))],
        out_specs=pl.BlockSpec((tr, len(cols)), lambda i: (i, 0)),
        compiler_params=_cparams("parallel"),
        name=name,
    )(w)


def _rope_tables(positions):
    half = MLA_ROPE // 2
    inv_freq = ROPE_THETA ** (-jnp.arange(0, MLA_ROPE, 2, dtype=F32) / MLA_ROPE)
    ang = positions.astype(F32).reshape(-1, 1) * inv_freq
    cos = jnp.cos(ang)
    sin = jnp.sin(ang)
    return jnp.tile(cos, (1, 4)), jnp.concatenate([-sin, -sin, sin, sin], axis=1)


def kernel(x, mem, positions, norm_mem, norm_mix, w_in, mla_q_norm, w_uq, mla_kv_norm, w_ukv, mla_out_norm,
           w_gate_up, b_gate, gla_out_norm, w_out, norm_cross, w_cq, w_ck, w_cv, w_co, norm_ffn, w_peer_q,
           peer_sub_keys, peer_u, peer_v, norm_final):
    batch, seq, d = x.shape
    mem_tokens = mem.shape[1]
    m = batch * seq
    h = x.reshape(m, d)
    cos_t, sin_t = _rope_tables(positions)

    mn_kv = norm_matmul(mem.reshape(batch * mem_tokens, d), norm_mem,
                        jnp.concatenate([w_ck[0], w_cv[0]], axis=1).astype(BF16),
                        k=d, tm=512, tn=1024, out_dtype=BF16, name="mem_kv_proj")

    for l in range(norm_mix.shape[0]):
        w_in_p = layout_cols(w_in[l], _w_in_columns(), BF16, tr=256, name="w_in_layout")
        h_in = norm_matmul(h, norm_mix[l], w_in_p, k=d, tm=512, tn=1024, out_dtype=BF16, name="in_proj")

        w_uq_p = _take_cols(w_uq[l], _w_uq_columns(), BF16)
        q_full = mla_q_proj(h_in, mla_q_norm[l], w_uq_p, cos_t, sin_t, tm=512)
        kv, k_rope = mla_kv_proj(h_in, mla_kv_norm[l], w_ukv[l].astype(BF16), cos_t, sin_t, tm=512)
        o_mla = mla_attention(q_full, kv, k_rope, batch=batch, seq=seq, blk=512)

        w_gate_pad = jnp.zeros((LANES, GLA_K_WIDTH), F32).at[:GLA_RANK].set(w_gate_up[l]).astype(BF16)
        o_gla = gla_mixer(h_in, w_gate_pad, b_gate[l], gla_out_norm[l], batch=batch, seq=seq, sb=1024)

        h = mixer_out_proj(o_mla, mla_out_norm[l], o_gla, w_out[l].astype(BF16), h, tm=512, tn=1024)

        qx = norm_matmul(h, norm_cross[l], w_cq[l].astype(BF16), k=d, tm=512, tn=1024, out_dtype=BF16,
                         name="xattn_q_proj")
        h = cross_attention(qx, mn_kv, w_co[l].astype(BF16), h, batch=batch, seq=seq,
                            mem_tokens=mem_tokens, tm=256)

        qp, xq, sx = norm_matmul(h, norm_ffn[l], w_peer_q[l].astype(BF16), k=d, tm=256, tn=2048, out_dtype=BF16,
                                 emit_xq=True, name="peer_q_proj")
        keys = peer_sub_keys[l].reshape(PEER_HEADS * 2, PEER_KEYS, PEER_HALF).astype(BF16)
        eid, gate = peer_topk(qp, keys, tm=256)
        g = peer_gates(eid, gate, tm=128)
        uq, su = quantize_rows(peer_u[l], tr=512, name="peer_u_quant")
        vq, sv = quantize_rows(peer_v[l], tr=512, name="peer_v_quant")
        h = peer_dense(xq, sx, uq, su, g, vq, sv, h, norm_final,
                       final_norm=(l + 1 == norm_mix.shape[0]), tm=512, te=512)
    return h.reshape(batch, seq, d)
```

```python
import functools
import math

import numpy as np
import jax
import jax.numpy as jnp
from jax import lax
from jax.experimental import pallas as pl
from jax.experimental.pallas import tpu as pltpu

F32 = jnp.float32
BF16 = jnp.bfloat16
F8 = jnp.float8_e4m3fn
I32 = jnp.int32
FP8_AMAX = 256.0

EPS = 1e-6
ROPE_THETA = 10000.0

MLA_HEADS = 16
MLA_Q_LORA = 1024
MLA_KV_LORA = 512
MLA_NOPE = 128
MLA_ROPE = 64
MLA_QK = MLA_NOPE + MLA_ROPE
MLA_V = 128
MLA_WIDTH = MLA_HEADS * MLA_V

GLA_HEADS = 4
GLA_HEAD_K = 256
GLA_HEAD_V = 512
GLA_K_WIDTH = GLA_HEADS * GLA_HEAD_K
GLA_V_WIDTH = GLA_HEADS * GLA_HEAD_V
GLA_RANK = 16
GLA_TAU = 16.0
GLA_CHUNK = 64

XATTN_HEADS = 4
XATTN_DIM = 256
XATTN_WIDTH = XATTN_HEADS * XATTN_DIM

PEER_HEADS = 8
PEER_KEYS = 128
PEER_HALF = 128
PEER_TOPK = 16
PEER_SLOTS = PEER_HEADS * PEER_TOPK
PEER_EXPERTS = PEER_KEYS * PEER_KEYS

LANES = 128
VMEM_LIMIT = 56 * 1024 * 1024

COL_CQ = 0
COL_CKV = COL_CQ + MLA_Q_LORA
COL_GQ = COL_CKV + MLA_KV_LORA
COL_GK = COL_GQ + GLA_K_WIDTH
COL_GV = COL_GK + GLA_K_WIDTH
COL_OG = COL_GV + GLA_V_WIDTH
COL_KR = COL_OG + GLA_V_WIDTH
COL_LR = COL_KR + LANES
IN_PAD = 8192

NEG = -1e30


def _cparams(*sem):
    return pltpu.CompilerParams(dimension_semantics=sem, vmem_limit_bytes=VMEM_LIMIT)


def _rms(x, g):
    ms = jnp.mean(x * x, axis=-1, keepdims=True)
    return x * lax.rsqrt(ms + EPS) * g


def _dot(a, b):
    return jnp.dot(a, b, preferred_element_type=F32)


def _dot_nt(a, b):
    return lax.dot_general(a, b, (((1,), (1,)), ((), ())), preferred_element_type=F32)


def _dot_tn(a, b):
    return lax.dot_general(a, b, (((0,), (0,)), ((), ())), preferred_element_type=F32)


def _quantize_rows(x):
    amax = jnp.max(jnp.abs(x), axis=-1, keepdims=True)
    scale = jnp.where(amax > 0.0, amax * (1.0 / FP8_AMAX), 1.0)
    return (x * (1.0 / scale)).astype(F8), scale


def _norm_matmul_kernel(x_ref, g_ref, w_ref, o_ref, *rest, emit_xq, whole_n, w_transposed):
    if emit_xq:
        xq_ref, sx_ref, xn_ref = rest
    else:
        (xn_ref,) = rest

    def normalise():
        xn = _rms(x_ref[...].astype(F32), g_ref[...])
        xn_ref[...] = xn.astype(BF16)
        if emit_xq:
            xq_ref[...], sx_ref[...] = _quantize_rows(xn)

    if whole_n:
        normalise()
    else:
        pl.when(pl.program_id(1) == 0)(normalise)
    mm = _dot_nt if w_transposed else _dot
    o_ref[...] = mm(xn_ref[...], w_ref[...]).astype(o_ref.dtype)


def norm_matmul(x, gain, w, *, k, x_col_blk=0, tm, tn, out_dtype, emit_xq=False, w_transposed=False, name):
    m = x.shape[0]
    n = w.shape[0] if w_transposed else w.shape[1]
    tm = min(tm, m)
    tn = min(tn, n)
    out_shape = [jax.ShapeDtypeStruct((m, n), out_dtype)]
    out_specs = [pl.BlockSpec((tm, tn), lambda i, j: (i, j))]
    if emit_xq:
        out_shape += [jax.ShapeDtypeStruct((m, k), F8), jax.ShapeDtypeStruct((m, 1), F32)]
        out_specs += [pl.BlockSpec((tm, k), lambda i, j: (i, 0)), pl.BlockSpec((tm, 1), lambda i, j: (i, 0))]
    whole_n = tn == n
    w_mode = pl.Buffered(1) if whole_n else None
    w_spec = (pl.BlockSpec((tn, k), lambda i, j: (j, 0), pipeline_mode=w_mode) if w_transposed
              else pl.BlockSpec((k, tn), lambda i, j: (0, j), pipeline_mode=w_mode))
    res = pl.pallas_call(
        functools.partial(_norm_matmul_kernel, emit_xq=emit_xq, whole_n=whole_n, w_transposed=w_transposed),
        out_shape=out_shape,
        grid=(m // tm, n // tn),
        in_specs=[pl.BlockSpec((tm, k), lambda i, j: (i, x_col_blk)),
                  pl.BlockSpec((1, k), lambda i, j: (0, 0)),
                  w_spec],
        out_specs=out_specs,
        scratch_shapes=[pltpu.VMEM((tm, k), BF16)],
        compiler_params=_cparams("parallel", "arbitrary"),
        name=name,
    )(x, gain.reshape(1, k).astype(F32), w)
    return res if emit_xq else res[0]


def _quantize_kernel(w_ref, q_ref, s_ref):
    q_ref[...], s_ref[...] = _quantize_rows(w_ref[...])


def quantize_rows(w, *, tr, name):
    r, d = w.shape
    return pl.pallas_call(
        _quantize_kernel,
        out_shape=[jax.ShapeDtypeStruct((r, d), F8), jax.ShapeDtypeStruct((r, 1), F32)],
        grid=(r // tr,),
        in_specs=[pl.BlockSpec((tr, d), lambda i: (i, 0))],
        out_specs=[pl.BlockSpec((tr, d), lambda i: (i, 0)), pl.BlockSpec((tr, 1), lambda i: (i, 0))],
        compiler_params=_cparams("parallel"),
        name=name,
    )(w)


def _rope_pair(r, cosv, sinv):
    return r * cosv + pltpu.roll(r, 2 * 32, axis=1) * sinv


def _qproj_kernel(c_ref, g_ref, w_ref, cos_ref, sin_ref, o_ref, *, scale):
    xn = _rms(c_ref[...].astype(F32), g_ref[...]).astype(BF16)
    q = _dot(xn, w_ref[...])
    cosv = cos_ref[...]
    sinv = sin_ref[...]
    lane = lax.broadcasted_iota(I32, (1, LANES), 1)
    rope_base = MLA_HEADS * MLA_NOPE
    for j in range(MLA_HEADS // 2):
        r = q[:, rope_base + j * LANES: rope_base + (j + 1) * LANES]
        r = _rope_pair(r, cosv, sinv) * scale
        for p in range(2):
            h = 2 * j + p
            own = ((lane // 32) % 2) == p
            o_ref[:, h * 256: h * 256 + 128] = (q[:, h * 128:(h + 1) * 128] * scale).astype(o_ref.dtype)
            o_ref[:, h * 256 + 128: h * 256 + 256] = jnp.where(own, r, 0.0).astype(o_ref.dtype)


def mla_q_proj(h_in, gain, w_uq_perm, cos_t, sin_t, *, tm):
    m = h_in.shape[0]
    tm = min(tm, m)
    return pl.pallas_call(
        functools.partial(_qproj_kernel, scale=MLA_QK ** -0.5 * math.log2(math.e)),
        out_shape=jax.ShapeDtypeStruct((m, MLA_HEADS * 256), BF16),
        grid=(m // tm,),
        in_specs=[pl.BlockSpec((tm, MLA_Q_LORA), lambda i: (i, COL_CQ // MLA_Q_LORA)),
                  pl.BlockSpec((1, MLA_Q_LORA), lambda i: (0, 0)),
                  pl.BlockSpec(w_uq_perm.shape, lambda i: (0, 0)),
                  pl.BlockSpec((tm, LANES), lambda i: (i, 0)),
                  pl.BlockSpec((tm, LANES), lambda i: (i, 0))],
        out_specs=pl.BlockSpec((tm, MLA_HEADS * 256), lambda i: (i, 0)),
        compiler_params=_cparams("parallel"),
        name="mla_q_proj",
    )(h_in, gain.reshape(1, -1).astype(F32), w_uq_perm, cos_t, sin_t)


def _kvproj_kernel(c_ref, g_ref, w_ref, kr_ref, cos_ref, sin_ref, kv_ref, kro_ref):
    xn = _rms(c_ref[...].astype(F32), g_ref[...]).astype(BF16)
    kv_ref[...] = _dot(xn, w_ref[...]).astype(kv_ref.dtype)
    kro_ref[...] = _rope_pair(kr_ref[...].astype(F32), cos_ref[...], sin_ref[...]).astype(kro_ref.dtype)


def mla_kv_proj(h_in, gain, w_ukv, cos_t, sin_t, *, tm):
    m = h_in.shape[0]
    tm = min(tm, m)
    n = w_ukv.shape[1]
    return pl.pallas_call(
        _kvproj_kernel,
        out_shape=[jax.ShapeDtypeStruct((m, n), BF16), jax.ShapeDtypeStruct((m, LANES), BF16)],
        grid=(m // tm,),
        in_specs=[pl.BlockSpec((tm, MLA_KV_LORA), lambda i: (i, COL_CKV // MLA_KV_LORA)),
                  pl.BlockSpec((1, MLA_KV_LORA), lambda i: (0, 0)),
                  pl.BlockSpec(w_ukv.shape, lambda i: (0, 0)),
                  pl.BlockSpec((tm, LANES), lambda i: (i, COL_KR // LANES)),
                  pl.BlockSpec((tm, LANES), lambda i: (i, 0)),
                  pl.BlockSpec((tm, LANES), lambda i: (i, 0))],
        out_specs=[pl.BlockSpec((tm, n), lambda i: (i, 0)),
                   pl.BlockSpec((tm, LANES), lambda i: (i, 0))],
        compiler_params=_cparams("parallel"),
        name="mla_kv_proj",
    )(h_in, gain.reshape(1, -1).astype(F32), w_ukv, h_in, cos_t, sin_t)


ATTN_UNROLL = 8


def _mla_attn_kernel(qi_tab, kj_tab, q_ref, kv_ref, kr_ref, o_ref, s0_scr, s1_scr, m_scr, acc_scr,
                     *, blk, nq, n_items):
    ones = jnp.ones((blk, LANES), BF16)
    s_scr = (s0_scr, s1_scr)

    def scores(t, s_ref):
        q_rows = pl.ds(pl.multiple_of(qi_tab[t] * blk, blk), blk)
        k_rows = pl.ds(pl.multiple_of(kj_tab[t] * blk, blk), blk)
        k = jnp.concatenate([kv_ref[k_rows, :MLA_NOPE], kr_ref[k_rows, :]], axis=1)
        s_ref[...] = _dot_nt(q_ref[q_rows, :], k)

    def finish(t, s_ref, diagonal):
        qi = qi_tab[t]
        k_rows = pl.ds(pl.multiple_of(kj_tab[t] * blk, blk), blk)
        v = jnp.concatenate([kv_ref[k_rows, MLA_NOPE:], ones], axis=1)
        s = s_ref[...]
        if diagonal:
            row = lax.broadcasted_iota(I32, (blk, blk), 0)
            col = lax.broadcasted_iota(I32, (blk, blk), 1)
            s = jnp.where(col <= row, s, NEG)
            m_new = jnp.max(s, axis=1, keepdims=True) + jnp.zeros((blk, LANES), F32)
        else:
            m = m_scr[qi]
            m_new = jnp.maximum(m, jnp.max(s, axis=1, keepdims=True))
        p = jnp.exp2(s - jnp.concatenate([m_new] * (blk // LANES), axis=1)).astype(BF16)
        pv = _dot(p, v)
        if diagonal:
            acc_scr[qi] = pv
        else:
            alpha = jnp.exp2(m - m_new)
            acc_scr[qi] = jnp.concatenate([alpha, alpha], axis=1) * acc_scr[qi] + pv
        m_scr[qi] = m_new

    def run(t0, n, slot, diagonal):
        for u in range(n):
            scores(jnp.minimum(t0 + u + 1, n_items - 1), s_scr[(slot + u + 1) % 2])
            finish(t0 + u, s_scr[(slot + u) % 2], diagonal)

    def phase(t0, n, slot, diagonal):
        def body(i, carry):
            run(t0 + i * ATTN_UNROLL, ATTN_UNROLL, slot, diagonal)
            return carry

        lax.fori_loop(0, n // ATTN_UNROLL, body, 0)
        if n % ATTN_UNROLL:
            run(t0 + n - n % ATTN_UNROLL, n % ATTN_UNROLL, slot, diagonal)
        return (slot + n) % 2

    scores(0, s0_scr)
    slot = phase(0, nq, 0, True)
    phase(nq, n_items - nq, slot, False)

    def normalise(qi, carry):
        acc = acc_scr[qi]
        o_ref[pl.ds(pl.multiple_of(qi * blk, blk), blk), :] = (acc[:, :MLA_V] / acc[:, MLA_V:]).astype(o_ref.dtype)
        return carry

    lax.fori_loop(0, nq, normalise, 0)


def mla_attention(q_full, kv, k_rope, *, batch, seq, blk):
    blk = min(blk, seq)
    nq = seq // blk
    m = batch * seq
    pairs = [(i, i) for i in range(nq)] + [(i, i - d) for d in range(1, nq) for i in range(d, nq)]
    qi_tab = jnp.asarray([p[0] for p in pairs], I32)
    kj_tab = jnp.asarray([p[1] for p in pairs], I32)
    grid_spec = pltpu.PrefetchScalarGridSpec(
        num_scalar_prefetch=2,
        grid=(batch, MLA_HEADS),
        in_specs=[pl.BlockSpec((seq, 256), lambda b, h, *_: (b, h)),
                  pl.BlockSpec((seq, 256), lambda b, h, *_: (b, h)),
                  pl.BlockSpec((seq, LANES), lambda b, h, *_: (b, 0))],
        out_specs=pl.BlockSpec((seq, MLA_V), lambda b, h, *_: (b, h)),
        scratch_shapes=[pltpu.VMEM((blk, blk), F32), pltpu.VMEM((blk, blk), F32),
                        pltpu.VMEM((nq, blk, LANES), F32), pltpu.VMEM((nq, blk, 2 * MLA_V), F32)],
    )
    return pl.pallas_call(
        functools.partial(_mla_attn_kernel, blk=blk, nq=nq, n_items=len(pairs)),
        out_shape=jax.ShapeDtypeStruct((m, MLA_WIDTH), BF16),
        grid_spec=grid_spec,
        compiler_params=_cparams("parallel", "parallel"),
        name="mla_attention",
    )(qi_tab, kj_tab, q_full, kv, k_rope)


def _gla_kernel(q_ref, k_ref, v_ref, og_ref, lr_ref, wg_ref, bg_ref, gn_ref, o_ref, st_ref, *, sb, c):
    @pl.when(pl.program_id(2) == 0)
    def _():
        st_ref[...] = jnp.zeros_like(st_ref)

    z = _dot(lr_ref[...], wg_ref[...]) + bg_ref[...]
    log_a = (jnp.minimum(z, 0.0) - jnp.log(1.0 + jnp.exp(-jnp.abs(z)))) * (1.0 / GLA_TAU)
    rows = lax.broadcasted_iota(I32, (c, c), 0)
    cols = lax.broadcasted_iota(I32, (c, c), 1)
    causal = cols <= rows
    tril = causal.astype(F32)
    gn = gn_ref[...]
    scale = GLA_HEAD_K ** -0.5

    for n in range(sb // c):
        sl = slice(n * c, (n + 1) * c)
        g = log_a[sl, :]
        b = jnp.dot(tril, g, preferred_element_type=F32, precision=lax.Precision.HIGHEST)
        b_last = b[c - 1:c, :]
        q = q_ref[sl, :].astype(F32)
        k = k_ref[sl, :].astype(F32)
        v = v_ref[sl, :]
        qe = (q * scale * jnp.exp(b)).astype(BF16)
        ke = (k * jnp.exp(-b)).astype(BF16)
        kd = (k * jnp.exp(b_last - b)).astype(BF16)
        att = jnp.where(causal, _dot_nt(qe, ke), 0.0).astype(BF16)
        st = st_ref[...]
        o = _dot(att, v) + _dot_nt(qe, st.astype(BF16))
        st_ref[...] = st * jnp.exp(b_last) + _dot_tn(v, kd)
        og = og_ref[sl, :].astype(F32)
        o = _rms(o, gn) * (og * (1.0 / (1.0 + jnp.exp(-og))))
        o_ref[sl, :] = o.astype(o_ref.dtype)


def gla_mixer(h_in, w_gate_pad, b_gate, gla_out_norm, *, batch, seq, sb):
    sb = min(sb, seq)
    nsb = seq // sb
    m = batch * seq
    return pl.pallas_call(
        functools.partial(_gla_kernel, sb=sb, c=GLA_CHUNK),
        out_shape=jax.ShapeDtypeStruct((m, GLA_V_WIDTH), BF16),
        grid=(batch, GLA_HEADS, nsb),
        in_specs=[pl.BlockSpec((sb, GLA_HEAD_K), lambda b, h, s: (b * nsb + s, COL_GQ // GLA_HEAD_K + h)),
                  pl.BlockSpec((sb, GLA_HEAD_K), lambda b, h, s: (b * nsb + s, COL_GK // GLA_HEAD_K + h)),
                  pl.BlockSpec((sb, GLA_HEAD_V), lambda b, h, s: (b * nsb + s, COL_GV // GLA_HEAD_V + h)),
                  pl.BlockSpec((sb, GLA_HEAD_V), lambda b, h, s: (b * nsb + s, COL_OG // GLA_HEAD_V + h)),
                  pl.BlockSpec((sb, LANES), lambda b, h, s: (b * nsb + s, COL_LR // LANES)),
                  pl.BlockSpec((LANES, GLA_HEAD_K), lambda b, h, s: (0, h)),
                  pl.BlockSpec((1, GLA_HEAD_K), lambda b, h, s: (0, h)),
                  pl.BlockSpec((1, GLA_HEAD_V), lambda b, h, s: (0, 0))],
        out_specs=pl.BlockSpec((sb, GLA_HEAD_V), lambda b, h, s: (b * nsb + s, h)),
        scratch_shapes=[pltpu.VMEM((GLA_HEAD_V, GLA_HEAD_K), F32)],
        compiler_params=_cparams("parallel", "parallel", "arbitrary"),
        name="gla_mixer",
    )(h_in, h_in, h_in, h_in, h_in, w_gate_pad, b_gate.reshape(1, -1).astype(F32),
      gla_out_norm.reshape(1, -1).astype(F32))


def _out_proj_kernel(om_ref, g_ref, og_ref, w_ref, x_ref, o_ref, mix_ref):
    @pl.when(pl.program_id(1) == 0)
    def _():
        mix_ref[:, :MLA_WIDTH] = _rms(om_ref[...].astype(F32), g_ref[...]).astype(BF16)
        mix_ref[:, MLA_WIDTH:] = og_ref[...]

    o_ref[...] = x_ref[...] + _dot(mix_ref[...], w_ref[...])


def mixer_out_proj(o_mla, mla_out_norm, o_gla, w_out, x, *, tm, tn):
    m, d = x.shape
    tm = min(tm, m)
    tn = min(tn, d)
    kmix = MLA_WIDTH + GLA_V_WIDTH
    return pl.pallas_call(
        _out_proj_kernel,
        out_shape=jax.ShapeDtypeStruct((m, d), F32),
        grid=(m // tm, d // tn),
        in_specs=[pl.BlockSpec((tm, MLA_WIDTH), lambda i, j: (i, 0)),
                  pl.BlockSpec((1, MLA_WIDTH), lambda i, j: (0, 0)),
                  pl.BlockSpec((tm, GLA_V_WIDTH), lambda i, j: (i, 0)),
                  pl.BlockSpec((kmix, tn), lambda i, j: (0, j)),
                  pl.BlockSpec((tm, tn), lambda i, j: (i, j))],
        out_specs=pl.BlockSpec((tm, tn), lambda i, j: (i, j)),
        scratch_shapes=[pltpu.VMEM((tm, kmix), BF16)],
        compiler_params=_cparams("parallel", "arbitrary"),
        name="mixer_out_proj",
    )(o_mla, mla_out_norm.reshape(1, -1).astype(F32), o_gla, w_out, x)


def _xattn_kernel(q_ref, kv_ref, w_ref, h_ref, o_ref):
    scale = XATTN_DIM ** -0.5
    outs = []
    for h in range(XATTN_HEADS):
        q = q_ref[:, h * XATTN_DIM:(h + 1) * XATTN_DIM]
        k = kv_ref[:, h * XATTN_DIM:(h + 1) * XATTN_DIM]
        v = kv_ref[:, XATTN_WIDTH + h * XATTN_DIM: XATTN_WIDTH + (h + 1) * XATTN_DIM]
        s = _dot_nt(q, k) * scale
        p = jnp.exp(s - jnp.max(s, axis=1, keepdims=True))
        p = p / jnp.sum(p, axis=1, keepdims=True)
        outs.append(_dot(p.astype(BF16), v).astype(BF16))
    o = jnp.concatenate(outs, axis=1)
    o_ref[...] = h_ref[...] + _dot(o, w_ref[...])


def cross_attention(qx, kvm, w_co, h1, *, batch, seq, mem_tokens, tm):
    m, d = h1.shape
    tm = min(tm, seq)
    per_b = seq // tm
    return pl.pallas_call(
        _xattn_kernel,
        out_shape=jax.ShapeDtypeStruct((m, d), F32),
        grid=(m // tm,),
        in_specs=[pl.BlockSpec((tm, XATTN_WIDTH), lambda i: (i, 0)),
                  pl.BlockSpec((mem_tokens, 2 * XATTN_WIDTH), lambda i: (i // per_b, 0)),
                  pl.BlockSpec(w_co.shape, lambda i: (0, 0)),
                  pl.BlockSpec((tm, d), lambda i: (i, 0))],
        out_specs=pl.BlockSpec((tm, d), lambda i: (i, 0)),
        compiler_params=_cparams("parallel"),
        name="cross_attention",
    )(qx, kvm, w_co, h1)


def _order_key(x):
    b = lax.bitcast_convert_type(x, I32)
    return b ^ (lax.shift_right_arithmetic(b, 31) & 0x7FFFFFFF)


def _order_key_inv(k):
    return lax.bitcast_convert_type(k ^ (lax.shift_right_arithmetic(k, 31) & 0x7FFFFFFF), F32)


KEY_BIAS = 0x20000000


def _pack_keys(x, bits):
    low = (1 << bits) - 1
    row = lax.broadcasted_iota(I32, x.shape, 0)
    k = (_order_key(x) & ~((low << 2) | 3)) | ((low - row) << 2)
    return lax.bitcast_convert_type(lax.shift_right_arithmetic(k, 2) + KEY_BIAS, F32)


def _unpack_keys(kf, bits):
    low = (1 << bits) - 1
    k = lax.shift_left(lax.bitcast_convert_type(kf, I32) - KEY_BIAS, 2)
    return _order_key_inv(k & ~((low << 2) | 3)), low - (lax.shift_right_arithmetic(k, 2) & low)


SUBLANES = 8


def _max_rows(k3):
    m8 = jnp.max(k3, axis=0)
    for shift in (4, 2, 1):
        m8 = jnp.maximum(m8, pltpu.roll(m8, shift, axis=0))
    return m8


def _top16_rows(s, bits):
    rows, t = s.shape
    k3 = _pack_keys(s, bits).reshape(rows // SUBLANES, SUBLANES, t)
    vals, idxs = [], []
    for _ in range(PEER_TOPK):
        m8 = _max_rows(k3)
        val, idx = _unpack_keys(m8[0:1, :], bits)
        vals.append(val)
        idxs.append(idx)
        k3 = jnp.where(k3 == m8[None], 0.0, k3)
    return vals, idxs


_PAIRS = [(a, b) for a in range(PEER_TOPK) for b in range(PEER_TOPK) if (a + 1) * (b + 1) <= PEER_TOPK]
_PAIR_ROWS = -(-len(_PAIRS) // 8) * 8


TOPK_HEADS_PER_TRIP = 2


def _peer_topk_kernel(q_ref, keys_ref, eid_ref, gate_ref, eid_t, gate_t, *, tm):
    def head(h):
        tops = []
        for p in range(2):
            col = pl.multiple_of((2 * h + p) * PEER_HALF, PEER_HALF)
            qh = q_ref[:, pl.ds(col, PEER_HALF)]
            s = _dot_nt(keys_ref[2 * h + p], qh)
            tops.append(_top16_rows(s, 7))
        (s1, i1), (s2, i2) = tops
        cand_s = [s1[a] + s2[b] for a, b in _PAIRS]
        cand_e = [i1[a] * PEER_KEYS + i2[b] for a, b in _PAIRS]
        pad = _PAIR_ROWS - len(_PAIRS)
        cs = jnp.concatenate(cand_s + [jnp.full((pad, tm), NEG, F32)], axis=0)
        ce = jnp.concatenate(cand_e + [jnp.zeros((pad, tm), I32)], axis=0)
        k3 = _pack_keys(cs, 6).reshape(_PAIR_ROWS // SUBLANES, SUBLANES, tm)
        ce3 = ce.astype(F32).reshape(_PAIR_ROWS // SUBLANES, SUBLANES, tm)
        sel_s, sel_e = [], []
        for _ in range(PEER_TOPK):
            m8 = _max_rows(k3)
            hit = k3 == m8[None]
            sel_s.append(_unpack_keys(m8[0:1, :], 6)[0])
            sel_e.append(_max_rows(jnp.where(hit, ce3, -1.0))[0:1, :].astype(I32))
            k3 = jnp.where(hit, 0.0, k3)
        top_s = jnp.concatenate(sel_s, axis=0)
        top_e = jnp.concatenate(sel_e, axis=0)
        w = jnp.exp(top_s - top_s[0:1, :])
        gate = w / jnp.sum(w, axis=0, keepdims=True)
        row = pl.multiple_of(h * PEER_TOPK, PEER_TOPK)
        eid_t[pl.ds(row, PEER_TOPK), :] = top_e
        gate_t[pl.ds(row, PEER_TOPK), :] = gate

    def trip(i, carry):
        for u in range(TOPK_HEADS_PER_TRIP):
            head(i * TOPK_HEADS_PER_TRIP + u)
        return carry

    lax.fori_loop(0, PEER_HEADS // TOPK_HEADS_PER_TRIP, trip, 0)
    eid_ref[...] = eid_t[...].T
    gate_ref[...] = gate_t[...].T


def peer_topk(qp, keys, *, tm):
    m = qp.shape[0]
    tm = min(tm, m)
    return pl.pallas_call(
        functools.partial(_peer_topk_kernel, tm=tm),
        out_shape=[jax.ShapeDtypeStruct((m, PEER_SLOTS), I32), jax.ShapeDtypeStruct((m, PEER_SLOTS), F32)],
        grid=(m // tm,),
        in_specs=[pl.BlockSpec((tm, qp.shape[1]), lambda i: (i, 0)),
                  pl.BlockSpec(keys.shape, lambda i: (0, 0, 0))],
        out_specs=[pl.BlockSpec((tm, PEER_SLOTS), lambda i: (i, 0)),
                   pl.BlockSpec((tm, PEER_SLOTS), lambda i: (i, 0))],
        scratch_shapes=[pltpu.VMEM((PEER_SLOTS, tm), I32), pltpu.VMEM((PEER_SLOTS, tm), F32)],
        compiler_params=_cparams("parallel"),
        name="peer_topk",
    )(qp, keys)


GATE_UNROLL = 32


def _peer_gates_kernel(eid_ref, gate_ref, o_ref, g3_ref, *, tm, pitch):
    sub = lax.broadcasted_iota(I32, (PEER_KEYS, PEER_SLOTS), 0)

    def token(t, carry):
        e = eid_ref[pl.ds(t, 1), :]
        g = gate_ref[pl.ds(t, 1), :]
        hi = lax.shift_right_logical(e, 7)
        lo = jnp.bitwise_and(e, PEER_KEYS - 1)
        a1 = jnp.where(sub == hi, g, 0.0).astype(BF16)
        a2 = jnp.where(sub == lo, 1.0, 0.0).astype(BF16)
        g3_ref[pl.ds(t, PEER_KEYS, stride=pitch), :] = _dot_nt(a1, a2)
        return carry

    lax.fori_loop(0, tm, token, 0, unroll=GATE_UNROLL)
    for e1 in range(PEER_KEYS):
        o_ref[:, e1 * PEER_KEYS:(e1 + 1) * PEER_KEYS] = g3_ref[e1 * pitch:e1 * pitch + tm, :].astype(o_ref.dtype)


def peer_gates(eid, gate, *, tm):
    m = eid.shape[0]
    tm = min(tm, m)
    pitch = tm + 8
    return pl.pallas_call(
        functools.partial(_peer_gates_kernel, tm=tm, pitch=pitch),
        out_shape=jax.ShapeDtypeStruct((m, PEER_EXPERTS), BF16),
        grid=(m // tm,),
        in_specs=[pl.BlockSpec((tm, PEER_SLOTS), lambda i: (i, 0)),
                  pl.BlockSpec((tm, PEER_SLOTS), lambda i: (i, 0))],
        out_specs=pl.BlockSpec((tm, PEER_EXPERTS), lambda i: (i, 0)),
        scratch_shapes=[pltpu.VMEM((PEER_KEYS * pitch, PEER_KEYS), F32)],
        compiler_params=_cparams("parallel"),
        name="peer_gates",
    )(eid, gate)


def _peer_dense_kernel(x_ref, sx_ref, u_ref, su_ref, g_ref, v_ref, sv_ref, h_ref, gn_ref, o_ref, aq_scr, sa_scr,
                       *, final_norm, n_blocks):
    j = pl.program_id(1)
    nj = n_blocks

    def step(p, do_values, do_gates):
        if do_gates:
            s = _dot_nt(x_ref[...], u_ref[...])
        if do_values:
            pv = _dot(aq_scr[1 - p], v_ref[...])
        if do_gates:
            s = s * sx_ref[...] * su_ref[...]
            a = 0.5 * s * (1.0 + lax.erf(s * (2.0 ** -0.5)))
            aq_scr[p], sa_scr[p] = _quantize_rows(a * g_ref[...].astype(F32) * sv_ref[...])
        if do_values:
            o_ref[...] += sa_scr[1 - p] * pv

    @pl.when(j == 0)
    def _():
        o_ref[...] = jnp.zeros_like(o_ref)
        step(0, False, True)

    for p in range(2):
        @pl.when(jnp.logical_and(jnp.logical_and(j > 0, j < nj), j % 2 == p))
        def _(p=p):
            step(p, True, True)

    @pl.when(j == nj)
    def _():
        step(nj % 2, True, False)
        y = h_ref[...] + o_ref[...]
        o_ref[...] = _rms(y, gn_ref[...]) if final_norm else y


def peer_dense(xq, sx, uq, su, g, vq, sv, h, gain, *, final_norm, tm, te):
    m, d = xq.shape
    e = uq.shape[0]
    tm = min(tm, m)
    nj = e // te
    once = pl.Buffered(1)

    def cur(j):
        return jnp.minimum(j, nj - 1)

    def prev(j):
        return jnp.maximum(j - 1, 0)

    return pl.pallas_call(
        functools.partial(_peer_dense_kernel, final_norm=final_norm, n_blocks=nj),
        out_shape=jax.ShapeDtypeStruct((m, d), F32),
        grid=(m // tm, nj + 1),
        in_specs=[pl.BlockSpec((tm, d), lambda i, j: (i, 0)),
                  pl.BlockSpec((tm, 1), lambda i, j: (i, 0)),
                  pl.BlockSpec((te, d), lambda i, j: (cur(j), 0)),
                  pl.BlockSpec((1, te), lambda i, j: (0, cur(j))),
                  pl.BlockSpec((tm, te), lambda i, j: (i, cur(j))),
                  pl.BlockSpec((te, d), lambda i, j: (prev(j), 0)),
                  pl.BlockSpec((1, te), lambda i, j: (0, cur(j))),
                  pl.BlockSpec((tm, d), lambda i, j: (i, 0), pipeline_mode=once),
                  pl.BlockSpec((1, d), lambda i, j: (0, 0))],
        out_specs=pl.BlockSpec((tm, d), lambda i, j: (i, 0)),
        scratch_shapes=[pltpu.VMEM((2, tm, te), F8), pltpu.VMEM((2, tm, 1), F32)],
        compiler_params=_cparams("parallel", "arbitrary"),
        name="peer_dense",
    )(xq, sx, uq, su.reshape(1, e), g, vq, sv.reshape(1, e), h, gain.reshape(1, d).astype(F32))


def _w_in_columns():
    src = {}
    off = 0
    for name, width in (("cq", MLA_Q_LORA), ("ckv", MLA_KV_LORA), ("kr", MLA_ROPE), ("gq", GLA_K_WIDTH),
                        ("gk", GLA_K_WIDTH), ("gv", GLA_V_WIDTH), ("lr", GLA_RANK), ("og", GLA_V_WIDTH)):
        src[name] = np.arange(off, off + width)
        off += width
    cols = np.full((IN_PAD,), -1, np.int64)
    for name, start in (("cq", COL_CQ), ("ckv", COL_CKV), ("gq", COL_GQ), ("gk", COL_GK),
                        ("gv", COL_GV), ("og", COL_OG), ("lr", COL_LR)):
        cols[start:start + len(src[name])] = src[name]
    half = MLA_ROPE // 2
    x1, x2 = src["kr"][:half], src["kr"][half:]
    cols[COL_KR:COL_KR + LANES] = np.concatenate([x1, x1, x2, x2])
    return cols


def _w_uq_columns():
    half = MLA_ROPE // 2
    nope = [h * MLA_QK + d for h in range(MLA_HEADS) for d in range(MLA_NOPE)]
    rope = []
    for j in range(MLA_HEADS // 2):
        a, b = 2 * j, 2 * j + 1
        for part in (0, 1):
            for h in (a, b):
                rope += [h * MLA_QK + MLA_NOPE + part * half + r for r in range(half)]
    return np.array(nope + rope, np.int64)


def _column_runs(cols):
    runs = []
    start = 0
    for i in range(1, len(cols) + 1):
        if i == len(cols) or cols[i] != (cols[i - 1] + 1 if cols[i - 1] >= 0 else -1):
            runs.append((int(cols[start]), i - start))
            start = i
    return runs


def _take_cols(w, cols, dtype):
    pieces = [jnp.zeros((w.shape[0], n), dtype) if src < 0 else w[:, src:src + n].astype(dtype)
              for src, n in _column_runs(cols)]
    return jnp.concatenate(pieces, axis=1)


def _take_rows(wt, rows, dtype):
    pieces = [jnp.zeros((n, wt.shape[1]), dtype) if src < 0 else wt[src:src + n].astype(dtype)
              for src, n in _column_runs(rows)]
    return jnp.concatenate(pieces, axis=0)


def _rope_tables(positions):
    half = MLA_ROPE // 2
    inv_freq = ROPE_THETA ** (-jnp.arange(0, MLA_ROPE, 2, dtype=F32) / MLA_ROPE)
    ang = positions.astype(F32).reshape(-1, 1) * inv_freq
    cos = jnp.cos(ang)
    sin = jnp.sin(ang)
    return jnp.tile(cos, (1, 4)), jnp.concatenate([-sin, -sin, sin, sin], axis=1)


def kernel(x, mem, positions, norm_mem, norm_mix, w_in, mla_q_norm, w_uq, mla_kv_norm, w_ukv, mla_out_norm,
           w_gate_up, b_gate, gla_out_norm, w_out, norm_cross, w_cq, w_ck, w_cv, w_co, norm_ffn, w_peer_q,
           peer_sub_keys, peer_u, peer_v, norm_final):
    batch, seq, d = x.shape
    mem_tokens = mem.shape[1]
    m = batch * seq
    h = x.reshape(m, d)
    cos_t, sin_t = _rope_tables(positions)

    mn_kv = norm_matmul(mem.reshape(batch * mem_tokens, d), norm_mem,
                        jnp.concatenate([w_ck[0], w_cv[0]], axis=1).astype(BF16),
                        k=d, tm=512, tn=1024, out_dtype=BF16, name="mem_kv_proj")

    for l in range(norm_mix.shape[0]):
        w_in_t = _take_rows(jnp.transpose(w_in[l]), _w_in_columns(), BF16)
        h_in = norm_matmul(h, norm_mix[l], w_in_t, k=d, tm=512, tn=1024, out_dtype=BF16, w_transposed=True,
                           name="in_proj")

        w_uq_p = _take_cols(w_uq[l], _w_uq_columns(), BF16)
        q_full = mla_q_proj(h_in, mla_q_norm[l], w_uq_p, cos_t, sin_t, tm=512)
        kv, k_rope = mla_kv_proj(h_in, mla_kv_norm[l], w_ukv[l].astype(BF16), cos_t, sin_t, tm=512)
        o_mla = mla_attention(q_full, kv, k_rope, batch=batch, seq=seq, blk=512)

        w_gate_pad = jnp.zeros((LANES, GLA_K_WIDTH), F32).at[:GLA_RANK].set(w_gate_up[l]).astype(BF16)
        o_gla = gla_mixer(h_in, w_gate_pad, b_gate[l], gla_out_norm[l], batch=batch, seq=seq, sb=1024)

        h = mixer_out_proj(o_mla, mla_out_norm[l], o_gla, w_out[l].astype(BF16), h, tm=512, tn=1024)

        qx = norm_matmul(h, norm_cross[l], w_cq[l].astype(BF16), k=d, tm=512, tn=1024, out_dtype=BF16,
                         name="xattn_q_proj")
        h = cross_attention(qx, mn_kv, w_co[l].astype(BF16), h, batch=batch, seq=seq,
                            mem_tokens=mem_tokens, tm=256)

        qp, xq, sx = norm_matmul(h, norm_ffn[l], w_peer_q[l].astype(BF16), k=d, tm=256, tn=2048, out_dtype=BF16,
                                 emit_xq=True, name="peer_q_proj")
        keys = peer_sub_keys[l].reshape(PEER_HEADS * 2, PEER_KEYS, PEER_HALF).astype(BF16)
        eid, gate = peer_topk(qp, keys, tm=256)
        g = peer_gates(eid, gate, tm=128)
        uq, su = quantize_rows(peer_u[l], tr=512, name="peer_u_quant")
        vq, sv = quantize_rows(peer_v[l], tr=512, name="peer_v_quant")
        h = peer_dense(xq, sx, uq, su, g, vq, sv, h, norm_final,
                       final_norm=(l + 1 == norm_mix.shape[0]), tm=512, te=512)
    return h.reshape(batch, seq, d)
```

```python
import functools
import math

import numpy as np
import jax
import jax.numpy as jnp
from jax import lax
from jax.experimental import pallas as pl
from jax.experimental.pallas import tpu as pltpu

F32 = jnp.float32
BF16 = jnp.bfloat16
F8 = jnp.float8_e4m3fn
I32 = jnp.int32
FP8_AMAX = 256.0

EPS = 1e-6
ROPE_THETA = 10000.0

MLA_HEADS = 16
MLA_Q_LORA = 1024
MLA_KV_LORA = 512
MLA_NOPE = 128
MLA_ROPE = 64
MLA_QK = MLA_NOPE + MLA_ROPE
MLA_V = 128
MLA_WIDTH = MLA_HEADS * MLA_V

GLA_HEADS = 4
GLA_HEAD_K = 256
GLA_HEAD_V = 512
GLA_K_WIDTH = GLA_HEADS * GLA_HEAD_K
GLA_V_WIDTH = GLA_HEADS * GLA_HEAD_V
GLA_RANK = 16
GLA_TAU = 16.0
GLA_CHUNK = 64

XATTN_HEADS = 4
XATTN_DIM = 256
XATTN_WIDTH = XATTN_HEADS * XATTN_DIM

PEER_HEADS = 8
PEER_KEYS = 128
PEER_HALF = 128
PEER_TOPK = 16
PEER_SLOTS = PEER_HEADS * PEER_TOPK
PEER_EXPERTS = PEER_KEYS * PEER_KEYS

LANES = 128
VMEM_LIMIT = 56 * 1024 * 1024

COL_CQ = 0
COL_CKV = COL_CQ + MLA_Q_LORA
COL_GQ = COL_CKV + MLA_KV_LORA
COL_GK = COL_GQ + GLA_K_WIDTH
COL_GV = COL_GK + GLA_K_WIDTH
COL_OG = COL_GV + GLA_V_WIDTH
COL_KR = COL_OG + GLA_V_WIDTH
COL_LR = COL_KR + LANES
IN_PAD = 8192

NEG = -1e30


def _cparams(*sem):
    return pltpu.CompilerParams(dimension_semantics=sem, vmem_limit_bytes=VMEM_LIMIT)


def _rms(x, g):
    ms = jnp.mean(x * x, axis=-1, keepdims=True)
    return x * lax.rsqrt(ms + EPS) * g


def _dot(a, b):
    return jnp.dot(a, b, preferred_element_type=F32)


def _dot_nt(a, b):
    return lax.dot_general(a, b, (((1,), (1,)), ((), ())), preferred_element_type=F32)


def _dot_tn(a, b):
    return lax.dot_general(a, b, (((0,), (0,)), ((), ())), preferred_element_type=F32)


def _quantize_rows(x):
    amax = jnp.max(jnp.abs(x), axis=-1, keepdims=True)
    scale = jnp.where(amax > 0.0, amax * (1.0 / FP8_AMAX), 1.0)
    return (x * (1.0 / scale)).astype(F8), scale


def _norm_matmul_kernel(x_ref, g_ref, w_ref, o_ref, *rest, emit_xq, whole_n, w_transposed):
    if emit_xq:
        xq_ref, sx_ref, xn_ref = rest
    else:
        (xn_ref,) = rest

    def normalise():
        xn = _rms(x_ref[...].astype(F32), g_ref[...])
        xn_ref[...] = xn.astype(BF16)
        if emit_xq:
            xq_ref[...], sx_ref[...] = _quantize_rows(xn)

    if whole_n:
        normalise()
    else:
        pl.when(pl.program_id(1) == 0)(normalise)
    mm = _dot_nt if w_transposed else _dot
    o_ref[...] = mm(xn_ref[...], w_ref[...]).astype(o_ref.dtype)


def norm_matmul(x, gain, w, *, k, x_col_blk=0, tm, tn, out_dtype, emit_xq=False, w_transposed=False, name):
    m = x.shape[0]
    n = w.shape[0] if w_transposed else w.shape[1]
    tm = min(tm, m)
    tn = min(tn, n)
    out_shape = [jax.ShapeDtypeStruct((m, n), out_dtype)]
    out_specs = [pl.BlockSpec((tm, tn), lambda i, j: (i, j))]
    if emit_xq:
        out_shape += [jax.ShapeDtypeStruct((m, k), F8), jax.ShapeDtypeStruct((m, 1), F32)]
        out_specs += [pl.BlockSpec((tm, k), lambda i, j: (i, 0)), pl.BlockSpec((tm, 1), lambda i, j: (i, 0))]
    whole_n = tn == n
    w_mode = pl.Buffered(1) if whole_n else None
    w_spec = (pl.BlockSpec((tn, k), lambda i, j: (j, 0), pipeline_mode=w_mode) if w_transposed
              else pl.BlockSpec((k, tn), lambda i, j: (0, j), pipeline_mode=w_mode))
    res = pl.pallas_call(
        functools.partial(_norm_matmul_kernel, emit_xq=emit_xq, whole_n=whole_n, w_transposed=w_transposed),
        out_shape=out_shape,
        grid=(m // tm, n // tn),
        in_specs=[pl.BlockSpec((tm, k), lambda i, j: (i, x_col_blk)),
                  pl.BlockSpec((1, k), lambda i, j: (0, 0)),
                  w_spec],
        out_specs=out_specs,
        scratch_shapes=[pltpu.VMEM((tm, k), BF16)],
        compiler_params=_cparams("parallel", "arbitrary"),
        name=name,
    )(x, gain.reshape(1, k).astype(F32), w)
    return res if emit_xq else res[0]


def _quantize_kernel(w_ref, q_ref, s_ref):
    q_ref[...], s_ref[...] = _quantize_rows(w_ref[...])


def quantize_rows(w, *, tr, name):
    r, d = w.shape
    return pl.pallas_call(
        _quantize_kernel,
        out_shape=[jax.ShapeDtypeStruct((r, d), F8), jax.ShapeDtypeStruct((r, 1), F32)],
        grid=(r // tr,),
        in_specs=[pl.BlockSpec((tr, d), lambda i: (i, 0))],
        out_specs=[pl.BlockSpec((tr, d), lambda i: (i, 0)), pl.BlockSpec((tr, 1), lambda i: (i, 0))],
        compiler_params=_cparams("parallel"),
        name=name,
    )(w)


def _rope_pair(r, cosv, sinv):
    return r * cosv + pltpu.roll(r, 2 * 32, axis=1) * sinv


def _qproj_kernel(c_ref, g_ref, w_ref, cos_ref, sin_ref, o_ref, *, scale):
    xn = _rms(c_ref[...].astype(F32), g_ref[...]).astype(BF16)
    q = _dot(xn, w_ref[...])
    cosv = cos_ref[...]
    sinv = sin_ref[...]
    lane = lax.broadcasted_iota(I32, (1, LANES), 1)
    rope_base = MLA_HEADS * MLA_NOPE
    for j in range(MLA_HEADS // 2):
        r = q[:, rope_base + j * LANES: rope_base + (j + 1) * LANES]
        r = _rope_pair(r, cosv, sinv) * scale
        for p in range(2):
            h = 2 * j + p
            own = ((lane // 32) % 2) == p
            o_ref[:, h * 256: h * 256 + 128] = (q[:, h * 128:(h + 1) * 128] * scale).astype(o_ref.dtype)
            o_ref[:, h * 256 + 128: h * 256 + 256] = jnp.where(own, r, 0.0).astype(o_ref.dtype)


def mla_q_proj(h_in, gain, w_uq_perm, cos_t, sin_t, *, tm):
    m = h_in.shape[0]
    tm = min(tm, m)
    return pl.pallas_call(
        functools.partial(_qproj_kernel, scale=MLA_QK ** -0.5 * math.log2(math.e)),
        out_shape=jax.ShapeDtypeStruct((m, MLA_HEADS * 256), BF16),
        grid=(m // tm,),
        in_specs=[pl.BlockSpec((tm, MLA_Q_LORA), lambda i: (i, COL_CQ // MLA_Q_LORA)),
                  pl.BlockSpec((1, MLA_Q_LORA), lambda i: (0, 0)),
                  pl.BlockSpec(w_uq_perm.shape, lambda i: (0, 0)),
                  pl.BlockSpec((tm, LANES), lambda i: (i, 0)),
                  pl.BlockSpec((tm, LANES), lambda i: (i, 0))],
        out_specs=pl.BlockSpec((tm, MLA_HEADS * 256), lambda i: (i, 0)),
        compiler_params=_cparams("parallel"),
        name="mla_q_proj",
    )(h_in, gain.reshape(1, -1).astype(F32), w_uq_perm, cos_t, sin_t)


def _kvproj_kernel(c_ref, g_ref, w_ref, kr_ref, cos_ref, sin_ref, kv_ref, kro_ref):
    xn = _rms(c_ref[...].astype(F32), g_ref[...]).astype(BF16)
    kv_ref[...] = _dot(xn, w_ref[...]).astype(kv_ref.dtype)
    kro_ref[...] = _rope_pair(kr_ref[...].astype(F32), cos_ref[...], sin_ref[...]).astype(kro_ref.dtype)


def mla_kv_proj(h_in, gain, w_ukv, cos_t, sin_t, *, tm):
    m = h_in.shape[0]
    tm = min(tm, m)
    n = w_ukv.shape[1]
    return pl.pallas_call(
        _kvproj_kernel,
        out_shape=[jax.ShapeDtypeStruct((m, n), BF16), jax.ShapeDtypeStruct((m, LANES), BF16)],
        grid=(m // tm,),
        in_specs=[pl.BlockSpec((tm, MLA_KV_LORA), lambda i: (i, COL_CKV // MLA_KV_LORA)),
                  pl.BlockSpec((1, MLA_KV_LORA), lambda i: (0, 0)),
                  pl.BlockSpec(w_ukv.shape, lambda i: (0, 0)),
                  pl.BlockSpec((tm, LANES), lambda i: (i, COL_KR // LANES)),
                  pl.BlockSpec((tm, LANES), lambda i: (i, 0)),
                  pl.BlockSpec((tm, LANES), lambda i: (i, 0))],
        out_specs=[pl.BlockSpec((tm, n), lambda i: (i, 0)),
                   pl.BlockSpec((tm, LANES), lambda i: (i, 0))],
        compiler_params=_cparams("parallel"),
        name="mla_kv_proj",
    )(h_in, gain.reshape(1, -1).astype(F32), w_ukv, h_in, cos_t, sin_t)


ATTN_UNROLL = 8


def _mla_attn_kernel(qi_tab, kj_tab, q_ref, kv_ref, kr_ref, o_ref, s0_scr, s1_scr, m_scr, acc_scr,
                     *, blk, nq, n_items):
    ones = jnp.ones((blk, LANES), BF16)
    s_scr = (s0_scr, s1_scr)

    def scores(t, s_ref):
        q_rows = pl.ds(pl.multiple_of(qi_tab[t] * blk, blk), blk)
        k_rows = pl.ds(pl.multiple_of(kj_tab[t] * blk, blk), blk)
        k = jnp.concatenate([kv_ref[k_rows, :MLA_NOPE], kr_ref[k_rows, :]], axis=1)
        s_ref[...] = _dot_nt(q_ref[q_rows, :], k)

    def finish(t, s_ref, diagonal):
        qi = qi_tab[t]
        k_rows = pl.ds(pl.multiple_of(kj_tab[t] * blk, blk), blk)
        v = jnp.concatenate([kv_ref[k_rows, MLA_NOPE:], ones], axis=1)
        s = s_ref[...]
        if diagonal:
            row = lax.broadcasted_iota(I32, (blk, blk), 0)
            col = lax.broadcasted_iota(I32, (blk, blk), 1)
            s = jnp.where(col <= row, s, NEG)
            m_new = jnp.max(s, axis=1, keepdims=True) + jnp.zeros((blk, LANES), F32)
        else:
            m = m_scr[qi]
            m_new = jnp.maximum(m, jnp.max(s, axis=1, keepdims=True))
        p = jnp.exp2(s - jnp.concatenate([m_new] * (blk // LANES), axis=1)).astype(BF16)
        pv = _dot(p, v)
        if diagonal:
            acc_scr[qi] = pv
        else:
            alpha = jnp.exp2(m - m_new)
            acc_scr[qi] = jnp.concatenate([alpha, alpha], axis=1) * acc_scr[qi] + pv
        m_scr[qi] = m_new

    def run(t0, n, slot, diagonal):
        for u in range(n):
            scores(jnp.minimum(t0 + u + 1, n_items - 1), s_scr[(slot + u + 1) % 2])
            finish(t0 + u, s_scr[(slot + u) % 2], diagonal)

    def phase(t0, n, slot, diagonal):
        def body(i, carry):
            run(t0 + i * ATTN_UNROLL, ATTN_UNROLL, slot, diagonal)
            return carry

        lax.fori_loop(0, n // ATTN_UNROLL, body, 0)
        if n % ATTN_UNROLL:
            run(t0 + n - n % ATTN_UNROLL, n % ATTN_UNROLL, slot, diagonal)
        return (slot + n) % 2

    scores(0, s0_scr)
    slot = phase(0, nq, 0, True)
    phase(nq, n_items - nq, slot, False)

    def normalise(qi, carry):
        acc = acc_scr[qi]
        o_ref[pl.ds(pl.multiple_of(qi * blk, blk), blk), :] = (acc[:, :MLA_V] / acc[:, MLA_V:]).astype(o_ref.dtype)
        return carry

    lax.fori_loop(0, nq, normalise, 0)


def mla_attention(q_full, kv, k_rope, *, batch, seq, blk):
    blk = min(blk, seq)
    nq = seq // blk
    m = batch * seq
    pairs = [(i, i) for i in range(nq)] + [(i, i - d) for d in range(1, nq) for i in range(d, nq)]
    qi_tab = jnp.asarray([p[0] for p in pairs], I32)
    kj_tab = jnp.asarray([p[1] for p in pairs], I32)
    grid_spec = pltpu.PrefetchScalarGridSpec(
        num_scalar_prefetch=2,
        grid=(batch, MLA_HEADS),
        in_specs=[pl.BlockSpec((seq, 256), lambda b, h, *_: (b, h)),
                  pl.BlockSpec((seq, 256), lambda b, h, *_: (b, h)),
                  pl.BlockSpec((seq, LANES), lambda b, h, *_: (b, 0))],
        out_specs=pl.BlockSpec((seq, MLA_V), lambda b, h, *_: (b, h)),
        scratch_shapes=[pltpu.VMEM((blk, blk), F32), pltpu.VMEM((blk, blk), F32),
                        pltpu.VMEM((nq, blk, LANES), F32), pltpu.VMEM((nq, blk, 2 * MLA_V), F32)],
    )
    return pl.pallas_call(
        functools.partial(_mla_attn_kernel, blk=blk, nq=nq, n_items=len(pairs)),
        out_shape=jax.ShapeDtypeStruct((m, MLA_WIDTH), BF16),
        grid_spec=grid_spec,
        compiler_params=_cparams("parallel", "parallel"),
        name="mla_attention",
    )(qi_tab, kj_tab, q_full, kv, k_rope)


def _gla_kernel(q_ref, k_ref, v_ref, og_ref, lr_ref, wg_ref, bg_ref, gn_ref, o_ref, st_ref, *, sb, c):
    @pl.when(pl.program_id(2) == 0)
    def _():
        st_ref[...] = jnp.zeros_like(st_ref)

    z = _dot(lr_ref[...], wg_ref[...]) + bg_ref[...]
    log_a = (jnp.minimum(z, 0.0) - jnp.log(1.0 + jnp.exp(-jnp.abs(z)))) * (1.0 / GLA_TAU)
    rows = lax.broadcasted_iota(I32, (c, c), 0)
    cols = lax.broadcasted_iota(I32, (c, c), 1)
    causal = cols <= rows
    tril = causal.astype(F32)
    gn = gn_ref[...]
    scale = GLA_HEAD_K ** -0.5

    for n in range(sb // c):
        sl = slice(n * c, (n + 1) * c)
        g = log_a[sl, :]
        b = jnp.dot(tril, g, preferred_element_type=F32, precision=lax.Precision.HIGHEST)
        b_last = b[c - 1:c, :]
        q = q_ref[sl, :].astype(F32)
        k = k_ref[sl, :].astype(F32)
        v = v_ref[sl, :]
        qe = (q * scale * jnp.exp(b)).astype(BF16)
        ke = (k * jnp.exp(-b)).astype(BF16)
        kd = (k * jnp.exp(b_last - b)).astype(BF16)
        att = jnp.where(causal, _dot_nt(qe, ke), 0.0).astype(BF16)
        st = st_ref[...]
        o = _dot(att, v) + _dot_nt(qe, st.astype(BF16))
        st_ref[...] = st * jnp.exp(b_last) + _dot_tn(v, kd)
        og = og_ref[sl, :].astype(F32)
        o = _rms(o, gn) * (og * (1.0 / (1.0 + jnp.exp(-og))))
        o_ref[sl, :] = o.astype(o_ref.dtype)


def gla_mixer(h_in, w_gate_pad, b_gate, gla_out_norm, *, batch, seq, sb):
    sb = min(sb, seq)
    nsb = seq // sb
    m = batch * seq
    return pl.pallas_call(
        functools.partial(_gla_kernel, sb=sb, c=GLA_CHUNK),
        out_shape=jax.ShapeDtypeStruct((m, GLA_V_WIDTH), BF16),
        grid=(batch, GLA_HEADS, nsb),
        in_specs=[pl.BlockSpec((sb, GLA_HEAD_K), lambda b, h, s: (b * nsb + s, COL_GQ // GLA_HEAD_K + h)),
                  pl.BlockSpec((sb, GLA_HEAD_K), lambda b, h, s: (b * nsb + s, COL_GK // GLA_HEAD_K + h)),
                  pl.BlockSpec((sb, GLA_HEAD_V), lambda b, h, s: (b * nsb + s, COL_GV // GLA_HEAD_V + h)),
                  pl.BlockSpec((sb, GLA_HEAD_V), lambda b, h, s: (b * nsb + s, COL_OG // GLA_HEAD_V + h)),
                  pl.BlockSpec((sb, LANES), lambda b, h, s: (b * nsb + s, COL_LR // LANES)),
                  pl.BlockSpec((LANES, GLA_HEAD_K), lambda b, h, s: (0, h)),
                  pl.BlockSpec((1, GLA_HEAD_K), lambda b, h, s: (0, h)),
                  pl.BlockSpec((1, GLA_HEAD_V), lambda b, h, s: (0, 0))],
        out_specs=pl.BlockSpec((sb, GLA_HEAD_V), lambda b, h, s: (b * nsb + s, h)),
        scratch_shapes=[pltpu.VMEM((GLA_HEAD_V, GLA_HEAD_K), F32)],
        compiler_params=_cparams("parallel", "parallel", "arbitrary"),
        name="gla_mixer",
    )(h_in, h_in, h_in, h_in, h_in, w_gate_pad, b_gate.reshape(1, -1).astype(F32),
      gla_out_norm.reshape(1, -1).astype(F32))


def _out_proj_kernel(om_ref, g_ref, og_ref, w_ref, x_ref, o_ref, mix_ref):
    @pl.when(pl.program_id(1) == 0)
    def _():
        mix_ref[:, :MLA_WIDTH] = _rms(om_ref[...].astype(F32), g_ref[...]).astype(BF16)
        mix_ref[:, MLA_WIDTH:] = og_ref[...]

    o_ref[...] = x_ref[...] + _dot(mix_ref[...], w_ref[...])


def mixer_out_proj(o_mla, mla_out_norm, o_gla, w_out, x, *, tm, tn):
    m, d = x.shape
    tm = min(tm, m)
    tn = min(tn, d)
    kmix = MLA_WIDTH + GLA_V_WIDTH
    return pl.pallas_call(
        _out_proj_kernel,
        out_shape=jax.ShapeDtypeStruct((m, d), F32),
        grid=(m // tm, d // tn),
        in_specs=[pl.BlockSpec((tm, MLA_WIDTH), lambda i, j: (i, 0)),
                  pl.BlockSpec((1, MLA_WIDTH), lambda i, j: (0, 0)),
                  pl.BlockSpec((tm, GLA_V_WIDTH), lambda i, j: (i, 0)),
                  pl.BlockSpec((kmix, tn), lambda i, j: (0, j)),
                  pl.BlockSpec((tm, tn), lambda i, j: (i, j))],
        out_specs=pl.BlockSpec((tm, tn), lambda i, j: (i, j)),
        scratch_shapes=[pltpu.VMEM((tm, kmix), BF16)],
        compiler_params=_cparams("parallel", "arbitrary"),
        name="mixer_out_proj",
    )(o_mla, mla_out_norm.reshape(1, -1).astype(F32), o_gla, w_out, x)


def _xattn_kernel(q_ref, kv_ref, w_ref, h_ref, o_ref):
    scale = XATTN_DIM ** -0.5
    outs = []
    for h in range(XATTN_HEADS):
        q = q_ref[:, h * XATTN_DIM:(h + 1) * XATTN_DIM]
        k = kv_ref[:, h * XATTN_DIM:(h + 1) * XATTN_DIM]
        v = kv_ref[:, XATTN_WIDTH + h * XATTN_DIM: XATTN_WIDTH + (h + 1) * XATTN_DIM]
        s = _dot_nt(q, k) * scale
        p = jnp.exp(s - jnp.max(s, axis=1, keepdims=True))
        p = p / jnp.sum(p, axis=1, keepdims=True)
        outs.append(_dot(p.astype(BF16), v).astype(BF16))
    o = jnp.concatenate(outs, axis=1)
    o_ref[...] = h_ref[...] + _dot(o, w_ref[...])


def cross_attention(qx, kvm, w_co, h1, *, batch, seq, mem_tokens, tm):
    m, d = h1.shape
    tm = min(tm, seq)
    per_b = seq // tm
    return pl.pallas_call(
        _xattn_kernel,
        out_shape=jax.ShapeDtypeStruct((m, d), F32),
        grid=(m // tm,),
        in_specs=[pl.BlockSpec((tm, XATTN_WIDTH), lambda i: (i, 0)),
                  pl.BlockSpec((mem_tokens, 2 * XATTN_WIDTH), lambda i: (i // per_b, 0)),
                  pl.BlockSpec(w_co.shape, lambda i: (0, 0)),
                  pl.BlockSpec((tm, d), lambda i: (i, 0))],
        out_specs=pl.BlockSpec((tm, d), lambda i: (i, 0)),
        compiler_params=_cparams("parallel"),
        name="cross_attention",
    )(qx, kvm, w_co, h1)


def _order_key(x):
    b = lax.bitcast_convert_type(x, I32)
    return b ^ (lax.shift_right_arithmetic(b, 31) & 0x7FFFFFFF)


def _order_key_inv(k):
    return lax.bitcast_convert_type(k ^ (lax.shift_right_arithmetic(k, 31) & 0x7FFFFFFF), F32)


KEY_BIAS = 0x20000000


def _pack_keys(x, bits):
    low = (1 << bits) - 1
    row = lax.broadcasted_iota(I32, x.shape, 0)
    k = (_order_key(x) & ~((low << 2) | 3)) | ((low - row) << 2)
    return lax.bitcast_convert_type(lax.shift_right_arithmetic(k, 2) + KEY_BIAS, F32)


def _unpack_keys(kf, bits):
    low = (1 << bits) - 1
    k = lax.shift_left(lax.bitcast_convert_type(kf, I32) - KEY_BIAS, 2)
    return _order_key_inv(k & ~((low << 2) | 3)), low - (lax.shift_right_arithmetic(k, 2) & low)


SUBLANES = 8


def _max_rows(k3):
    m8 = jnp.max(k3, axis=0)
    for shift in (4, 2, 1):
        m8 = jnp.maximum(m8, pltpu.roll(m8, shift, axis=0))
    return m8


def _top16_rows(s, bits):
    rows, t = s.shape
    k3 = _pack_keys(s, bits).reshape(rows // SUBLANES, SUBLANES, t)
    vals, idxs = [], []
    for _ in range(PEER_TOPK):
        m8 = _max_rows(k3)
        val, idx = _unpack_keys(m8[0:1, :], bits)
        vals.append(val)
        idxs.append(idx)
        k3 = jnp.where(k3 == m8[None], 0.0, k3)
    return vals, idxs


_PAIRS = [(a, b) for a in range(PEER_TOPK) for b in range(PEER_TOPK) if (a + 1) * (b + 1) <= PEER_TOPK]
_PAIR_ROWS = -(-len(_PAIRS) // 8) * 8


TOPK_HEADS_PER_TRIP = 2


def _peer_topk_kernel(q_ref, keys_ref, eid_ref, gate_ref, eid_t, gate_t, *, tm):
    def head(h):
        tops = []
        for p in range(2):
            col = pl.multiple_of((2 * h + p) * PEER_HALF, PEER_HALF)
            qh = q_ref[:, pl.ds(col, PEER_HALF)]
            s = _dot_nt(keys_ref[2 * h + p], qh)
            tops.append(_top16_rows(s, 7))
        (s1, i1), (s2, i2) = tops
        cand_s = [s1[a] + s2[b] for a, b in _PAIRS]
        cand_e = [i1[a] * PEER_KEYS + i2[b] for a, b in _PAIRS]
        pad = _PAIR_ROWS - len(_PAIRS)
        cs = jnp.concatenate(cand_s + [jnp.full((pad, tm), NEG, F32)], axis=0)
        ce = jnp.concatenate(cand_e + [jnp.zeros((pad, tm), I32)], axis=0)
        k3 = _pack_keys(cs, 6).reshape(_PAIR_ROWS // SUBLANES, SUBLANES, tm)
        ce3 = ce.astype(F32).reshape(_PAIR_ROWS // SUBLANES, SUBLANES, tm)
        sel_s, sel_e = [], []
        for _ in range(PEER_TOPK):
            m8 = _max_rows(k3)
            hit = k3 == m8[None]
            sel_s.append(_unpack_keys(m8[0:1, :], 6)[0])
            sel_e.append(_max_rows(jnp.where(hit, ce3, -1.0))[0:1, :].astype(I32))
            k3 = jnp.where(hit, 0.0, k3)
        top_s = jnp.concatenate(sel_s, axis=0)
        top_e = jnp.concatenate(sel_e, axis=0)
        w = jnp.exp(top_s - top_s[0:1, :])
        gate = w / jnp.sum(w, axis=0, keepdims=True)
        row = pl.multiple_of(h * PEER_TOPK, PEER_TOPK)
        eid_t[pl.ds(row, PEER_TOPK), :] = top_e
        gate_t[pl.ds(row, PEER_TOPK), :] = gate

    def trip(i, carry):
        for u in range(TOPK_HEADS_PER_TRIP):
            head(i * TOPK_HEADS_PER_TRIP + u)
        return carry

    lax.fori_loop(0, PEER_HEADS // TOPK_HEADS_PER_TRIP, trip, 0)
    eid_ref[...] = eid_t[...].T
    gate_ref[...] = gate_t[...].T


def peer_topk(qp, keys, *, tm):
    m = qp.shape[0]
    tm = min(tm, m)
    return pl.pallas_call(
        functools.partial(_peer_topk_kernel, tm=tm),
        out_shape=[jax.ShapeDtypeStruct((m, PEER_SLOTS), I32), jax.ShapeDtypeStruct((m, PEER_SLOTS), F32)],
        grid=(m // tm,),
        in_specs=[pl.BlockSpec((tm, qp.shape[1]), lambda i: (i, 0)),
                  pl.BlockSpec(keys.shape, lambda i: (0, 0, 0))],
        out_specs=[pl.BlockSpec((tm, PEER_SLOTS), lambda i: (i, 0)),
                   pl.BlockSpec((tm, PEER_SLOTS), lambda i: (i, 0))],
        scratch_shapes=[pltpu.VMEM((PEER_SLOTS, tm), I32), pltpu.VMEM((PEER_SLOTS, tm), F32)],
        compiler_params=_cparams("parallel"),
        name="peer_topk",
    )(qp, keys)


GATE_UNROLL = 32


def _peer_gates_kernel(eid_ref, gate_ref, o_ref, g3_ref, *, tm, pitch):
    sub = lax.broadcasted_iota(I32, (PEER_KEYS, PEER_SLOTS), 0)

    def token(t, carry):
        e = eid_ref[pl.ds(t, 1), :]
        g = gate_ref[pl.ds(t, 1), :]
        hi = lax.shift_right_logical(e, 7)
        lo = jnp.bitwise_and(e, PEER_KEYS - 1)
        a1 = jnp.where(sub == hi, g, 0.0).astype(BF16)
        a2 = jnp.where(sub == lo, 1.0, 0.0).astype(BF16)
        g3_ref[pl.ds(t, PEER_KEYS, stride=pitch), :] = _dot_nt(a1, a2)
        return carry

    lax.fori_loop(0, tm, token, 0, unroll=GATE_UNROLL)
    for e1 in range(PEER_KEYS):
        o_ref[:, e1 * PEER_KEYS:(e1 + 1) * PEER_KEYS] = g3_ref[e1 * pitch:e1 * pitch + tm, :].astype(o_ref.dtype)


def peer_gates(eid, gate, *, tm):
    m = eid.shape[0]
    tm = min(tm, m)
    pitch = tm + 8
    return pl.pallas_call(
        functools.partial(_peer_gates_kernel, tm=tm, pitch=pitch),
        out_shape=jax.ShapeDtypeStruct((m, PEER_EXPERTS), BF16),
        grid=(m // tm,),
        in_specs=[pl.BlockSpec((tm, PEER_SLOTS), lambda i: (i, 0)),
                  pl.BlockSpec((tm, PEER_SLOTS), lambda i: (i, 0))],
        out_specs=pl.BlockSpec((tm, PEER_EXPERTS), lambda i: (i, 0)),
        scratch_shapes=[pltpu.VMEM((PEER_KEYS * pitch, PEER_KEYS), F32)],
        compiler_params=_cparams("parallel"),
        name="peer_gates",
    )(eid, gate)


def _peer_dense_kernel(x_ref, sx_ref, u_ref, su_ref, g_ref, v_ref, sv_ref, h_ref, gn_ref, o_ref, aq_scr, sa_scr,
                       *, final_norm, n_blocks):
    j = pl.program_id(1)
    nj = n_blocks

    def step(p, do_values, do_gates):
        if do_gates:
            s = _dot_nt(x_ref[...], u_ref[...])
        if do_values:
            pv = _dot(aq_scr[1 - p], v_ref[...])
        if do_gates:
            s = s * sx_ref[...] * su_ref[...]
            a = 0.5 * s * (1.0 + lax.erf(s * (2.0 ** -0.5)))
            aq_scr[p], sa_scr[p] = _quantize_rows(a * g_ref[...].astype(F32) * sv_ref[...])
        if do_values:
            o_ref[...] += sa_scr[1 - p] * pv

    @pl.when(j == 0)
    def _():
        o_ref[...] = jnp.zeros_like(o_ref)
        step(0, False, True)

    for p in range(2):
        @pl.when(jnp.logical_and(jnp.logical_and(j > 0, j < nj), j % 2 == p))
        def _(p=p):
            step(p, True, True)

    @pl.when(j == nj)
    def _():
        step(nj % 2, True, False)
        y = h_ref[...] + o_ref[...]
        o_ref[...] = _rms(y, gn_ref[...]) if final_norm else y


def peer_dense(xq, sx, uq, su, g, vq, sv, h, gain, *, final_norm, tm, te):
    m, d = xq.shape
    e = uq.shape[0]
    tm = min(tm, m)
    nj = e // te
    once = pl.Buffered(1)

    def cur(j):
        return jnp.minimum(j, nj - 1)

    def prev(j):
        return jnp.maximum(j - 1, 0)

    return pl.pallas_call(
        functools.partial(_peer_dense_kernel, final_norm=final_norm, n_blocks=nj),
        out_shape=jax.ShapeDtypeStruct((m, d), F32),
        grid=(m // tm, nj + 1),
        in_specs=[pl.BlockSpec((tm, d), lambda i, j: (i, 0)),
                  pl.BlockSpec((tm, 1), lambda i, j: (i, 0)),
                  pl.BlockSpec((te, d), lambda i, j: (cur(j), 0)),
                  pl.BlockSpec((1, te), lambda i, j: (0, cur(j))),
                  pl.BlockSpec((tm, te), lambda i, j: (i, cur(j))),
                  pl.BlockSpec((te, d), lambda i, j: (prev(j), 0)),
                  pl.BlockSpec((1, te), lambda i, j: (0, cur(j))),
                  pl.BlockSpec((tm, d), lambda i, j: (i, 0), pipeline_mode=once),
                  pl.BlockSpec((1, d), lambda i, j: (0, 0))],
        out_specs=pl.BlockSpec((tm, d), lambda i, j: (i, 0), pipeline_mode=once),
        scratch_shapes=[pltpu.VMEM((2, tm, te), F8), pltpu.VMEM((2, tm, 1), F32)],
        compiler_params=_cparams("parallel", "arbitrary"),
        name="peer_dense",
    )(xq, sx, uq, su.reshape(1, e), g, vq, sv.reshape(1, e), h, gain.reshape(1, d).astype(F32))


def _w_in_columns():
    src = {}
    off = 0
    for name, width in (("cq", MLA_Q_LORA), ("ckv", MLA_KV_LORA), ("kr", MLA_ROPE), ("gq", GLA_K_WIDTH),
                        ("gk", GLA_K_WIDTH), ("gv", GLA_V_WIDTH), ("lr", GLA_RANK), ("og", GLA_V_WIDTH)):
        src[name] = np.arange(off, off + width)
        off += width
    cols = np.full((IN_PAD,), -1, np.int64)
    for name, start in (("cq", COL_CQ), ("ckv", COL_CKV), ("gq", COL_GQ), ("gk", COL_GK),
                        ("gv", COL_GV), ("og", COL_OG), ("lr", COL_LR)):
        cols[start:start + len(src[name])] = src[name]
    half = MLA_ROPE // 2
    x1, x2 = src["kr"][:half], src["kr"][half:]
    cols[COL_KR:COL_KR + LANES] = np.concatenate([x1, x1, x2, x2])
    return cols


def _w_uq_columns():
    half = MLA_ROPE // 2
    nope = [h * MLA_QK + d for h in range(MLA_HEADS) for d in range(MLA_NOPE)]
    rope = []
    for j in range(MLA_HEADS // 2):
        a, b = 2 * j, 2 * j + 1
        for part in (0, 1):
            for h in (a, b):
                rope += [h * MLA_QK + MLA_NOPE + part * half + r for r in range(half)]
    return np.array(nope + rope, np.int64)


def _column_runs(cols):
    runs = []
    start = 0
    for i in range(1, len(cols) + 1):
        if i == len(cols) or cols[i] != (cols[i - 1] + 1 if cols[i - 1] >= 0 else -1):
            runs.append((int(cols[start]), i - start))
            start = i
    return runs


def _take_cols(w, cols, dtype):
    pieces = [jnp.zeros((w.shape[0], n), dtype) if src < 0 else w[:, src:src + n].astype(dtype)
              for src, n in _column_runs(cols)]
    return jnp.concatenate(pieces, axis=1)


def _take_rows(wt, rows, dtype):
    pieces = [jnp.zeros((n, wt.shape[1]), dtype) if src < 0 else wt[src:src + n].astype(dtype)
              for src, n in _column_runs(rows)]
    return jnp.concatenate(pieces, axis=0)


def _rope_tables(positions):
    half = MLA_ROPE // 2
    inv_freq = ROPE_THETA ** (-jnp.arange(0, MLA_ROPE, 2, dtype=F32) / MLA_ROPE)
    ang = positions.astype(F32).reshape(-1, 1) * inv_freq
    cos = jnp.cos(ang)
    sin = jnp.sin(ang)
    return jnp.tile(cos, (1, 4)), jnp.concatenate([-sin, -sin, sin, sin], axis=1)


def kernel(x, mem, positions, norm_mem, norm_mix, w_in, mla_q_norm, w_uq, mla_kv_norm, w_ukv, mla_out_norm,
           w_gate_up, b_gate, gla_out_norm, w_out, norm_cross, w_cq, w_ck, w_cv, w_co, norm_ffn, w_peer_q,
           peer_sub_keys, peer_u, peer_v, norm_final):
    batch, seq, d = x.shape
    mem_tokens = mem.shape[1]
    m = batch * seq
    h = x.reshape(m, d)
    cos_t, sin_t = _rope_tables(positions)

    mn_kv = norm_matmul(mem.reshape(batch * mem_tokens, d), norm_mem,
                        jnp.concatenate([w_ck[0], w_cv[0]], axis=1).astype(BF16),
                        k=d, tm=512, tn=1024, out_dtype=BF16, name="mem_kv_proj")

    for l in range(norm_mix.shape[0]):
        w_in_t = _take_rows(jnp.transpose(w_in[l]), _w_in_columns(), BF16)
        h_in = norm_matmul(h, norm_mix[l], w_in_t, k=d, tm=512, tn=1024, out_dtype=BF16, w_transposed=True,
                           name="in_proj")

        w_uq_p = _take_cols(w_uq[l], _w_uq_columns(), BF16)
        q_full = mla_q_proj(h_in, mla_q_norm[l], w_uq_p, cos_t, sin_t, tm=512)
        kv, k_rope = mla_kv_proj(h_in, mla_kv_norm[l], w_ukv[l].astype(BF16), cos_t, sin_t, tm=512)
        o_mla = mla_attention(q_full, kv, k_rope, batch=batch, seq=seq, blk=512)

        w_gate_pad = jnp.zeros((LANES, GLA_K_WIDTH), F32).at[:GLA_RANK].set(w_gate_up[l]).astype(BF16)
        o_gla = gla_mixer(h_in, w_gate_pad, b_gate[l], gla_out_norm[l], batch=batch, seq=seq, sb=1024)

        h = mixer_out_proj(o_mla, mla_out_norm[l], o_gla, w_out[l].astype(BF16), h, tm=512, tn=1024)

        qx = norm_matmul(h, norm_cross[l], w_cq[l].astype(BF16), k=d, tm=512, tn=1024, out_dtype=BF16,
                         name="xattn_q_proj")
        h = cross_attention(qx, mn_kv, w_co[l].astype(BF16), h, batch=batch, seq=seq,
                            mem_tokens=mem_tokens, tm=256)

        qp, xq, sx = norm_matmul(h, norm_ffn[l], w_peer_q[l].astype(BF16), k=d, tm=256, tn=2048, out_dtype=BF16,
                                 emit_xq=True, name="peer_q_proj")
        keys = peer_sub_keys[l].reshape(PEER_HEADS * 2, PEER_KEYS, PEER_HALF).astype(BF16)
        eid, gate = peer_topk(qp, keys, tm=256)
        g = peer_gates(eid, gate, tm=128)
        uq, su = quantize_rows(peer_u[l], tr=512, name="peer_u_quant")
        vq, sv = quantize_rows(peer_v[l], tr=512, name="peer_v_quant")
        h = peer_dense(xq, sx, uq, su, g, vq, sv, h, norm_final,
                       final_norm=(l + 1 == norm_mix.shape[0]), tm=512, te=1024)
    return h.reshape(batch, seq, d)
```

```python
import functools
import math

import numpy as np
import jax
import jax.numpy as jnp
from jax import lax
from jax.experimental import pallas as pl
from jax.experimental.pallas import tpu as pltpu

F32 = jnp.float32
BF16 = jnp.bfloat16
F8 = jnp.float8_e4m3fn
I32 = jnp.int32
FP8_AMAX = 256.0

EPS = 1e-6
ROPE_THETA = 10000.0

MLA_HEADS = 16
MLA_Q_LORA = 1024
MLA_KV_LORA = 512
MLA_NOPE = 128
MLA_ROPE = 64
MLA_QK = MLA_NOPE + MLA_ROPE
MLA_V = 128
MLA_WIDTH = MLA_HEADS * MLA_V

GLA_HEADS = 4
GLA_HEAD_K = 256
GLA_HEAD_V = 512
GLA_K_WIDTH = GLA_HEADS * GLA_HEAD_K
GLA_V_WIDTH = GLA_HEADS * GLA_HEAD_V
GLA_RANK = 16
GLA_TAU = 16.0
GLA_CHUNK = 64

XATTN_HEADS = 4
XATTN_DIM = 256
XATTN_WIDTH = XATTN_HEADS * XATTN_DIM

PEER_HEADS = 8
PEER_KEYS = 128
PEER_HALF = 128
PEER_TOPK = 16
PEER_SLOTS = PEER_HEADS * PEER_TOPK
PEER_EXPERTS = PEER_KEYS * PEER_KEYS

LANES = 128
VMEM_LIMIT = 56 * 1024 * 1024

COL_CQ = 0
COL_CKV = COL_CQ + MLA_Q_LORA
COL_KR = COL_CKV + MLA_KV_LORA
COL_LR = COL_KR + LANES
COL_GQ = 2048
COL_GK = COL_GQ + GLA_K_WIDTH
COL_GV = COL_GK + GLA_K_WIDTH
COL_OG = COL_GV + GLA_V_WIDTH
IN_PAD = COL_OG + GLA_V_WIDTH

NEG = -1e30


def _cparams(*sem):
    return pltpu.CompilerParams(dimension_semantics=sem, vmem_limit_bytes=VMEM_LIMIT)


def _rms(x, g):
    ms = jnp.mean(x * x, axis=-1, keepdims=True)
    return x * lax.rsqrt(ms + EPS) * g


def _dot(a, b):
    return jnp.dot(a, b, preferred_element_type=F32)


def _dot_nt(a, b):
    return lax.dot_general(a, b, (((1,), (1,)), ((), ())), preferred_element_type=F32)


def _dot_tn(a, b):
    return lax.dot_general(a, b, (((0,), (0,)), ((), ())), preferred_element_type=F32)


def _quantize_rows(x):
    amax = jnp.max(jnp.abs(x), axis=-1, keepdims=True)
    scale = jnp.where(amax > 0.0, amax * (1.0 / FP8_AMAX), 1.0)
    return (x * (1.0 / scale)).astype(F8), scale


def _norm_matmul_kernel(x_ref, g_ref, w_ref, o_ref, *rest, emit_xq, whole_n, w_transposed):
    if emit_xq:
        xq_ref, sx_ref, xn_ref = rest
    else:
        (xn_ref,) = rest

    def normalise():
        xn = _rms(x_ref[...].astype(F32), g_ref[...])
        xn_ref[...] = xn.astype(BF16)
        if emit_xq:
            xq_ref[...], sx_ref[...] = _quantize_rows(xn)

    if whole_n:
        normalise()
    else:
        pl.when(pl.program_id(1) == 0)(normalise)
    mm = _dot_nt if w_transposed else _dot
    o_ref[...] = mm(xn_ref[...], w_ref[...]).astype(o_ref.dtype)


def norm_matmul(x, gain, w, *, k, x_col_blk=0, tm, tn, out_dtype, emit_xq=False, w_transposed=False, name):
    m = x.shape[0]
    n = w.shape[0] if w_transposed else w.shape[1]
    tm = min(tm, m)
    tn = min(tn, n)
    out_shape = [jax.ShapeDtypeStruct((m, n), out_dtype)]
    out_specs = [pl.BlockSpec((tm, tn), lambda i, j: (i, j))]
    if emit_xq:
        out_shape += [jax.ShapeDtypeStruct((m, k), F8), jax.ShapeDtypeStruct((m, 1), F32)]
        out_specs += [pl.BlockSpec((tm, k), lambda i, j: (i, 0)), pl.BlockSpec((tm, 1), lambda i, j: (i, 0))]
    whole_n = tn == n
    w_mode = pl.Buffered(1) if whole_n else None
    w_spec = (pl.BlockSpec((tn, k), lambda i, j: (j, 0), pipeline_mode=w_mode) if w_transposed
              else pl.BlockSpec((k, tn), lambda i, j: (0, j), pipeline_mode=w_mode))
    res = pl.pallas_call(
        functools.partial(_norm_matmul_kernel, emit_xq=emit_xq, whole_n=whole_n, w_transposed=w_transposed),
        out_shape=out_shape,
        grid=(m // tm, n // tn),
        in_specs=[pl.BlockSpec((tm, k), lambda i, j: (i, x_col_blk)),
                  pl.BlockSpec((1, k), lambda i, j: (0, 0)),
                  w_spec],
        out_specs=out_specs,
        scratch_shapes=[pltpu.VMEM((tm, k), BF16)],
        compiler_params=_cparams("parallel", "arbitrary"),
        name=name,
    )(x, gain.reshape(1, k).astype(F32), w)
    return res if emit_xq else res[0]


def _quantize_kernel(w_ref, q_ref, s_ref):
    q_ref[...], s_ref[...] = _quantize_rows(w_ref[...])


def quantize_rows(w, *, tr, name):
    r, d = w.shape
    return pl.pallas_call(
        _quantize_kernel,
        out_shape=[jax.ShapeDtypeStruct((r, d), F8), jax.ShapeDtypeStruct((r, 1), F32)],
        grid=(r // tr,),
        in_specs=[pl.BlockSpec((tr, d), lambda i: (i, 0))],
        out_specs=[pl.BlockSpec((tr, d), lambda i: (i, 0)), pl.BlockSpec((tr, 1), lambda i: (i, 0))],
        compiler_params=_cparams("parallel"),
        name=name,
    )(w)


def _rope_pair(r, cosv, sinv):
    return r * cosv + pltpu.roll(r, 2 * 32, axis=1) * sinv


def _qproj_kernel(c_ref, g_ref, w_ref, cos_ref, sin_ref, o_ref, *, scale):
    xn = _rms(c_ref[...].astype(F32), g_ref[...]).astype(BF16)
    q = _dot(xn, w_ref[...])
    cosv = cos_ref[...]
    sinv = sin_ref[...]
    lane = lax.broadcasted_iota(I32, (1, LANES), 1)
    rope_base = MLA_HEADS * MLA_NOPE
    for j in range(MLA_HEADS // 2):
        r = q[:, rope_base + j * LANES: rope_base + (j + 1) * LANES]
        r = _rope_pair(r, cosv, sinv) * scale
        for p in range(2):
            h = 2 * j + p
            own = ((lane // 32) % 2) == p
            o_ref[:, h * 256: h * 256 + 128] = (q[:, h * 128:(h + 1) * 128] * scale).astype(o_ref.dtype)
            o_ref[:, h * 256 + 128: h * 256 + 256] = jnp.where(own, r, 0.0).astype(o_ref.dtype)


def mla_q_proj(h_in, gain, w_uq_perm, cos_t, sin_t, *, tm):
    m = h_in.shape[0]
    tm = min(tm, m)
    return pl.pallas_call(
        functools.partial(_qproj_kernel, scale=MLA_QK ** -0.5 * math.log2(math.e)),
        out_shape=jax.ShapeDtypeStruct((m, MLA_HEADS * 256), BF16),
        grid=(m // tm,),
        in_specs=[pl.BlockSpec((tm, MLA_Q_LORA), lambda i: (i, COL_CQ // MLA_Q_LORA)),
                  pl.BlockSpec((1, MLA_Q_LORA), lambda i: (0, 0)),
                  pl.BlockSpec(w_uq_perm.shape, lambda i: (0, 0)),
                  pl.BlockSpec((tm, LANES), lambda i: (i, 0)),
                  pl.BlockSpec((tm, LANES), lambda i: (i, 0))],
        out_specs=pl.BlockSpec((tm, MLA_HEADS * 256), lambda i: (i, 0)),
        compiler_params=_cparams("parallel"),
        name="mla_q_proj",
    )(h_in, gain.reshape(1, -1).astype(F32), w_uq_perm, cos_t, sin_t)


def _kvproj_kernel(c_ref, g_ref, w_ref, kr_ref, cos_ref, sin_ref, kv_ref, kro_ref):
    xn = _rms(c_ref[...].astype(F32), g_ref[...]).astype(BF16)
    kv_ref[...] = _dot(xn, w_ref[...]).astype(kv_ref.dtype)
    kro_ref[...] = _rope_pair(kr_ref[...].astype(F32), cos_ref[...], sin_ref[...]).astype(kro_ref.dtype)


def mla_kv_proj(h_in, gain, w_ukv, cos_t, sin_t, *, tm):
    m = h_in.shape[0]
    tm = min(tm, m)
    n = w_ukv.shape[1]
    return pl.pallas_call(
        _kvproj_kernel,
        out_shape=[jax.ShapeDtypeStruct((m, n), BF16), jax.ShapeDtypeStruct((m, LANES), BF16)],
        grid=(m // tm,),
        in_specs=[pl.BlockSpec((tm, MLA_KV_LORA), lambda i: (i, COL_CKV // MLA_KV_LORA)),
                  pl.BlockSpec((1, MLA_KV_LORA), lambda i: (0, 0)),
                  pl.BlockSpec(w_ukv.shape, lambda i: (0, 0)),
                  pl.BlockSpec((tm, LANES), lambda i: (i, COL_KR // LANES)),
                  pl.BlockSpec((tm, LANES), lambda i: (i, 0)),
                  pl.BlockSpec((tm, LANES), lambda i: (i, 0))],
        out_specs=[pl.BlockSpec((tm, n), lambda i: (i, 0)),
                   pl.BlockSpec((tm, LANES), lambda i: (i, 0))],
        compiler_params=_cparams("parallel"),
        name="mla_kv_proj",
    )(h_in, gain.reshape(1, -1).astype(F32), w_ukv, h_in, cos_t, sin_t)


ATTN_UNROLL = 8


def _mla_attn_kernel(qi_tab, kj_tab, q_ref, kv_ref, kr_ref, o_ref, s0_scr, s1_scr, m_scr, acc_scr,
                     *, blk, nq, n_items):
    ones = jnp.ones((blk, LANES), BF16)
    s_scr = (s0_scr, s1_scr)

    def scores(t, s_ref):
        q_rows = pl.ds(pl.multiple_of(qi_tab[t] * blk, blk), blk)
        k_rows = pl.ds(pl.multiple_of(kj_tab[t] * blk, blk), blk)
        k = jnp.concatenate([kv_ref[k_rows, :MLA_NOPE], kr_ref[k_rows, :]], axis=1)
        s_ref[...] = _dot_nt(q_ref[q_rows, :], k)

    def finish(t, s_ref, diagonal):
        qi = qi_tab[t]
        k_rows = pl.ds(pl.multiple_of(kj_tab[t] * blk, blk), blk)
        v = jnp.concatenate([kv_ref[k_rows, MLA_NOPE:], ones], axis=1)
        s = s_ref[...]
        if diagonal:
            row = lax.broadcasted_iota(I32, (blk, blk), 0)
            col = lax.broadcasted_iota(I32, (blk, blk), 1)
            s = jnp.where(col <= row, s, NEG)
            m_new = jnp.max(s, axis=1, keepdims=True) + jnp.zeros((blk, LANES), F32)
        else:
            m = m_scr[qi]
            m_new = jnp.maximum(m, jnp.max(s, axis=1, keepdims=True))
        p = jnp.exp2(s - jnp.concatenate([m_new] * (blk // LANES), axis=1)).astype(BF16)
        pv = _dot(p, v)
        if diagonal:
            acc_scr[qi] = pv
        else:
            alpha = jnp.exp2(m - m_new)
            acc_scr[qi] = jnp.concatenate([alpha, alpha], axis=1) * acc_scr[qi] + pv
        m_scr[qi] = m_new

    def run(t0, n, slot, diagonal):
        for u in range(n):
            scores(jnp.minimum(t0 + u + 1, n_items - 1), s_scr[(slot + u + 1) % 2])
            finish(t0 + u, s_scr[(slot + u) % 2], diagonal)

    def phase(t0, n, slot, diagonal):
        def body(i, carry):
            run(t0 + i * ATTN_UNROLL, ATTN_UNROLL, slot, diagonal)
            return carry

        lax.fori_loop(0, n // ATTN_UNROLL, body, 0)
        if n % ATTN_UNROLL:
            run(t0 + n - n % ATTN_UNROLL, n % ATTN_UNROLL, slot, diagonal)
        return (slot + n) % 2

    scores(0, s0_scr)
    slot = phase(0, nq, 0, True)
    phase(nq, n_items - nq, slot, False)

    def normalise(qi, carry):
        acc = acc_scr[qi]
        o_ref[pl.ds(pl.multiple_of(qi * blk, blk), blk), :] = (acc[:, :MLA_V] / acc[:, MLA_V:]).astype(o_ref.dtype)
        return carry

    lax.fori_loop(0, nq, normalise, 0)


def mla_attention(q_full, kv, k_rope, *, batch, seq, blk):
    blk = min(blk, seq)
    nq = seq // blk
    m = batch * seq
    pairs = [(i, i) for i in range(nq)] + [(i, i - d) for d in range(1, nq) for i in range(d, nq)]
    qi_tab = jnp.asarray([p[0] for p in pairs], I32)
    kj_tab = jnp.asarray([p[1] for p in pairs], I32)
    grid_spec = pltpu.PrefetchScalarGridSpec(
        num_scalar_prefetch=2,
        grid=(batch, MLA_HEADS),
        in_specs=[pl.BlockSpec((seq, 256), lambda b, h, *_: (b, h)),
                  pl.BlockSpec((seq, 256), lambda b, h, *_: (b, h)),
                  pl.BlockSpec((seq, LANES), lambda b, h, *_: (b, 0))],
        out_specs=pl.BlockSpec((seq, MLA_V), lambda b, h, *_: (b, h)),
        scratch_shapes=[pltpu.VMEM((blk, blk), F32), pltpu.VMEM((blk, blk), F32),
                        pltpu.VMEM((nq, blk, LANES), F32), pltpu.VMEM((nq, blk, 2 * MLA_V), F32)],
    )
    return pl.pallas_call(
        functools.partial(_mla_attn_kernel, blk=blk, nq=nq, n_items=len(pairs)),
        out_shape=jax.ShapeDtypeStruct((m, MLA_WIDTH), BF16),
        grid_spec=grid_spec,
        compiler_params=_cparams("parallel", "parallel"),
        name="mla_attention",
    )(qi_tab, kj_tab, q_full, kv, k_rope)


GLA_HEADS_PER_STEP = 1


def _gla_kernel(q_ref, k_ref, v_ref, og_ref, lr_ref, wg_ref, bg_ref, gn_ref, o_ref, st_ref, *, sb, c):
    @pl.when(pl.program_id(2) == 0)
    def _():
        st_ref[...] = jnp.zeros_like(st_ref)

    z = _dot(lr_ref[...], wg_ref[...]) + bg_ref[...]
    log_a = (jnp.minimum(z, 0.0) - jnp.log(1.0 + jnp.exp(-jnp.abs(z)))) * (1.0 / GLA_TAU)
    rows = lax.broadcasted_iota(I32, (c, c), 0)
    cols = lax.broadcasted_iota(I32, (c, c), 1)
    causal = cols <= rows
    tril = causal.astype(F32)
    gn = gn_ref[...]
    scale = GLA_HEAD_K ** -0.5

    for n in range(sb // c):
        sl = slice(n * c, (n + 1) * c)
        for hh in range(GLA_HEADS_PER_STEP):
            kcols = slice(hh * GLA_HEAD_K, (hh + 1) * GLA_HEAD_K)
            vcols = slice(hh * GLA_HEAD_V, (hh + 1) * GLA_HEAD_V)
            g = log_a[sl, kcols]
            b = jnp.dot(tril, g, preferred_element_type=F32, precision=lax.Precision.HIGHEST)
            b_last = b[c - 1:c, :]
            q = q_ref[sl, kcols].astype(F32)
            k = k_ref[sl, kcols].astype(F32)
            v = v_ref[sl, vcols]
            qe = (q * scale * jnp.exp(b)).astype(BF16)
            ke = (k * jnp.exp(-b)).astype(BF16)
            kd = (k * jnp.exp(b_last - b)).astype(BF16)
            att = jnp.where(causal, _dot_nt(qe, ke), 0.0).astype(BF16)
            st = st_ref[hh]
            o = _dot(att, v) + _dot_nt(qe, st.astype(BF16))
            st_ref[hh] = st * jnp.exp(b_last) + _dot_tn(v, kd)
            og = og_ref[sl, vcols].astype(F32)
            o = _rms(o, gn) * (og * (1.0 / (1.0 + jnp.exp(-og))))
            o_ref[sl, vcols] = o.astype(o_ref.dtype)


def gla_mixer(h_in, w_gate_pad, b_gate, gla_out_norm, *, batch, seq, sb):
    sb = min(sb, seq)
    nsb = seq // sb
    m = batch * seq
    g = GLA_HEADS_PER_STEP
    kw, vw = g * GLA_HEAD_K, g * GLA_HEAD_V
    return pl.pallas_call(
        functools.partial(_gla_kernel, sb=sb, c=GLA_CHUNK),
        out_shape=jax.ShapeDtypeStruct((m, GLA_V_WIDTH), BF16),
        grid=(batch, GLA_HEADS // g, nsb),
        in_specs=[pl.BlockSpec((sb, kw), lambda b, h, s: (b * nsb + s, COL_GQ // kw + h)),
                  pl.BlockSpec((sb, kw), lambda b, h, s: (b * nsb + s, COL_GK // kw + h)),
                  pl.BlockSpec((sb, vw), lambda b, h, s: (b * nsb + s, COL_GV // vw + h)),
                  pl.BlockSpec((sb, vw), lambda b, h, s: (b * nsb + s, COL_OG // vw + h)),
                  pl.BlockSpec((sb, LANES), lambda b, h, s: (b * nsb + s, COL_LR // LANES)),
                  pl.BlockSpec((LANES, kw), lambda b, h, s: (0, h)),
                  pl.BlockSpec((1, kw), lambda b, h, s: (0, h)),
                  pl.BlockSpec((1, GLA_HEAD_V), lambda b, h, s: (0, 0))],
        out_specs=pl.BlockSpec((sb, vw), lambda b, h, s: (b * nsb + s, h)),
        scratch_shapes=[pltpu.VMEM((g, GLA_HEAD_V, GLA_HEAD_K), F32)],
        compiler_params=_cparams("parallel", "parallel", "arbitrary"),
        name="gla_mixer",
    )(h_in, h_in, h_in, h_in, h_in, w_gate_pad, b_gate.reshape(1, -1).astype(F32),
      gla_out_norm.reshape(1, -1).astype(F32))


def _out_proj_kernel(om_ref, g_ref, og_ref, w_ref, x_ref, o_ref, mix_ref):
    @pl.when(pl.program_id(1) == 0)
    def _():
        mix_ref[:, :MLA_WIDTH] = _rms(om_ref[...].astype(F32), g_ref[...]).astype(BF16)
        mix_ref[:, MLA_WIDTH:] = og_ref[...]

    o_ref[...] = x_ref[...] + _dot(mix_ref[...], w_ref[...])


def mixer_out_proj(o_mla, mla_out_norm, o_gla, w_out, x, *, tm, tn):
    m, d = x.shape
    tm = min(tm, m)
    tn = min(tn, d)
    kmix = MLA_WIDTH + GLA_V_WIDTH
    return pl.pallas_call(
        _out_proj_kernel,
        out_shape=jax.ShapeDtypeStruct((m, d), F32),
        grid=(m // tm, d // tn),
        in_specs=[pl.BlockSpec((tm, MLA_WIDTH), lambda i, j: (i, 0)),
                  pl.BlockSpec((1, MLA_WIDTH), lambda i, j: (0, 0)),
                  pl.BlockSpec((tm, GLA_V_WIDTH), lambda i, j: (i, 0)),
                  pl.BlockSpec((kmix, tn), lambda i, j: (0, j)),
                  pl.BlockSpec((tm, tn), lambda i, j: (i, j))],
        out_specs=pl.BlockSpec((tm, tn), lambda i, j: (i, j)),
        scratch_shapes=[pltpu.VMEM((tm, kmix), BF16)],
        compiler_params=_cparams("parallel", "arbitrary"),
        name="mixer_out_proj",
    )(o_mla, mla_out_norm.reshape(1, -1).astype(F32), o_gla, w_out, x)


def _xattn_kernel(q_ref, kv_ref, w_ref, h_ref, o_ref):
    scale = XATTN_DIM ** -0.5
    outs = []
    for h in range(XATTN_HEADS):
        q = q_ref[:, h * XATTN_DIM:(h + 1) * XATTN_DIM]
        k = kv_ref[:, h * XATTN_DIM:(h + 1) * XATTN_DIM]
        v = kv_ref[:, XATTN_WIDTH + h * XATTN_DIM: XATTN_WIDTH + (h + 1) * XATTN_DIM]
        s = _dot_nt(q, k) * scale
        p = jnp.exp(s - jnp.max(s, axis=1, keepdims=True))
        p = p / jnp.sum(p, axis=1, keepdims=True)
        outs.append(_dot(p.astype(BF16), v).astype(BF16))
    o = jnp.concatenate(outs, axis=1)
    o_ref[...] = h_ref[...] + _dot(o, w_ref[...])


def cross_attention(qx, kvm, w_co, h1, *, batch, seq, mem_tokens, tm):
    m, d = h1.shape
    tm = min(tm, seq)
    per_b = seq // tm
    return pl.pallas_call(
        _xattn_kernel,
        out_shape=jax.ShapeDtypeStruct((m, d), F32),
        grid=(m // tm,),
        in_specs=[pl.BlockSpec((tm, XATTN_WIDTH), lambda i: (i, 0)),
                  pl.BlockSpec((mem_tokens, 2 * XATTN_WIDTH), lambda i: (i // per_b, 0)),
                  pl.BlockSpec(w_co.shape, lambda i: (0, 0), pipeline_mode=pl.Buffered(1)),
                  pl.BlockSpec((tm, d), lambda i: (i, 0))],
        out_specs=pl.BlockSpec((tm, d), lambda i: (i, 0)),
        compiler_params=_cparams("parallel"),
        name="cross_attention",
    )(qx, kvm, w_co, h1)


def _order_key(x):
    b = lax.bitcast_convert_type(x, I32)
    return b ^ (lax.shift_right_arithmetic(b, 31) & 0x7FFFFFFF)


def _order_key_inv(k):
    return lax.bitcast_convert_type(k ^ (lax.shift_right_arithmetic(k, 31) & 0x7FFFFFFF), F32)


KEY_BIAS = 0x20000000


def _pack_keys(x, bits):
    low = (1 << bits) - 1
    row = lax.broadcasted_iota(I32, x.shape, 0)
    k = (_order_key(x) & ~((low << 2) | 3)) | ((low - row) << 2)
    return lax.bitcast_convert_type(lax.shift_right_arithmetic(k, 2) + KEY_BIAS, F32)


def _unpack_keys(kf, bits):
    low = (1 << bits) - 1
    k = lax.shift_left(lax.bitcast_convert_type(kf, I32) - KEY_BIAS, 2)
    return _order_key_inv(k & ~((low << 2) | 3)), low - (lax.shift_right_arithmetic(k, 2) & low)


SUBLANES = 8


def _max_rows(k3):
    m8 = jnp.max(k3, axis=0)
    for shift in (4, 2, 1):
        m8 = jnp.maximum(m8, pltpu.roll(m8, shift, axis=0))
    return m8


def _top16_rows(s, bits):
    rows, t = s.shape
    k3 = _pack_keys(s, bits).reshape(rows // SUBLANES, SUBLANES, t)
    vals, idxs = [], []
    for _ in range(PEER_TOPK):
        m8 = _max_rows(k3)
        val, idx = _unpack_keys(m8[0:1, :], bits)
        vals.append(val)
        idxs.append(idx)
        k3 = jnp.where(k3 == m8[None], 0.0, k3)
    return vals, idxs


_PAIRS = [(a, b) for a in range(PEER_TOPK) for b in range(PEER_TOPK) if (a + 1) * (b + 1) <= PEER_TOPK]
_PAIR_ROWS = -(-len(_PAIRS) // 8) * 8


TOPK_HEADS_PER_TRIP = 2


def _peer_topk_kernel(q_ref, keys_ref, eid_ref, gate_ref, eid_t, gate_t, *, tm):
    def head(h):
        tops = []
        for p in range(2):
            col = pl.multiple_of((2 * h + p) * PEER_HALF, PEER_HALF)
            qh = q_ref[:, pl.ds(col, PEER_HALF)]
            s = _dot_nt(keys_ref[2 * h + p], qh)
            tops.append(_top16_rows(s, 7))
        (s1, i1), (s2, i2) = tops
        cand_s = [s1[a] + s2[b] for a, b in _PAIRS]
        cand_e = [i1[a] * PEER_KEYS + i2[b] for a, b in _PAIRS]
        pad = _PAIR_ROWS - len(_PAIRS)
        cs = jnp.concatenate(cand_s + [jnp.full((pad, tm), NEG, F32)], axis=0)
        ce = jnp.concatenate(cand_e + [jnp.zeros((pad, tm), I32)], axis=0)
        k3 = _pack_keys(cs, 6).reshape(_PAIR_ROWS // SUBLANES, SUBLANES, tm)
        ce3 = ce.astype(F32).reshape(_PAIR_ROWS // SUBLANES, SUBLANES, tm)
        sel_s, sel_e = [], []
        for _ in range(PEER_TOPK):
            m8 = _max_rows(k3)
            hit = k3 == m8[None]
            sel_s.append(_unpack_keys(m8[0:1, :], 6)[0])
            sel_e.append(_max_rows(jnp.where(hit, ce3, -1.0))[0:1, :].astype(I32))
            k3 = jnp.where(hit, 0.0, k3)
        top_s = jnp.concatenate(sel_s, axis=0)
        top_e = jnp.concatenate(sel_e, axis=0)
        w = jnp.exp(top_s - top_s[0:1, :])
        gate = w / jnp.sum(w, axis=0, keepdims=True)
        row = pl.multiple_of(h * PEER_TOPK, PEER_TOPK)
        eid_t[pl.ds(row, PEER_TOPK), :] = top_e
        gate_t[pl.ds(row, PEER_TOPK), :] = gate

    def trip(i, carry):
        for u in range(TOPK_HEADS_PER_TRIP):
            head(i * TOPK_HEADS_PER_TRIP + u)
        return carry

    lax.fori_loop(0, PEER_HEADS // TOPK_HEADS_PER_TRIP, trip, 0)
    eid_ref[...] = eid_t[...].T
    gate_ref[...] = gate_t[...].T


def peer_topk(qp, keys, *, tm):
    m = qp.shape[0]
    tm = min(tm, m)
    return pl.pallas_call(
        functools.partial(_peer_topk_kernel, tm=tm),
        out_shape=[jax.ShapeDtypeStruct((m, PEER_SLOTS), I32), jax.ShapeDtypeStruct((m, PEER_SLOTS), F32)],
        grid=(m // tm,),
        in_specs=[pl.BlockSpec((tm, qp.shape[1]), lambda i: (i, 0)),
                  pl.BlockSpec(keys.shape, lambda i: (0, 0, 0))],
        out_specs=[pl.BlockSpec((tm, PEER_SLOTS), lambda i: (i, 0)),
                   pl.BlockSpec((tm, PEER_SLOTS), lambda i: (i, 0))],
        scratch_shapes=[pltpu.VMEM((PEER_SLOTS, tm), I32), pltpu.VMEM((PEER_SLOTS, tm), F32)],
        compiler_params=_cparams("parallel"),
        name="peer_topk",
    )(qp, keys)


GATE_UNROLL = 32


def _peer_gates_kernel(eid_ref, gate_ref, o_ref, g3_ref, *, tm, pitch):
    sub = lax.broadcasted_iota(I32, (PEER_KEYS, PEER_SLOTS), 0)

    def token(t, carry):
        e = eid_ref[pl.ds(t, 1), :]
        g = gate_ref[pl.ds(t, 1), :]
        hi = lax.shift_right_logical(e, 7)
        lo = jnp.bitwise_and(e, PEER_KEYS - 1)
        a1 = jnp.where(sub == hi, g, 0.0).astype(BF16)
        a2 = jnp.where(sub == lo, 1.0, 0.0).astype(BF16)
        g3_ref[pl.ds(t, PEER_KEYS, stride=pitch), :] = _dot_nt(a1, a2)
        return carry

    lax.fori_loop(0, tm, token, 0, unroll=GATE_UNROLL)
    for e1 in range(PEER_KEYS):
        o_ref[:, e1 * PEER_KEYS:(e1 + 1) * PEER_KEYS] = g3_ref[e1 * pitch:e1 * pitch + tm, :].astype(o_ref.dtype)


def peer_gates(eid, gate, *, tm):
    m = eid.shape[0]
    tm = min(tm, m)
    pitch = tm + 8
    return pl.pallas_call(
        functools.partial(_peer_gates_kernel, tm=tm, pitch=pitch),
        out_shape=jax.ShapeDtypeStruct((m, PEER_EXPERTS), BF16),
        grid=(m // tm,),
        in_specs=[pl.BlockSpec((tm, PEER_SLOTS), lambda i: (i, 0)),
                  pl.BlockSpec((tm, PEER_SLOTS), lambda i: (i, 0))],
        out_specs=pl.BlockSpec((tm, PEER_EXPERTS), lambda i: (i, 0)),
        scratch_shapes=[pltpu.VMEM((PEER_KEYS * pitch, PEER_KEYS), F32)],
        compiler_params=_cparams("parallel"),
        name="peer_gates",
    )(eid, gate)


def _peer_dense_kernel(x_ref, sx_ref, u_ref, su_ref, g_ref, v_ref, sv_ref, h_ref, gn_ref, o_ref, aq_scr, sa_scr,
                       *, final_norm, n_blocks):
    j = pl.program_id(1)
    nj = n_blocks

    def step(p, do_values, do_gates):
        if do_gates:
            s = _dot_nt(x_ref[...], u_ref[...])
        if do_values:
            pv = _dot(aq_scr[1 - p], v_ref[...])
        if do_gates:
            s = s * sx_ref[...] * su_ref[...]
            a = 0.5 * s * (1.0 + lax.erf(s * (2.0 ** -0.5)))
            aq_scr[p], sa_scr[p] = _quantize_rows(a * g_ref[...].astype(F32) * sv_ref[...])
        if do_values:
            o_ref[...] += sa_scr[1 - p] * pv

    half = o_ref.shape[1] // 2

    @pl.when(j == 0)
    def _():
        o_ref[:, :half] = h_ref[...]
        o_ref[:, half:] = jnp.zeros((o_ref.shape[0], half), F32)
        step(0, False, True)

    @pl.when(j == 1)
    def _():
        o_ref[:, half:] += h_ref[...]

    for p in range(2):
        @pl.when(jnp.logical_and(jnp.logical_and(j > 0, j < nj), j % 2 == p))
        def _(p=p):
            step(p, True, True)

    @pl.when(j == nj)
    def _():
        step(nj % 2, True, False)
        if final_norm:
            o_ref[...] = _rms(o_ref[...], gn_ref[...])


def peer_dense(xq, sx, uq, su, g, vq, sv, h, gain, *, final_norm, tm, te):
    m, d = xq.shape
    e = uq.shape[0]
    tm = min(tm, m)
    nj = e // te
    def cur(j):
        return jnp.minimum(j, nj - 1)

    def prev(j):
        return jnp.maximum(j - 1, 0)

    return pl.pallas_call(
        functools.partial(_peer_dense_kernel, final_norm=final_norm, n_blocks=nj),
        out_shape=jax.ShapeDtypeStruct((m, d), F32),
        grid=(m // tm, nj + 1),
        in_specs=[pl.BlockSpec((tm, d), lambda i, j: (i, 0)),
                  pl.BlockSpec((tm, 1), lambda i, j: (i, 0)),
                  pl.BlockSpec((te, d), lambda i, j: (cur(j), 0)),
                  pl.BlockSpec((1, te), lambda i, j: (0, cur(j))),
                  pl.BlockSpec((tm, te), lambda i, j: (i, cur(j))),
                  pl.BlockSpec((te, d), lambda i, j: (prev(j), 0)),
                  pl.BlockSpec((1, te), lambda i, j: (0, cur(j))),
                  pl.BlockSpec((tm, d // 2), lambda i, j: (i, jnp.minimum(j, 1))),
                  pl.BlockSpec((1, d), lambda i, j: (0, 0))],
        out_specs=pl.BlockSpec((tm, d), lambda i, j: (i, 0)),
        scratch_shapes=[pltpu.VMEM((2, tm, te), F8), pltpu.VMEM((2, tm, 1), F32)],
        compiler_params=_cparams("parallel", "arbitrary"),
        name="peer_dense",
    )(xq, sx, uq, su.reshape(1, e), g, vq, sv.reshape(1, e), h, gain.reshape(1, d).astype(F32))


def _w_in_columns():
    src = {}
    off = 0
    for name, width in (("cq", MLA_Q_LORA), ("ckv", MLA_KV_LORA), ("kr", MLA_ROPE), ("gq", GLA_K_WIDTH),
                        ("gk", GLA_K_WIDTH), ("gv", GLA_V_WIDTH), ("lr", GLA_RANK), ("og", GLA_V_WIDTH)):
        src[name] = np.arange(off, off + width)
        off += width
    cols = np.full((IN_PAD,), -1, np.int64)
    for name, start in (("cq", COL_CQ), ("ckv", COL_CKV), ("gq", COL_GQ), ("gk", COL_GK),
                        ("gv", COL_GV), ("og", COL_OG), ("lr", COL_LR)):
        cols[start:start + len(src[name])] = src[name]
    half = MLA_ROPE // 2
    x1, x2 = src["kr"][:half], src["kr"][half:]
    cols[COL_KR:COL_KR + LANES] = np.concatenate([x1, x1, x2, x2])
    return cols


def _w_uq_columns():
    half = MLA_ROPE // 2
    nope = [h * MLA_QK + d for h in range(MLA_HEADS) for d in range(MLA_NOPE)]
    rope = []
    for j in range(MLA_HEADS // 2):
        a, b = 2 * j, 2 * j + 1
        for part in (0, 1):
            for h in (a, b):
                rope += [h * MLA_QK + MLA_NOPE + part * half + r for r in range(half)]
    return np.array(nope + rope, np.int64)


def _column_runs(cols):
    runs = []
    start = 0
    for i in range(1, len(cols) + 1):
        if i == len(cols) or cols[i] != (cols[i - 1] + 1 if cols[i - 1] >= 0 else -1):
            runs.append((int(cols[start]), i - start))
            start = i
    return runs


def _take_cols(w, cols, dtype):
    pieces = [jnp.zeros((w.shape[0], n), dtype) if src < 0 else w[:, src:src + n].astype(dtype)
              for src, n in _column_runs(cols)]
    return jnp.concatenate(pieces, axis=1)


def _take_rows(wt, rows, dtype):
    pieces = [jnp.zeros((n, wt.shape[1]), dtype) if src < 0 else wt[src:src + n].astype(dtype)
              for src, n in _column_runs(rows)]
    return jnp.concatenate(pieces, axis=0)


def _rope_tables(positions):
    half = MLA_ROPE // 2
    inv_freq = ROPE_THETA ** (-jnp.arange(0, MLA_ROPE, 2, dtype=F32) / MLA_ROPE)
    ang = positions.astype(F32).reshape(-1, 1) * inv_freq
    cos = jnp.cos(ang)
    sin = jnp.sin(ang)
    return jnp.tile(cos, (1, 4)), jnp.concatenate([-sin, -sin, sin, sin], axis=1)


def kernel(x, mem, positions, norm_mem, norm_mix, w_in, mla_q_norm, w_uq, mla_kv_norm, w_ukv, mla_out_norm,
           w_gate_up, b_gate, gla_out_norm, w_out, norm_cross, w_cq, w_ck, w_cv, w_co, norm_ffn, w_peer_q,
           peer_sub_keys, peer_u, peer_v, norm_final):
    batch, seq, d = x.shape
    mem_tokens = mem.shape[1]
    m = batch * seq
    h = x.reshape(m, d)
    cos_t, sin_t = _rope_tables(positions)

    mn_kv = norm_matmul(mem.reshape(batch * mem_tokens, d), norm_mem,
                        jnp.concatenate([w_ck[0], w_cv[0]], axis=1).astype(BF16),
                        k=d, tm=512, tn=1024, out_dtype=BF16, name="mem_kv_proj")

    for l in range(norm_mix.shape[0]):
        w_in_t = _take_rows(jnp.transpose(w_in[l]), _w_in_columns(), BF16)
        h_in = norm_matmul(h, norm_mix[l], w_in_t, k=d, tm=512, tn=1024, out_dtype=BF16, w_transposed=True,
                           name="in_proj")

        w_uq_p = _take_cols(w_uq[l], _w_uq_columns(), BF16)
        q_full = mla_q_proj(h_in, mla_q_norm[l], w_uq_p, cos_t, sin_t, tm=512)
        kv, k_rope = mla_kv_proj(h_in, mla_kv_norm[l], w_ukv[l].astype(BF16), cos_t, sin_t, tm=512)
        o_mla = mla_attention(q_full, kv, k_rope, batch=batch, seq=seq, blk=512)

        w_gate_pad = jnp.zeros((LANES, GLA_K_WIDTH), F32).at[:GLA_RANK].set(w_gate_up[l]).astype(BF16)
        o_gla = gla_mixer(h_in, w_gate_pad, b_gate[l], gla_out_norm[l], batch=batch, seq=seq, sb=1024)

        h = mixer_out_proj(o_mla, mla_out_norm[l], o_gla, w_out[l].astype(BF16), h, tm=512, tn=1024)

        qx = norm_matmul(h, norm_cross[l], w_cq[l].astype(BF16), k=d, tm=512, tn=1024, out_dtype=BF16,
                         name="xattn_q_proj")
        h = cross_attention(qx, mn_kv, w_co[l].astype(BF16), h, batch=batch, seq=seq,
                            mem_tokens=mem_tokens, tm=512)

        qp, xq, sx = norm_matmul(h, norm_ffn[l], w_peer_q[l].astype(BF16), k=d, tm=256, tn=2048, out_dtype=BF16,
                                 emit_xq=True, name="peer_q_proj")
        keys = peer_sub_keys[l].reshape(PEER_HEADS * 2, PEER_KEYS, PEER_HALF).astype(BF16)
        eid, gate = peer_topk(qp, keys, tm=256)
        g = peer_gates(eid, gate, tm=128)
        uq, su = quantize_rows(peer_u[l], tr=512, name="peer_u_quant")
        vq, sv = quantize_rows(peer_v[l], tr=512, name="peer_v_quant")
        h = peer_dense(xq, sx, uq, su, g, vq, sv, h, norm_final,
                       final_norm=(l + 1 == norm_mix.shape[0]), tm=512, te=1024)
    return h.reshape(batch, seq, d)
```

```python
import functools
import math
from typing import NamedTuple

import numpy as np
import jax
import jax.numpy as jnp
from jax import lax
from jax.experimental import pallas as pl
from jax.experimental.pallas import tpu as pltpu

F32 = jnp.float32
BF16 = jnp.bfloat16
F8 = jnp.float8_e4m3fn
I32 = jnp.int32
FP8_AMAX = 256.0

EPS = 1e-6
ROPE_THETA = 10000.0

MLA_HEADS = 16
MLA_Q_LORA = 1024
MLA_KV_LORA = 512
MLA_NOPE = 128
MLA_ROPE = 64
MLA_QK = MLA_NOPE + MLA_ROPE
MLA_V = 128
MLA_WIDTH = MLA_HEADS * MLA_V

GLA_HEADS = 4
GLA_HEAD_K = 256
GLA_HEAD_V = 512
GLA_K_WIDTH = GLA_HEADS * GLA_HEAD_K
GLA_V_WIDTH = GLA_HEADS * GLA_HEAD_V
GLA_RANK = 16
GLA_TAU = 16.0
GLA_CHUNK = 64

XATTN_HEADS = 4
XATTN_DIM = 256
XATTN_WIDTH = XATTN_HEADS * XATTN_DIM

PEER_HEADS = 8
PEER_KEYS = 128
PEER_HALF = 128
PEER_TOPK = 16
PEER_SLOTS = PEER_HEADS * PEER_TOPK
PEER_EXPERTS = PEER_KEYS * PEER_KEYS

LANES = 128
VMEM_LIMIT = 56 * 1024 * 1024


class _Tiles(NamedTuple):
    rows: int = 512
    cols: int = 1024
    attn_block: int = 512
    gla_rows: int = 1024
    peer_q_rows: int = 256
    topk_tokens: int = 256
    gate_tokens: int = 128
    peer_experts: int = 1024


TILES = _Tiles()

COL_CQ = 0
COL_CKV = COL_CQ + MLA_Q_LORA
COL_KR = COL_CKV + MLA_KV_LORA
COL_LR = COL_KR + LANES
COL_GQ = 2048
COL_GK = COL_GQ + GLA_K_WIDTH
COL_GV = COL_GK + GLA_K_WIDTH
COL_OG = COL_GV + GLA_V_WIDTH
IN_PAD = COL_OG + GLA_V_WIDTH

NEG = -1e30


def _cparams(*sem):
    return pltpu.CompilerParams(dimension_semantics=sem, vmem_limit_bytes=VMEM_LIMIT)


def _rms(x, g):
    ms = jnp.mean(x * x, axis=-1, keepdims=True)
    return x * lax.rsqrt(ms + EPS) * g


def _dot(a, b):
    return jnp.dot(a, b, preferred_element_type=F32)


def _dot_nt(a, b):
    return lax.dot_general(a, b, (((1,), (1,)), ((), ())), preferred_element_type=F32)


def _dot_tn(a, b):
    return lax.dot_general(a, b, (((0,), (0,)), ((), ())), preferred_element_type=F32)


def _quantize_rows(x):
    amax = jnp.max(jnp.abs(x), axis=-1, keepdims=True)
    scale = jnp.where(amax > 0.0, amax * (1.0 / FP8_AMAX), 1.0)
    return (x * (1.0 / scale)).astype(F8), scale


def _norm_matmul_kernel(x_ref, g_ref, w_ref, o_ref, *rest, emit_xq, whole_n, w_transposed):
    if emit_xq:
        xq_ref, sx_ref, xn_ref = rest
    else:
        (xn_ref,) = rest

    def normalise():
        xn = _rms(x_ref[...].astype(F32), g_ref[...])
        xn_ref[...] = xn.astype(BF16)
        if emit_xq:
            xq_ref[...], sx_ref[...] = _quantize_rows(xn)

    if whole_n:
        normalise()
    else:
        pl.when(pl.program_id(1) == 0)(normalise)
    mm = _dot_nt if w_transposed else _dot
    o_ref[...] = mm(xn_ref[...], w_ref[...]).astype(o_ref.dtype)


def norm_matmul(x, gain, w, *, k, x_col_blk=0, tm, tn, out_dtype, emit_xq=False, w_transposed=False, name):
    m = x.shape[0]
    n = w.shape[0] if w_transposed else w.shape[1]
    tm = min(tm, m)
    tn = min(tn, n)
    out_shape = [jax.ShapeDtypeStruct((m, n), out_dtype)]
    out_specs = [pl.BlockSpec((tm, tn), lambda i, j: (i, j))]
    if emit_xq:
        out_shape += [jax.ShapeDtypeStruct((m, k), F8), jax.ShapeDtypeStruct((m, 1), F32)]
        out_specs += [pl.BlockSpec((tm, k), lambda i, j: (i, 0)), pl.BlockSpec((tm, 1), lambda i, j: (i, 0))]
    whole_n = tn == n
    w_mode = pl.Buffered(1) if whole_n else None
    w_spec = (pl.BlockSpec((tn, k), lambda i, j: (j, 0), pipeline_mode=w_mode) if w_transposed
              else pl.BlockSpec((k, tn), lambda i, j: (0, j), pipeline_mode=w_mode))
    res = pl.pallas_call(
        functools.partial(_norm_matmul_kernel, emit_xq=emit_xq, whole_n=whole_n, w_transposed=w_transposed),
        out_shape=out_shape,
        grid=(m // tm, n // tn),
        in_specs=[pl.BlockSpec((tm, k), lambda i, j: (i, x_col_blk)),
                  pl.BlockSpec((1, k), lambda i, j: (0, 0)),
                  w_spec],
        out_specs=out_specs,
        scratch_shapes=[pltpu.VMEM((tm, k), BF16)],
        compiler_params=_cparams("parallel", "arbitrary"),
        name=name,
    )(x, gain.reshape(1, k).astype(F32), w)
    return res if emit_xq else res[0]


def _quantize_kernel(w_ref, q_ref, s_ref):
    q_ref[...], s_ref[...] = _quantize_rows(w_ref[...])


def quantize_rows(w, *, tr, name):
    r, d = w.shape
    return pl.pallas_call(
        _quantize_kernel,
        out_shape=[jax.ShapeDtypeStruct((r, d), F8), jax.ShapeDtypeStruct((r, 1), F32)],
        grid=(r // tr,),
        in_specs=[pl.BlockSpec((tr, d), lambda i: (i, 0))],
        out_specs=[pl.BlockSpec((tr, d), lambda i: (i, 0)), pl.BlockSpec((tr, 1), lambda i: (i, 0))],
        compiler_params=_cparams("parallel"),
        name=name,
    )(w)


def _rope_pair(r, cosv, sinv):
    return r * cosv + pltpu.roll(r, 2 * 32, axis=1) * sinv


def _qproj_kernel(c_ref, g_ref, w_ref, cos_ref, sin_ref, o_ref, *, scale):
    xn = _rms(c_ref[...].astype(F32), g_ref[...]).astype(BF16)
    q = _dot(xn, w_ref[...])
    cosv = cos_ref[...]
    sinv = sin_ref[...]
    lane = lax.broadcasted_iota(I32, (1, LANES), 1)
    rope_base = MLA_HEADS * MLA_NOPE
    for j in range(MLA_HEADS // 2):
        r = q[:, rope_base + j * LANES: rope_base + (j + 1) * LANES]
        r = _rope_pair(r, cosv, sinv) * scale
        for p in range(2):
            h = 2 * j + p
            own = ((lane // 32) % 2) == p
            o_ref[:, h * 256: h * 256 + 128] = (q[:, h * 128:(h + 1) * 128] * scale).astype(o_ref.dtype)
            o_ref[:, h * 256 + 128: h * 256 + 256] = jnp.where(own, r, 0.0).astype(o_ref.dtype)


def mla_q_proj(h_in, gain, w_uq_perm, cos_t, sin_t, *, tm):
    m = h_in.shape[0]
    tm = min(tm, m)
    return pl.pallas_call(
        functools.partial(_qproj_kernel, scale=MLA_QK ** -0.5 * math.log2(math.e)),
        out_shape=jax.ShapeDtypeStruct((m, MLA_HEADS * 256), BF16),
        grid=(m // tm,),
        in_specs=[pl.BlockSpec((tm, MLA_Q_LORA), lambda i: (i, COL_CQ // MLA_Q_LORA)),
                  pl.BlockSpec((1, MLA_Q_LORA), lambda i: (0, 0)),
                  pl.BlockSpec(w_uq_perm.shape, lambda i: (0, 0)),
                  pl.BlockSpec((tm, LANES), lambda i: (i, 0)),
                  pl.BlockSpec((tm, LANES), lambda i: (i, 0))],
        out_specs=pl.BlockSpec((tm, MLA_HEADS * 256), lambda i: (i, 0)),
        compiler_params=_cparams("parallel"),
        name="mla_q_proj",
    )(h_in, gain.reshape(1, -1).astype(F32), w_uq_perm, cos_t, sin_t)


def _kvproj_kernel(c_ref, g_ref, w_ref, kr_ref, cos_ref, sin_ref, kv_ref, kro_ref):
    xn = _rms(c_ref[...].astype(F32), g_ref[...]).astype(BF16)
    kv_ref[...] = _dot(xn, w_ref[...]).astype(kv_ref.dtype)
    kro_ref[...] = _rope_pair(kr_ref[...].astype(F32), cos_ref[...], sin_ref[...]).astype(kro_ref.dtype)


def mla_kv_proj(h_in, gain, w_ukv, cos_t, sin_t, *, tm):
    m = h_in.shape[0]
    tm = min(tm, m)
    n = w_ukv.shape[1]
    return pl.pallas_call(
        _kvproj_kernel,
        out_shape=[jax.ShapeDtypeStruct((m, n), BF16), jax.ShapeDtypeStruct((m, LANES), BF16)],
        grid=(m // tm,),
        in_specs=[pl.BlockSpec((tm, MLA_KV_LORA), lambda i: (i, COL_CKV // MLA_KV_LORA)),
                  pl.BlockSpec((1, MLA_KV_LORA), lambda i: (0, 0)),
                  pl.BlockSpec(w_ukv.shape, lambda i: (0, 0)),
                  pl.BlockSpec((tm, LANES), lambda i: (i, COL_KR // LANES)),
                  pl.BlockSpec((tm, LANES), lambda i: (i, 0)),
                  pl.BlockSpec((tm, LANES), lambda i: (i, 0))],
        out_specs=[pl.BlockSpec((tm, n), lambda i: (i, 0)),
                   pl.BlockSpec((tm, LANES), lambda i: (i, 0))],
        compiler_params=_cparams("parallel"),
        name="mla_kv_proj",
    )(h_in, gain.reshape(1, -1).astype(F32), w_ukv, h_in, cos_t, sin_t)


ATTN_UNROLL = 16


def _mla_attn_kernel(qi_tab, kj_tab, q_ref, kv_ref, kr_ref, o_ref, s0_scr, s1_scr, m_scr, acc_scr,
                     *, blk, nq, n_items):
    ones = jnp.ones((blk, LANES), BF16)
    s_scr = (s0_scr, s1_scr)

    def scores(t, s_ref):
        q_rows = pl.ds(pl.multiple_of(qi_tab[t] * blk, blk), blk)
        k_rows = pl.ds(pl.multiple_of(kj_tab[t] * blk, blk), blk)
        k = jnp.concatenate([kv_ref[k_rows, :MLA_NOPE], kr_ref[k_rows, :]], axis=1)
        s_ref[...] = _dot_nt(q_ref[q_rows, :], k)

    def finish(t, s_ref, diagonal):
        qi = qi_tab[t]
        k_rows = pl.ds(pl.multiple_of(kj_tab[t] * blk, blk), blk)
        v = jnp.concatenate([kv_ref[k_rows, MLA_NOPE:], ones], axis=1)
        s = s_ref[...]
        if diagonal:
            row = lax.broadcasted_iota(I32, (blk, blk), 0)
            col = lax.broadcasted_iota(I32, (blk, blk), 1)
            s = jnp.where(col <= row, s, NEG)
            m_new = jnp.max(s, axis=1, keepdims=True) + jnp.zeros((blk, LANES), F32)
        else:
            m = m_scr[qi]
            m_new = jnp.maximum(m, jnp.max(s, axis=1, keepdims=True))
        p = jnp.exp2(s - jnp.concatenate([m_new] * (blk // LANES), axis=1)).astype(BF16)
        pv = _dot(p, v)
        if diagonal:
            acc_scr[qi] = pv
        else:
            alpha = jnp.exp2(m - m_new)
            acc_scr[qi] = jnp.concatenate([alpha, alpha], axis=1) * acc_scr[qi] + pv
        m_scr[qi] = m_new

    def run(t0, n, slot, diagonal):
        for u in range(n):
            scores(jnp.minimum(t0 + u + 1, n_items - 1), s_scr[(slot + u + 1) % 2])
            finish(t0 + u, s_scr[(slot + u) % 2], diagonal)

    def phase(t0, n, slot, diagonal):
        def body(i, carry):
            run(t0 + i * ATTN_UNROLL, ATTN_UNROLL, slot, diagonal)
            return carry

        lax.fori_loop(0, n // ATTN_UNROLL, body, 0)
        if n % ATTN_UNROLL:
            run(t0 + n - n % ATTN_UNROLL, n % ATTN_UNROLL, slot, diagonal)
        return (slot + n) % 2

    scores(0, s0_scr)
    slot = phase(0, nq, 0, True)
    phase(nq, n_items - nq, slot, False)

    def normalise(qi, carry):
        acc = acc_scr[qi]
        o_ref[pl.ds(pl.multiple_of(qi * blk, blk), blk), :] = (acc[:, :MLA_V] / acc[:, MLA_V:]).astype(o_ref.dtype)
        return carry

    lax.fori_loop(0, nq, normalise, 0)


def mla_attention(q_full, kv, k_rope, *, batch, seq, blk):
    blk = min(blk, seq)
    nq = seq // blk
    m = batch * seq
    pairs = [(i, i) for i in range(nq)] + [(i, i - d) for d in range(1, nq) for i in range(d, nq)]
    qi_tab = jnp.asarray([p[0] for p in pairs], I32)
    kj_tab = jnp.asarray([p[1] for p in pairs], I32)
    grid_spec = pltpu.PrefetchScalarGridSpec(
        num_scalar_prefetch=2,
        grid=(batch, MLA_HEADS),
        in_specs=[pl.BlockSpec((seq, 256), lambda b, h, *_: (b, h)),
                  pl.BlockSpec((seq, 256), lambda b, h, *_: (b, h)),
                  pl.BlockSpec((seq, LANES), lambda b, h, *_: (b, 0))],
        out_specs=pl.BlockSpec((seq, MLA_V), lambda b, h, *_: (b, h)),
        scratch_shapes=[pltpu.VMEM((blk, blk), F32), pltpu.VMEM((blk, blk), F32),
                        pltpu.VMEM((nq, blk, LANES), F32), pltpu.VMEM((nq, blk, 2 * MLA_V), F32)],
    )
    return pl.pallas_call(
        functools.partial(_mla_attn_kernel, blk=blk, nq=nq, n_items=len(pairs)),
        out_shape=jax.ShapeDtypeStruct((m, MLA_WIDTH), BF16),
        grid_spec=grid_spec,
        compiler_params=_cparams("parallel", "parallel"),
        name="mla_attention",
    )(qi_tab, kj_tab, q_full, kv, k_rope)


GLA_HEADS_PER_STEP = 1


def _gla_kernel(q_ref, k_ref, v_ref, og_ref, lr_ref, wg_ref, bg_ref, gn_ref, o_ref, st_ref, *, sb, c):
    @pl.when(pl.program_id(2) == 0)
    def _():
        st_ref[...] = jnp.zeros_like(st_ref)

    z = _dot(lr_ref[...], wg_ref[...]) + bg_ref[...]
    log_a = (jnp.minimum(z, 0.0) - jnp.log(1.0 + jnp.exp(-jnp.abs(z)))) * (1.0 / GLA_TAU)
    rows = lax.broadcasted_iota(I32, (c, c), 0)
    cols = lax.broadcasted_iota(I32, (c, c), 1)
    causal = cols <= rows
    tril = causal.astype(F32)
    gn = gn_ref[...]
    scale = GLA_HEAD_K ** -0.5

    for n in range(sb // c):
        sl = slice(n * c, (n + 1) * c)
        for hh in range(GLA_HEADS_PER_STEP):
            kcols = slice(hh * GLA_HEAD_K, (hh + 1) * GLA_HEAD_K)
            vcols = slice(hh * GLA_HEAD_V, (hh + 1) * GLA_HEAD_V)
            g = log_a[sl, kcols]
            b = jnp.dot(tril, g, preferred_element_type=F32, precision=lax.Precision.HIGHEST)
            b_last = b[c - 1:c, :]
            q = q_ref[sl, kcols].astype(F32)
            k = k_ref[sl, kcols].astype(F32)
            v = v_ref[sl, vcols]
            qe = (q * scale * jnp.exp(b)).astype(BF16)
            ke = (k * jnp.exp(-b)).astype(BF16)
            kd = (k * jnp.exp(b_last - b)).astype(BF16)
            att = jnp.where(causal, _dot_nt(qe, ke), 0.0).astype(BF16)
            st = st_ref[hh]
            o = _dot(att, v) + _dot_nt(qe, st.astype(BF16))
            st_ref[hh] = st * jnp.exp(b_last) + _dot_tn(v, kd)
            og = og_ref[sl, vcols].astype(F32)
            o = _rms(o, gn) * (og * (1.0 / (1.0 + jnp.exp(-og))))
            o_ref[sl, vcols] = o.astype(o_ref.dtype)


def gla_mixer(h_in, w_gate_pad, b_gate, gla_out_norm, *, batch, seq, sb):
    sb = min(sb, seq)
    nsb = seq // sb
    m = batch * seq
    g = GLA_HEADS_PER_STEP
    kw, vw = g * GLA_HEAD_K, g * GLA_HEAD_V
    return pl.pallas_call(
        functools.partial(_gla_kernel, sb=sb, c=GLA_CHUNK),
        out_shape=jax.ShapeDtypeStruct((m, GLA_V_WIDTH), BF16),
        grid=(batch, GLA_HEADS // g, nsb),
        in_specs=[pl.BlockSpec((sb, kw), lambda b, h, s: (b * nsb + s, COL_GQ // kw + h)),
                  pl.BlockSpec((sb, kw), lambda b, h, s: (b * nsb + s, COL_GK // kw + h)),
                  pl.BlockSpec((sb, vw), lambda b, h, s: (b * nsb + s, COL_GV // vw + h)),
                  pl.BlockSpec((sb, vw), lambda b, h, s: (b * nsb + s, COL_OG // vw + h)),
                  pl.BlockSpec((sb, LANES), lambda b, h, s: (b * nsb + s, COL_LR // LANES)),
                  pl.BlockSpec((LANES, kw), lambda b, h, s: (0, h)),
                  pl.BlockSpec((1, kw), lambda b, h, s: (0, h)),
                  pl.BlockSpec((1, GLA_HEAD_V), lambda b, h, s: (0, 0))],
        out_specs=pl.BlockSpec((sb, vw), lambda b, h, s: (b * nsb + s, h)),
        scratch_shapes=[pltpu.VMEM((g, GLA_HEAD_V, GLA_HEAD_K), F32)],
        compiler_params=_cparams("parallel", "parallel", "arbitrary"),
        name="gla_mixer",
    )(h_in, h_in, h_in, h_in, h_in, w_gate_pad, b_gate.reshape(1, -1).astype(F32),
      gla_out_norm.reshape(1, -1).astype(F32))


def _out_proj_kernel(om_ref, g_ref, og_ref, w_ref, x_ref, o_ref, mix_ref):
    @pl.when(pl.program_id(1) == 0)
    def _():
        mix_ref[:, :MLA_WIDTH] = _rms(om_ref[...].astype(F32), g_ref[...]).astype(BF16)
        mix_ref[:, MLA_WIDTH:] = og_ref[...]

    o_ref[...] = x_ref[...] + _dot(mix_ref[...], w_ref[...])


def mixer_out_proj(o_mla, mla_out_norm, o_gla, w_out, x, *, tm, tn):
    m, d = x.shape
    tm = min(tm, m)
    tn = min(tn, d)
    kmix = MLA_WIDTH + GLA_V_WIDTH
    return pl.pallas_call(
        _out_proj_kernel,
        out_shape=jax.ShapeDtypeStruct((m, d), F32),
        grid=(m // tm, d // tn),
        in_specs=[pl.BlockSpec((tm, MLA_WIDTH), lambda i, j: (i, 0)),
                  pl.BlockSpec((1, MLA_WIDTH), lambda i, j: (0, 0)),
                  pl.BlockSpec((tm, GLA_V_WIDTH), lambda i, j: (i, 0)),
                  pl.BlockSpec((kmix, tn), lambda i, j: (0, j)),
                  pl.BlockSpec((tm, tn), lambda i, j: (i, j))],
        out_specs=pl.BlockSpec((tm, tn), lambda i, j: (i, j)),
        scratch_shapes=[pltpu.VMEM((tm, kmix), BF16)],
        compiler_params=_cparams("parallel", "arbitrary"),
        name="mixer_out_proj",
    )(o_mla, mla_out_norm.reshape(1, -1).astype(F32), o_gla, w_out, x)


def _xattn_kernel(q_ref, kv_ref, w_ref, h_ref, o_ref):
    scale = XATTN_DIM ** -0.5
    outs = []
    for h in range(XATTN_HEADS):
        q = q_ref[:, h * XATTN_DIM:(h + 1) * XATTN_DIM]
        k = kv_ref[:, h * XATTN_DIM:(h + 1) * XATTN_DIM]
        v = kv_ref[:, XATTN_WIDTH + h * XATTN_DIM: XATTN_WIDTH + (h + 1) * XATTN_DIM]
        s = _dot_nt(q, k) * scale
        p = jnp.exp(s - jnp.max(s, axis=1, keepdims=True))
        p = p / jnp.sum(p, axis=1, keepdims=True)
        outs.append(_dot(p.astype(BF16), v).astype(BF16))
    o = jnp.concatenate(outs, axis=1)
    o_ref[...] = h_ref[...] + _dot(o, w_ref[...])


def cross_attention(qx, kvm, w_co, h1, *, batch, seq, mem_tokens, tm):
    m, d = h1.shape
    tm = min(tm, seq)
    per_b = seq // tm
    return pl.pallas_call(
        _xattn_kernel,
        out_shape=jax.ShapeDtypeStruct((m, d), F32),
        grid=(m // tm,),
        in_specs=[pl.BlockSpec((tm, XATTN_WIDTH), lambda i: (i, 0)),
                  pl.BlockSpec((mem_tokens, 2 * XATTN_WIDTH), lambda i: (i // per_b, 0)),
                  pl.BlockSpec(w_co.shape, lambda i: (0, 0), pipeline_mode=pl.Buffered(1)),
                  pl.BlockSpec((tm, d), lambda i: (i, 0))],
        out_specs=pl.BlockSpec((tm, d), lambda i: (i, 0)),
        compiler_params=_cparams("parallel"),
        name="cross_attention",
    )(qx, kvm, w_co, h1)


def _order_key(x):
    b = lax.bitcast_convert_type(x, I32)
    return b ^ (lax.shift_right_arithmetic(b, 31) & 0x7FFFFFFF)


def _order_key_inv(k):
    return lax.bitcast_convert_type(k ^ (lax.shift_right_arithmetic(k, 31) & 0x7FFFFFFF), F32)


KEY_BIAS = 0x20000000


def _pack_keys(x, bits):
    low = (1 << bits) - 1
    row = lax.broadcasted_iota(I32, x.shape, 0)
    k = (_order_key(x) & ~((low << 2) | 3)) | ((low - row) << 2)
    return lax.bitcast_convert_type(lax.shift_right_arithmetic(k, 2) + KEY_BIAS, F32)


def _unpack_keys(kf, bits):
    low = (1 << bits) - 1
    k = lax.shift_left(lax.bitcast_convert_type(kf, I32) - KEY_BIAS, 2)
    return _order_key_inv(k & ~((low << 2) | 3)), low - (lax.shift_right_arithmetic(k, 2) & low)


SUBLANES = 8


def _max_rows(k3):
    m8 = jnp.max(k3, axis=0)
    for shift in (4, 2, 1):
        m8 = jnp.maximum(m8, pltpu.roll(m8, shift, axis=0))
    return m8


def _top16_rows(s, bits):
    rows, t = s.shape
    k3 = _pack_keys(s, bits).reshape(rows // SUBLANES, SUBLANES, t)
    vals, idxs = [], []
    for _ in range(PEER_TOPK):
        m8 = _max_rows(k3)
        val, idx = _unpack_keys(m8[0:1, :], bits)
        vals.append(val)
        idxs.append(idx)
        k3 = jnp.where(k3 == m8[None], 0.0, k3)
    return vals, idxs


_PAIRS = [(a, b) for a in range(PEER_TOPK) for b in range(PEER_TOPK) if (a + 1) * (b + 1) <= PEER_TOPK]
_PAIR_ROWS = -(-len(_PAIRS) // 8) * 8


TOPK_HEADS_PER_TRIP = 2


def _peer_topk_kernel(q_ref, keys_ref, eid_ref, gate_ref, eid_t, gate_t, *, tm):
    def head(h):
        tops = []
        for p in range(2):
            col = pl.multiple_of((2 * h + p) * PEER_HALF, PEER_HALF)
            qh = q_ref[:, pl.ds(col, PEER_HALF)]
            s = _dot_nt(keys_ref[2 * h + p], qh)
            tops.append(_top16_rows(s, 7))
        (s1, i1), (s2, i2) = tops
        cand_s = [s1[a] + s2[b] for a, b in _PAIRS]
        cand_e = [i1[a] * PEER_KEYS + i2[b] for a, b in _PAIRS]
        pad = _PAIR_ROWS - len(_PAIRS)
        cs = jnp.concatenate(cand_s + [jnp.full((pad, tm), NEG, F32)], axis=0)
        ce = jnp.concatenate(cand_e + [jnp.zeros((pad, tm), I32)], axis=0)
        k3 = _pack_keys(cs, 6).reshape(_PAIR_ROWS // SUBLANES, SUBLANES, tm)
        ce3 = ce.astype(F32).reshape(_PAIR_ROWS // SUBLANES, SUBLANES, tm)
        sel_s, sel_e = [], []
        for _ in range(PEER_TOPK):
            m8 = _max_rows(k3)
            hit = k3 == m8[None]
            sel_s.append(_unpack_keys(m8[0:1, :], 6)[0])
            sel_e.append(_max_rows(jnp.where(hit, ce3, -1.0))[0:1, :].astype(I32))
            k3 = jnp.where(hit, 0.0, k3)
        top_s = jnp.concatenate(sel_s, axis=0)
        top_e = jnp.concatenate(sel_e, axis=0)
        w = jnp.exp(top_s - top_s[0:1, :])
        gate = w / jnp.sum(w, axis=0, keepdims=True)
        row = pl.multiple_of(h * PEER_TOPK, PEER_TOPK)
        eid_t[pl.ds(row, PEER_TOPK), :] = top_e
        gate_t[pl.ds(row, PEER_TOPK), :] = gate

    def trip(i, carry):
        for u in range(TOPK_HEADS_PER_TRIP):
            head(i * TOPK_HEADS_PER_TRIP + u)
        return carry

    lax.fori_loop(0, PEER_HEADS // TOPK_HEADS_PER_TRIP, trip, 0)
    eid_ref[...] = eid_t[...].T
    gate_ref[...] = gate_t[...].T


def peer_topk(qp, keys, *, tm):
    m = qp.shape[0]
    tm = min(tm, m)
    return pl.pallas_call(
        functools.partial(_peer_topk_kernel, tm=tm),
        out_shape=[jax.ShapeDtypeStruct((m, PEER_SLOTS), I32), jax.ShapeDtypeStruct((m, PEER_SLOTS), F32)],
        grid=(m // tm,),
        in_specs=[pl.BlockSpec((tm, qp.shape[1]), lambda i: (i, 0)),
                  pl.BlockSpec(keys.shape, lambda i: (0, 0, 0))],
        out_specs=[pl.BlockSpec((tm, PEER_SLOTS), lambda i: (i, 0)),
                   pl.BlockSpec((tm, PEER_SLOTS), lambda i: (i, 0))],
        scratch_shapes=[pltpu.VMEM((PEER_SLOTS, tm), I32), pltpu.VMEM((PEER_SLOTS, tm), F32)],
        compiler_params=_cparams("parallel"),
        name="peer_topk",
    )(qp, keys)


GATE_UNROLL = 32


def _peer_gates_kernel(eid_ref, gate_ref, o_ref, g3_ref, *, tm, pitch):
    sub = lax.broadcasted_iota(I32, (PEER_KEYS, PEER_SLOTS), 0)

    def token(t, carry):
        e = eid_ref[pl.ds(t, 1), :]
        g = gate_ref[pl.ds(t, 1), :]
        hi = lax.shift_right_logical(e, 7)
        lo = jnp.bitwise_and(e, PEER_KEYS - 1)
        a1 = jnp.where(sub == hi, g, 0.0).astype(BF16)
        a2 = jnp.where(sub == lo, 1.0, 0.0).astype(BF16)
        g3_ref[pl.ds(t, PEER_KEYS, stride=pitch), :] = _dot_nt(a1, a2)
        return carry

    lax.fori_loop(0, tm, token, 0, unroll=GATE_UNROLL)
    for e1 in range(PEER_KEYS):
        o_ref[:, e1 * PEER_KEYS:(e1 + 1) * PEER_KEYS] = g3_ref[e1 * pitch:e1 * pitch + tm, :].astype(o_ref.dtype)


def peer_gates(eid, gate, *, tm):
    m = eid.shape[0]
    tm = min(tm, m)
    pitch = tm + 8
    return pl.pallas_call(
        functools.partial(_peer_gates_kernel, tm=tm, pitch=pitch),
        out_shape=jax.ShapeDtypeStruct((m, PEER_EXPERTS), BF16),
        grid=(m // tm,),
        in_specs=[pl.BlockSpec((tm, PEER_SLOTS), lambda i: (i, 0)),
                  pl.BlockSpec((tm, PEER_SLOTS), lambda i: (i, 0))],
        out_specs=pl.BlockSpec((tm, PEER_EXPERTS), lambda i: (i, 0)),
        scratch_shapes=[pltpu.VMEM((PEER_KEYS * pitch, PEER_KEYS), F32)],
        compiler_params=_cparams("parallel"),
        name="peer_gates",
    )(eid, gate)


def _peer_dense_kernel(x_ref, sx_ref, u_ref, su_ref, g_ref, v_ref, sv_ref, h_ref, gn_ref, o_ref, aq_scr, sa_scr,
                       *, final_norm, n_blocks):
    j = pl.program_id(1)
    nj = n_blocks

    def step(p, do_values, do_gates):
        if do_gates:
            s = _dot_nt(x_ref[...], u_ref[...])
        if do_values:
            pv = _dot(aq_scr[1 - p], v_ref[...])
        if do_gates:
            s = s * sx_ref[...] * su_ref[...]
            a = 0.5 * s * (1.0 + lax.erf(s * (2.0 ** -0.5)))
            aq_scr[p], sa_scr[p] = _quantize_rows(a * g_ref[...].astype(F32) * sv_ref[...])
        if do_values:
            o_ref[...] += sa_scr[1 - p] * pv

    half = o_ref.shape[1] // 2

    @pl.when(j == 0)
    def _():
        o_ref[:, :half] = h_ref[...]
        o_ref[:, half:] = jnp.zeros((o_ref.shape[0], half), F32)
        step(0, False, True)

    @pl.when(j == 1)
    def _():
        o_ref[:, half:] += h_ref[...]

    for p in range(2):
        @pl.when(jnp.logical_and(jnp.logical_and(j > 0, j < nj), j % 2 == p))
        def _(p=p):
            step(p, True, True)

    @pl.when(j == nj)
    def _():
        step(nj % 2, True, False)
        if final_norm:
            o_ref[...] = _rms(o_ref[...], gn_ref[...])


def peer_dense(xq, sx, uq, su, g, vq, sv, h, gain, *, final_norm, tm, te):
    m, d = xq.shape
    e = uq.shape[0]
    tm = min(tm, m)
    nj = e // te
    def cur(j):
        return jnp.minimum(j, nj - 1)

    def prev(j):
        return jnp.maximum(j - 1, 0)

    return pl.pallas_call(
        functools.partial(_peer_dense_kernel, final_norm=final_norm, n_blocks=nj),
        out_shape=jax.ShapeDtypeStruct((m, d), F32),
        grid=(m // tm, nj + 1),
        in_specs=[pl.BlockSpec((tm, d), lambda i, j: (i, 0)),
                  pl.BlockSpec((tm, 1), lambda i, j: (i, 0)),
                  pl.BlockSpec((te, d), lambda i, j: (cur(j), 0)),
                  pl.BlockSpec((1, te), lambda i, j: (0, cur(j))),
                  pl.BlockSpec((tm, te), lambda i, j: (i, cur(j))),
                  pl.BlockSpec((te, d), lambda i, j: (prev(j), 0)),
                  pl.BlockSpec((1, te), lambda i, j: (0, cur(j))),
                  pl.BlockSpec((tm, d // 2), lambda i, j: (i, jnp.minimum(j, 1))),
                  pl.BlockSpec((1, d), lambda i, j: (0, 0))],
        out_specs=pl.BlockSpec((tm, d), lambda i, j: (i, 0)),
        scratch_shapes=[pltpu.VMEM((2, tm, te), F8), pltpu.VMEM((2, tm, 1), F32)],
        compiler_params=_cparams("parallel", "arbitrary"),
        name="peer_dense",
    )(xq, sx, uq, su.reshape(1, e), g, vq, sv.reshape(1, e), h, gain.reshape(1, d).astype(F32))


def _w_in_columns():
    src = {}
    off = 0
    for name, width in (("cq", MLA_Q_LORA), ("ckv", MLA_KV_LORA), ("kr", MLA_ROPE), ("gq", GLA_K_WIDTH),
                        ("gk", GLA_K_WIDTH), ("gv", GLA_V_WIDTH), ("lr", GLA_RANK), ("og", GLA_V_WIDTH)):
        src[name] = np.arange(off, off + width)
        off += width
    cols = np.full((IN_PAD,), -1, np.int64)
    for name, start in (("cq", COL_CQ), ("ckv", COL_CKV), ("gq", COL_GQ), ("gk", COL_GK),
                        ("gv", COL_GV), ("og", COL_OG), ("lr", COL_LR)):
        cols[start:start + len(src[name])] = src[name]
    half = MLA_ROPE // 2
    x1, x2 = src["kr"][:half], src["kr"][half:]
    cols[COL_KR:COL_KR + LANES] = np.concatenate([x1, x1, x2, x2])
    return cols


def _w_uq_columns():
    half = MLA_ROPE // 2
    nope = [h * MLA_QK + d for h in range(MLA_HEADS) for d in range(MLA_NOPE)]
    rope = []
    for j in range(MLA_HEADS // 2):
        a, b = 2 * j, 2 * j + 1
        for part in (0, 1):
            for h in (a, b):
                rope += [h * MLA_QK + MLA_NOPE + part * half + r for r in range(half)]
    return np.array(nope + rope, np.int64)


def _column_runs(cols):
    runs = []
    start = 0
    for i in range(1, len(cols) + 1):
        if i == len(cols) or cols[i] != (cols[i - 1] + 1 if cols[i - 1] >= 0 else -1):
            runs.append((int(cols[start]), i - start))
            start = i
    return runs


def _take_cols(w, cols, dtype):
    pieces = [jnp.zeros((w.shape[0], n), dtype) if src < 0 else w[:, src:src + n].astype(dtype)
              for src, n in _column_runs(cols)]
    return jnp.concatenate(pieces, axis=1)


def _take_rows(wt, rows, dtype):
    pieces = [jnp.zeros((n, wt.shape[1]), dtype) if src < 0 else wt[src:src + n].astype(dtype)
              for src, n in _column_runs(rows)]
    return jnp.concatenate(pieces, axis=0)


def _rope_tables(positions):
    half = MLA_ROPE // 2
    inv_freq = ROPE_THETA ** (-jnp.arange(0, MLA_ROPE, 2, dtype=F32) / MLA_ROPE)
    ang = positions.astype(F32).reshape(-1, 1) * inv_freq
    cos = jnp.cos(ang)
    sin = jnp.sin(ang)
    return jnp.tile(cos, (1, 4)), jnp.concatenate([-sin, -sin, sin, sin], axis=1)


def kernel(x, mem, positions, norm_mem, norm_mix, w_in, mla_q_norm, w_uq, mla_kv_norm, w_ukv, mla_out_norm,
           w_gate_up, b_gate, gla_out_norm, w_out, norm_cross, w_cq, w_ck, w_cv, w_co, norm_ffn, w_peer_q,
           peer_sub_keys, peer_u, peer_v, norm_final):
    batch, seq, d = x.shape
    mem_tokens = mem.shape[1]
    m = batch * seq
    h = x.reshape(m, d)
    cos_t, sin_t = _rope_tables(positions)

    t = TILES
    depth = norm_mix.shape[0]
    for l in range(depth):
        w_in_t = _take_rows(jnp.transpose(w_in[l]), _w_in_columns(), BF16)
        h_in = norm_matmul(h, norm_mix[l], w_in_t, k=d, tm=t.rows, tn=t.cols, out_dtype=BF16, w_transposed=True,
                           name="in_proj")

        w_uq_p = _take_cols(w_uq[l], _w_uq_columns(), BF16)
        q_full = mla_q_proj(h_in, mla_q_norm[l], w_uq_p, cos_t, sin_t, tm=t.rows)
        kv, k_rope = mla_kv_proj(h_in, mla_kv_norm[l], w_ukv[l].astype(BF16), cos_t, sin_t, tm=t.rows)
        o_mla = mla_attention(q_full, kv, k_rope, batch=batch, seq=seq, blk=t.attn_block)

        w_gate_pad = jnp.zeros((LANES, GLA_K_WIDTH), F32).at[:GLA_RANK].set(w_gate_up[l]).astype(BF16)
        o_gla = gla_mixer(h_in, w_gate_pad, b_gate[l], gla_out_norm[l], batch=batch, seq=seq, sb=t.gla_rows)

        h = mixer_out_proj(o_mla, mla_out_norm[l], o_gla, w_out[l].astype(BF16), h, tm=t.rows, tn=t.cols)

        mn_kv = norm_matmul(mem.reshape(batch * mem_tokens, d), norm_mem,
                            jnp.concatenate([w_ck[l], w_cv[l]], axis=1).astype(BF16),
                            k=d, tm=t.rows, tn=t.cols, out_dtype=BF16, name="mem_kv_proj")
        qx = norm_matmul(h, norm_cross[l], w_cq[l].astype(BF16), k=d, tm=t.rows, tn=XATTN_WIDTH, out_dtype=BF16,
                         name="xattn_q_proj")
        h = cross_attention(qx, mn_kv, w_co[l].astype(BF16), h, batch=batch, seq=seq,
                            mem_tokens=mem_tokens, tm=t.rows)

        qp, xq, sx = norm_matmul(h, norm_ffn[l], w_peer_q[l].astype(BF16), k=d, tm=t.peer_q_rows,
                                 tn=PEER_HEADS * 2 * PEER_HALF, out_dtype=BF16, emit_xq=True, name="peer_q_proj")
        keys = peer_sub_keys[l].reshape(PEER_HEADS * 2, PEER_KEYS, PEER_HALF).astype(BF16)
        eid, gate = peer_topk(qp, keys, tm=t.topk_tokens)
        g = peer_gates(eid, gate, tm=t.gate_tokens)
        uq, su = quantize_rows(peer_u[l], tr=t.rows, name="peer_u_quant")
        vq, sv = quantize_rows(peer_v[l], tr=t.rows, name="peer_v_quant")
        h = peer_dense(xq, sx, uq, su, g, vq, sv, h, norm_final, final_norm=(l + 1 == depth),
                       tm=t.rows, te=t.peer_experts)
    return h.reshape(batch, seq, d)
```

```python
import functools
import math
from typing import NamedTuple

import numpy as np
import jax
import jax.numpy as jnp
from jax import lax
from jax.experimental import pallas as pl
from jax.experimental.pallas import tpu as pltpu

F32 = jnp.float32
BF16 = jnp.bfloat16
F8 = jnp.float8_e4m3fn
I32 = jnp.int32
FP8_AMAX = 256.0

EPS = 1e-6
ROPE_THETA = 10000.0

MLA_HEADS = 16
MLA_Q_LORA = 1024
MLA_KV_LORA = 512
MLA_NOPE = 128
MLA_ROPE = 64
MLA_QK = MLA_NOPE + MLA_ROPE
MLA_V = 128
MLA_WIDTH = MLA_HEADS * MLA_V

GLA_HEADS = 4
GLA_HEAD_K = 256
GLA_HEAD_V = 512
GLA_K_WIDTH = GLA_HEADS * GLA_HEAD_K
GLA_V_WIDTH = GLA_HEADS * GLA_HEAD_V
GLA_RANK = 16
GLA_TAU = 16.0
GLA_CHUNK = 64

XATTN_HEADS = 4
XATTN_DIM = 256
XATTN_WIDTH = XATTN_HEADS * XATTN_DIM

PEER_HEADS = 8
PEER_KEYS = 128
PEER_HALF = 128
PEER_TOPK = 16
PEER_SLOTS = PEER_HEADS * PEER_TOPK
PEER_EXPERTS = PEER_KEYS * PEER_KEYS

LANES = 128
VMEM_LIMIT = 56 * 1024 * 1024


class _Tiles(NamedTuple):
    rows: int = 512
    cols: int = 1024
    attn_block: int = 512
    gla_rows: int = 1024
    peer_q_rows: int = 256
    topk_tokens: int = 256
    gate_tokens: int = 128
    peer_experts: int = 1024


TILES = _Tiles()

COL_CQ = 0
COL_CKV = COL_CQ + MLA_Q_LORA
COL_KR = COL_CKV + MLA_KV_LORA
COL_LR = COL_KR + LANES
COL_GQ = 2048
COL_GK = COL_GQ + GLA_K_WIDTH
COL_GV = COL_GK + GLA_K_WIDTH
COL_OG = COL_GV + GLA_V_WIDTH
IN_PAD = COL_OG + GLA_V_WIDTH

NEG = -1e30


def _cparams(*sem):
    return pltpu.CompilerParams(dimension_semantics=sem, vmem_limit_bytes=VMEM_LIMIT)


def _rms(x, g):
    ms = jnp.mean(x * x, axis=-1, keepdims=True)
    return x * lax.rsqrt(ms + EPS) * g


def _dot(a, b):
    return jnp.dot(a, b, preferred_element_type=F32)


def _dot_nt(a, b):
    return lax.dot_general(a, b, (((1,), (1,)), ((), ())), preferred_element_type=F32)


def _dot_tn(a, b):
    return lax.dot_general(a, b, (((0,), (0,)), ((), ())), preferred_element_type=F32)


def _quantize_rows(x):
    amax = jnp.max(jnp.abs(x), axis=-1, keepdims=True)
    scale = jnp.where(amax > 0.0, amax * (1.0 / FP8_AMAX), 1.0)
    return (x * (1.0 / scale)).astype(F8), scale


def _norm_matmul_kernel(x_ref, g_ref, w_ref, o_ref, *rest, emit_xq, whole_n, w_transposed):
    if emit_xq:
        xq_ref, sx_ref, xn_ref = rest
    else:
        (xn_ref,) = rest

    def normalise():
        xn = _rms(x_ref[...].astype(F32), g_ref[...])
        xn_ref[...] = xn.astype(BF16)
        if emit_xq:
            xq_ref[...], sx_ref[...] = _quantize_rows(xn)

    if whole_n:
        normalise()
    else:
        pl.when(pl.program_id(1) == 0)(normalise)
    mm = _dot_nt if w_transposed else _dot
    o_ref[...] = mm(xn_ref[...], w_ref[...]).astype(o_ref.dtype)


def norm_matmul(x, gain, w, *, k, x_col_blk=0, tm, tn, out_dtype, emit_xq=False, w_transposed=False, name):
    m = x.shape[0]
    n = w.shape[0] if w_transposed else w.shape[1]
    tm = min(tm, m)
    tn = min(tn, n)
    out_shape = [jax.ShapeDtypeStruct((m, n), out_dtype)]
    out_specs = [pl.BlockSpec((tm, tn), lambda i, j: (i, j))]
    if emit_xq:
        out_shape += [jax.ShapeDtypeStruct((m, k), F8), jax.ShapeDtypeStruct((m, 1), F32)]
        out_specs += [pl.BlockSpec((tm, k), lambda i, j: (i, 0)), pl.BlockSpec((tm, 1), lambda i, j: (i, 0))]
    whole_n = tn == n
    w_mode = pl.Buffered(1) if whole_n else None
    w_spec = (pl.BlockSpec((tn, k), lambda i, j: (j, 0), pipeline_mode=w_mode) if w_transposed
              else pl.BlockSpec((k, tn), lambda i, j: (0, j), pipeline_mode=w_mode))
    res = pl.pallas_call(
        functools.partial(_norm_matmul_kernel, emit_xq=emit_xq, whole_n=whole_n, w_transposed=w_transposed),
        out_shape=out_shape,
        grid=(m // tm, n // tn),
        in_specs=[pl.BlockSpec((tm, k), lambda i, j: (i, x_col_blk)),
                  pl.BlockSpec((1, k), lambda i, j: (0, 0)),
                  w_spec],
        out_specs=out_specs,
        scratch_shapes=[pltpu.VMEM((tm, k), BF16)],
        compiler_params=_cparams("parallel", "arbitrary"),
        name=name,
    )(x, gain.reshape(1, k).astype(F32), w)
    return res if emit_xq else res[0]


def _quantize_kernel(w_ref, q_ref, s_ref):
    q_ref[...], s_ref[...] = _quantize_rows(w_ref[...])


def quantize_rows(w, *, tr, name):
    r, d = w.shape
    return pl.pallas_call(
        _quantize_kernel,
        out_shape=[jax.ShapeDtypeStruct((r, d), F8), jax.ShapeDtypeStruct((r, 1), F32)],
        grid=(r // tr,),
        in_specs=[pl.BlockSpec((tr, d), lambda i: (i, 0))],
        out_specs=[pl.BlockSpec((tr, d), lambda i: (i, 0)), pl.BlockSpec((tr, 1), lambda i: (i, 0))],
        compiler_params=_cparams("parallel"),
        name=name,
    )(w)


def _rope_pair(r, cosv, sinv):
    return r * cosv + pltpu.roll(r, 2 * 32, axis=1) * sinv


def _qproj_kernel(c_ref, g_ref, w_ref, cos_ref, sin_ref, o_ref, *, scale):
    xn = _rms(c_ref[...].astype(F32), g_ref[...]).astype(BF16)
    q = _dot(xn, w_ref[...])
    cosv = cos_ref[...]
    sinv = sin_ref[...]
    lane = lax.broadcasted_iota(I32, (1, LANES), 1)
    rope_base = MLA_HEADS * MLA_NOPE
    for j in range(MLA_HEADS // 2):
        r = q[:, rope_base + j * LANES: rope_base + (j + 1) * LANES]
        r = _rope_pair(r, cosv, sinv) * scale
        for p in range(2):
            h = 2 * j + p
            own = ((lane // 32) % 2) == p
            o_ref[:, h * 256: h * 256 + 128] = (q[:, h * 128:(h + 1) * 128] * scale).astype(o_ref.dtype)
            o_ref[:, h * 256 + 128: h * 256 + 256] = jnp.where(own, r, 0.0).astype(o_ref.dtype)


def mla_q_proj(h_in, gain, w_uq_perm, cos_t, sin_t, *, tm):
    m = h_in.shape[0]
    tm = min(tm, m)
    return pl.pallas_call(
        functools.partial(_qproj_kernel, scale=MLA_QK ** -0.5 * math.log2(math.e)),
        out_shape=jax.ShapeDtypeStruct((m, MLA_HEADS * 256), BF16),
        grid=(m // tm,),
        in_specs=[pl.BlockSpec((tm, MLA_Q_LORA), lambda i: (i, COL_CQ // MLA_Q_LORA)),
                  pl.BlockSpec((1, MLA_Q_LORA), lambda i: (0, 0)),
                  pl.BlockSpec(w_uq_perm.shape, lambda i: (0, 0)),
                  pl.BlockSpec((tm, LANES), lambda i: (i, 0)),
                  pl.BlockSpec((tm, LANES), lambda i: (i, 0))],
        out_specs=pl.BlockSpec((tm, MLA_HEADS * 256), lambda i: (i, 0)),
        compiler_params=_cparams("parallel"),
        name="mla_q_proj",
    )(h_in, gain.reshape(1, -1).astype(F32), w_uq_perm, cos_t, sin_t)


def _kvproj_kernel(c_ref, g_ref, w_ref, kr_ref, cos_ref, sin_ref, kv_ref, kro_ref):
    xn = _rms(c_ref[...].astype(F32), g_ref[...]).astype(BF16)
    kv_ref[...] = _dot(xn, w_ref[...]).astype(kv_ref.dtype)
    kro_ref[...] = _rope_pair(kr_ref[...].astype(F32), cos_ref[...], sin_ref[...]).astype(kro_ref.dtype)


def mla_kv_proj(h_in, gain, w_ukv, cos_t, sin_t, *, tm):
    m = h_in.shape[0]
    tm = min(tm, m)
    n = w_ukv.shape[1]
    return pl.pallas_call(
        _kvproj_kernel,
        out_shape=[jax.ShapeDtypeStruct((m, n), BF16), jax.ShapeDtypeStruct((m, LANES), BF16)],
        grid=(m // tm,),
        in_specs=[pl.BlockSpec((tm, MLA_KV_LORA), lambda i: (i, COL_CKV // MLA_KV_LORA)),
                  pl.BlockSpec((1, MLA_KV_LORA), lambda i: (0, 0)),
                  pl.BlockSpec(w_ukv.shape, lambda i: (0, 0)),
                  pl.BlockSpec((tm, LANES), lambda i: (i, COL_KR // LANES)),
                  pl.BlockSpec((tm, LANES), lambda i: (i, 0)),
                  pl.BlockSpec((tm, LANES), lambda i: (i, 0))],
        out_specs=[pl.BlockSpec((tm, n), lambda i: (i, 0)),
                   pl.BlockSpec((tm, LANES), lambda i: (i, 0))],
        compiler_params=_cparams("parallel"),
        name="mla_kv_proj",
    )(h_in, gain.reshape(1, -1).astype(F32), w_ukv, h_in, cos_t, sin_t)


ATTN_UNROLL = 16


def _mla_attn_kernel(qi_tab, kj_tab, q_ref, kv_ref, kr_ref, o_ref, s0_scr, s1_scr, m_scr, acc_scr,
                     *, blk, nq, n_items):
    ones = jnp.ones((blk, LANES), BF16)
    s_scr = (s0_scr, s1_scr)

    def scores(t, s_ref):
        q_rows = pl.ds(pl.multiple_of(qi_tab[t] * blk, blk), blk)
        k_rows = pl.ds(pl.multiple_of(kj_tab[t] * blk, blk), blk)
        k = jnp.concatenate([kv_ref[k_rows, :MLA_NOPE], kr_ref[k_rows, :]], axis=1)
        s_ref[...] = _dot_nt(q_ref[q_rows, :], k)

    def finish(t, s_ref, diagonal):
        qi = qi_tab[t]
        k_rows = pl.ds(pl.multiple_of(kj_tab[t] * blk, blk), blk)
        v = jnp.concatenate([kv_ref[k_rows, MLA_NOPE:], ones], axis=1)
        s = s_ref[...]
        if diagonal:
            row = lax.broadcasted_iota(I32, (blk, blk), 0)
            col = lax.broadcasted_iota(I32, (blk, blk), 1)
            s = jnp.where(col <= row, s, NEG)
            m_new = jnp.max(s, axis=1, keepdims=True) + jnp.zeros((blk, LANES), F32)
        else:
            m = m_scr[qi]
            m_new = jnp.maximum(m, jnp.max(s, axis=1, keepdims=True))
        p = jnp.exp2(s - jnp.concatenate([m_new] * (blk // LANES), axis=1)).astype(BF16)
        pv = _dot(p, v)
        if diagonal:
            acc_scr[qi] = pv
        else:
            alpha = jnp.exp2(m - m_new)
            acc_scr[qi] = jnp.concatenate([alpha, alpha], axis=1) * acc_scr[qi] + pv
        m_scr[qi] = m_new

    def run(t0, n, slot, diagonal):
        for u in range(n):
            scores(jnp.minimum(t0 + u + 1, n_items - 1), s_scr[(slot + u + 1) % 2])
            finish(t0 + u, s_scr[(slot + u) % 2], diagonal)

    def phase(t0, n, slot, diagonal):
        def body(i, carry):
            run(t0 + i * ATTN_UNROLL, ATTN_UNROLL, slot, diagonal)
            return carry

        lax.fori_loop(0, n // ATTN_UNROLL, body, 0)
        if n % ATTN_UNROLL:
            run(t0 + n - n % ATTN_UNROLL, n % ATTN_UNROLL, slot, diagonal)
        return (slot + n) % 2

    scores(0, s0_scr)
    slot = phase(0, nq, 0, True)
    phase(nq, n_items - nq, slot, False)

    def normalise(qi, carry):
        acc = acc_scr[qi]
        o_ref[pl.ds(pl.multiple_of(qi * blk, blk), blk), :] = (acc[:, :MLA_V] / acc[:, MLA_V:]).astype(o_ref.dtype)
        return carry

    lax.fori_loop(0, nq, normalise, 0)


def mla_attention(q_full, kv, k_rope, *, batch, seq, blk):
    blk = min(blk, seq)
    nq = seq // blk
    m = batch * seq
    pairs = [(i, i) for i in range(nq)] + [(i, i - d) for d in range(1, nq) for i in range(d, nq)]
    qi_tab = jnp.asarray([p[0] for p in pairs], I32)
    kj_tab = jnp.asarray([p[1] for p in pairs], I32)
    grid_spec = pltpu.PrefetchScalarGridSpec(
        num_scalar_prefetch=2,
        grid=(batch, MLA_HEADS),
        in_specs=[pl.BlockSpec((seq, 256), lambda b, h, *_: (b, h)),
                  pl.BlockSpec((seq, 256), lambda b, h, *_: (b, h)),
                  pl.BlockSpec((seq, LANES), lambda b, h, *_: (b, 0))],
        out_specs=pl.BlockSpec((seq, MLA_V), lambda b, h, *_: (b, h)),
        scratch_shapes=[pltpu.VMEM((blk, blk), F32), pltpu.VMEM((blk, blk), F32),
                        pltpu.VMEM((nq, blk, LANES), F32), pltpu.VMEM((nq, blk, 2 * MLA_V), F32)],
    )
    return pl.pallas_call(
        functools.partial(_mla_attn_kernel, blk=blk, nq=nq, n_items=len(pairs)),
        out_shape=jax.ShapeDtypeStruct((m, MLA_WIDTH), BF16),
        grid_spec=grid_spec,
        compiler_params=_cparams("parallel", "parallel"),
        name="mla_attention",
    )(qi_tab, kj_tab, q_full, kv, k_rope)


GLA_HEADS_PER_STEP = 1


def _gla_kernel(q_ref, k_ref, v_ref, og_ref, lr_ref, wg_ref, bg_ref, gn_ref, o_ref, st_ref, *, sb, c):
    @pl.when(pl.program_id(2) == 0)
    def _():
        st_ref[...] = jnp.zeros_like(st_ref)

    z = _dot(lr_ref[...], wg_ref[...]) + bg_ref[...]
    log_a = (jnp.minimum(z, 0.0) - jnp.log(1.0 + jnp.exp(-jnp.abs(z)))) * (1.0 / GLA_TAU)
    rows = lax.broadcasted_iota(I32, (c, c), 0)
    cols = lax.broadcasted_iota(I32, (c, c), 1)
    causal = cols <= rows
    tril = causal.astype(F32)
    gn = gn_ref[...]
    scale = GLA_HEAD_K ** -0.5

    for n in range(sb // c):
        sl = slice(n * c, (n + 1) * c)
        for hh in range(GLA_HEADS_PER_STEP):
            kcols = slice(hh * GLA_HEAD_K, (hh + 1) * GLA_HEAD_K)
            vcols = slice(hh * GLA_HEAD_V, (hh + 1) * GLA_HEAD_V)
            g = log_a[sl, kcols]
            b = jnp.dot(tril, g, preferred_element_type=F32, precision=lax.Precision.HIGHEST)
            b_last = b[c - 1:c, :]
            q = q_ref[sl, kcols].astype(F32)
            k = k_ref[sl, kcols].astype(F32)
            v = v_ref[sl, vcols]
            qe = (q * scale * jnp.exp(b)).astype(BF16)
            ke = (k * jnp.exp(-b)).astype(BF16)
            kd = (k * jnp.exp(b_last - b)).astype(BF16)
            att = jnp.where(causal, _dot_nt(qe, ke), 0.0).astype(BF16)
            st = st_ref[hh]
            o = _dot(att, v) + _dot_nt(qe, st.astype(BF16))
            st_ref[hh] = st * jnp.exp(b_last) + _dot_tn(v, kd)
            og = og_ref[sl, vcols].astype(F32)
            o = _rms(o, gn) * (og * (1.0 / (1.0 + jnp.exp(-og))))
            o_ref[sl, vcols] = o.astype(o_ref.dtype)


def gla_mixer(h_in, w_gate_pad, b_gate, gla_out_norm, *, batch, seq, sb):
    sb = min(sb, seq)
    nsb = seq // sb
    m = batch * seq
    g = GLA_HEADS_PER_STEP
    kw, vw = g * GLA_HEAD_K, g * GLA_HEAD_V
    return pl.pallas_call(
        functools.partial(_gla_kernel, sb=sb, c=GLA_CHUNK),
        out_shape=jax.ShapeDtypeStruct((m, GLA_V_WIDTH), BF16),
        grid=(batch, GLA_HEADS // g, nsb),
        in_specs=[pl.BlockSpec((sb, kw), lambda b, h, s: (b * nsb + s, COL_GQ // kw + h)),
                  pl.BlockSpec((sb, kw), lambda b, h, s: (b * nsb + s, COL_GK // kw + h)),
                  pl.BlockSpec((sb, vw), lambda b, h, s: (b * nsb + s, COL_GV // vw + h)),
                  pl.BlockSpec((sb, vw), lambda b, h, s: (b * nsb + s, COL_OG // vw + h)),
                  pl.BlockSpec((sb, LANES), lambda b, h, s: (b * nsb + s, COL_LR // LANES)),
                  pl.BlockSpec((LANES, kw), lambda b, h, s: (0, h)),
                  pl.BlockSpec((1, kw), lambda b, h, s: (0, h)),
                  pl.BlockSpec((1, GLA_HEAD_V), lambda b, h, s: (0, 0))],
        out_specs=pl.BlockSpec((sb, vw), lambda b, h, s: (b * nsb + s, h)),
        scratch_shapes=[pltpu.VMEM((g, GLA_HEAD_V, GLA_HEAD_K), F32)],
        compiler_params=_cparams("parallel", "parallel", "arbitrary"),
        name="gla_mixer",
    )(h_in, h_in, h_in, h_in, h_in, w_gate_pad, b_gate.reshape(1, -1).astype(F32),
      gla_out_norm.reshape(1, -1).astype(F32))


def _out_proj_kernel(om_ref, g_ref, og_ref, w_ref, x_ref, o_ref, mix_ref):
    @pl.when(pl.program_id(1) == 0)
    def _():
        mix_ref[:, :MLA_WIDTH] = _rms(om_ref[...].astype(F32), g_ref[...]).astype(BF16)
        mix_ref[:, MLA_WIDTH:] = og_ref[...]

    o_ref[...] = x_ref[...] + _dot(mix_ref[...], w_ref[...])


def mixer_out_proj(o_mla, mla_out_norm, o_gla, w_out, x, *, tm, tn):
    m, d = x.shape
    tm = min(tm, m)
    tn = min(tn, d)
    kmix = MLA_WIDTH + GLA_V_WIDTH
    return pl.pallas_call(
        _out_proj_kernel,
        out_shape=jax.ShapeDtypeStruct((m, d), F32),
        grid=(m // tm, d // tn),
        in_specs=[pl.BlockSpec((tm, MLA_WIDTH), lambda i, j: (i, 0)),
                  pl.BlockSpec((1, MLA_WIDTH), lambda i, j: (0, 0)),
                  pl.BlockSpec((tm, GLA_V_WIDTH), lambda i, j: (i, 0)),
                  pl.BlockSpec((kmix, tn), lambda i, j: (0, j)),
                  pl.BlockSpec((tm, tn), lambda i, j: (i, j))],
        out_specs=pl.BlockSpec((tm, tn), lambda i, j: (i, j)),
        scratch_shapes=[pltpu.VMEM((tm, kmix), BF16)],
        compiler_params=_cparams("parallel", "arbitrary"),
        name="mixer_out_proj",
    )(o_mla, mla_out_norm.reshape(1, -1).astype(F32), o_gla, w_out, x)


def _xattn_kernel(q_ref, kv_ref, w_ref, h_ref, o_ref):
    scale = XATTN_DIM ** -0.5
    outs = []
    for h in range(XATTN_HEADS):
        q = q_ref[:, h * XATTN_DIM:(h + 1) * XATTN_DIM]
        k = kv_ref[:, h * XATTN_DIM:(h + 1) * XATTN_DIM]
        v = kv_ref[:, XATTN_WIDTH + h * XATTN_DIM: XATTN_WIDTH + (h + 1) * XATTN_DIM]
        s = _dot_nt(q, k) * scale
        p = jnp.exp(s - jnp.max(s, axis=1, keepdims=True))
        p = p / jnp.sum(p, axis=1, keepdims=True)
        outs.append(_dot(p.astype(BF16), v).astype(BF16))
    o = jnp.concatenate(outs, axis=1)
    o_ref[...] = h_ref[...] + _dot(o, w_ref[...])


def cross_attention(qx, kvm, w_co, h1, *, batch, seq, mem_tokens, tm):
    m, d = h1.shape
    tm = min(tm, seq)
    per_b = seq // tm
    return pl.pallas_call(
        _xattn_kernel,
        out_shape=jax.ShapeDtypeStruct((m, d), F32),
        grid=(m // tm,),
        in_specs=[pl.BlockSpec((tm, XATTN_WIDTH), lambda i: (i, 0)),
                  pl.BlockSpec((mem_tokens, 2 * XATTN_WIDTH), lambda i: (i // per_b, 0)),
                  pl.BlockSpec(w_co.shape, lambda i: (0, 0), pipeline_mode=pl.Buffered(1)),
                  pl.BlockSpec((tm, d), lambda i: (i, 0))],
        out_specs=pl.BlockSpec((tm, d), lambda i: (i, 0)),
        compiler_params=_cparams("parallel"),
        name="cross_attention",
    )(qx, kvm, w_co, h1)


def _order_key(x):
    b = lax.bitcast_convert_type(x, I32)
    return b ^ (lax.shift_right_arithmetic(b, 31) & 0x7FFFFFFF)


def _order_key_inv(k):
    return lax.bitcast_convert_type(k ^ (lax.shift_right_arithmetic(k, 31) & 0x7FFFFFFF), F32)


KEY_BIAS = 0x20000000


def _pack_keys(x, bits):
    low = (1 << bits) - 1
    row = lax.broadcasted_iota(I32, x.shape, 0)
    k = (_order_key(x) & ~((low << 2) | 3)) | ((low - row) << 2)
    return lax.bitcast_convert_type(lax.shift_right_arithmetic(k, 2) + KEY_BIAS, F32)


def _unpack_keys(kf, bits):
    low = (1 << bits) - 1
    k = lax.shift_left(lax.bitcast_convert_type(kf, I32) - KEY_BIAS, 2)
    return _order_key_inv(k & ~((low << 2) | 3)), low - (lax.shift_right_arithmetic(k, 2) & low)


SUBLANES = 8


def _max_rows(k3):
    m8 = jnp.max(k3, axis=0)
    for shift in (4, 2, 1):
        m8 = jnp.maximum(m8, pltpu.roll(m8, shift, axis=0))
    return m8


def _top16_rows(s, bits):
    rows, t = s.shape
    k3 = _pack_keys(s, bits).reshape(rows // SUBLANES, SUBLANES, t)
    vals, idxs = [], []
    for _ in range(PEER_TOPK):
        m8 = _max_rows(k3)
        val, idx = _unpack_keys(m8[0:1, :], bits)
        vals.append(val)
        idxs.append(idx)
        k3 = jnp.where(k3 == m8[None], 0.0, k3)
    return vals, idxs


_PAIRS = [(a, b) for a in range(PEER_TOPK) for b in range(PEER_TOPK) if (a + 1) * (b + 1) <= PEER_TOPK]
_PAIR_ROWS = -(-len(_PAIRS) // 8) * 8


TOPK_HEADS_PER_TRIP = 2


def _peer_topk_kernel(q_ref, keys_ref, eid_ref, gate_ref, eid_t, gate_t, *, tm):
    def head(h):
        tops = []
        for p in range(2):
            col = pl.multiple_of((2 * h + p) * PEER_HALF, PEER_HALF)
            qh = q_ref[:, pl.ds(col, PEER_HALF)]
            s = _dot_nt(keys_ref[2 * h + p], qh)
            tops.append(_top16_rows(s, 7))
        (s1, i1), (s2, i2) = tops
        cand_s = [s1[a] + s2[b] for a, b in _PAIRS]
        cand_e = [i1[a] * PEER_KEYS + i2[b] for a, b in _PAIRS]
        pad = _PAIR_ROWS - len(_PAIRS)
        cs = jnp.concatenate(cand_s + [jnp.full((pad, tm), NEG, F32)], axis=0)
        ce = jnp.concatenate(cand_e + [jnp.zeros((pad, tm), I32)], axis=0)
        k3 = _pack_keys(cs, 6).reshape(_PAIR_ROWS // SUBLANES, SUBLANES, tm)
        ce3 = ce.astype(F32).reshape(_PAIR_ROWS // SUBLANES, SUBLANES, tm)
        sel_s, sel_e = [], []
        for _ in range(PEER_TOPK):
            m8 = _max_rows(k3)
            hit = k3 == m8[None]
            sel_s.append(_unpack_keys(m8[0:1, :], 6)[0])
            sel_e.append(_max_rows(jnp.where(hit, ce3, -1.0))[0:1, :].astype(I32))
            k3 = jnp.where(hit, 0.0, k3)
        top_s = jnp.concatenate(sel_s, axis=0)
        top_e = jnp.concatenate(sel_e, axis=0)
        w = jnp.exp(top_s - top_s[0:1, :])
        gate = w / jnp.sum(w, axis=0, keepdims=True)
        row = pl.multiple_of(h * PEER_TOPK, PEER_TOPK)
        eid_t[pl.ds(row, PEER_TOPK), :] = top_e
        gate_t[pl.ds(row, PEER_TOPK), :] = gate

    def trip(i, carry):
        for u in range(TOPK_HEADS_PER_TRIP):
            head(i * TOPK_HEADS_PER_TRIP + u)
        return carry

    lax.fori_loop(0, PEER_HEADS // TOPK_HEADS_PER_TRIP, trip, 0)
    eid_ref[...] = eid_t[...].T
    gate_ref[...] = gate_t[...].T


def peer_topk(qp, keys, *, tm):
    m = qp.shape[0]
    tm = min(tm, m)
    return pl.pallas_call(
        functools.partial(_peer_topk_kernel, tm=tm),
        out_shape=[jax.ShapeDtypeStruct((m, PEER_SLOTS), I32), jax.ShapeDtypeStruct((m, PEER_SLOTS), F32),
                   jax.ShapeDtypeStruct((PEER_SLOTS, m), I32)],
        grid=(m // tm,),
        in_specs=[pl.BlockSpec((tm, qp.shape[1]), lambda i: (i, 0)),
                  pl.BlockSpec(keys.shape, lambda i: (0, 0, 0))],
        out_specs=[pl.BlockSpec((tm, PEER_SLOTS), lambda i: (i, 0)),
                   pl.BlockSpec((tm, PEER_SLOTS), lambda i: (i, 0)),
                   pl.BlockSpec((PEER_SLOTS, tm), lambda i: (0, i))],
        scratch_shapes=[pltpu.VMEM((PEER_SLOTS, tm), F32)],
        compiler_params=_cparams("parallel"),
        name="peer_topk",
    )(qp, keys)


def _peer_gates_kernel(eid_ref, eidt_ref, gate_ref, o_ref, g3_ref, *, tm, pitch):
    sub = lax.broadcasted_iota(I32, (PEER_KEYS, PEER_SLOTS), 0)
    lane = lax.broadcasted_iota(I32, (PEER_SLOTS, PEER_KEYS), 1).astype(F32).astype(BF16)
    lo_all = jnp.bitwise_and(eidt_ref[...], PEER_KEYS - 1).astype(F32).astype(BF16)
    one = jnp.ones((), BF16)
    zero = jnp.zeros((), BF16)
    for t in range(tm):
        e = eid_ref[t:t + 1, :]
        hi = lax.shift_right_logical(e, 7)
        a1 = jnp.where(sub == hi, gate_ref[t:t + 1, :], 0.0).astype(BF16)
        a2 = jnp.where(lane == lo_all[:, t:t + 1], one, zero)
        g3_ref[pl.ds(t, PEER_KEYS, stride=pitch), :] = _dot(a1, a2)
    for e1 in range(PEER_KEYS):
        o_ref[:, e1 * PEER_KEYS:(e1 + 1) * PEER_KEYS] = g3_ref[e1 * pitch:e1 * pitch + tm, :].astype(o_ref.dtype)


def peer_gates(eid, eid_t, gate, *, tm):
    m = eid.shape[0]
    tm = min(tm, m)
    pitch = tm + 8
    return pl.pallas_call(
        functools.partial(_peer_gates_kernel, tm=tm, pitch=pitch),
        out_shape=jax.ShapeDtypeStruct((m, PEER_EXPERTS), BF16),
        grid=(m // tm,),
        in_specs=[pl.BlockSpec((tm, PEER_SLOTS), lambda i: (i, 0)),
                  pl.BlockSpec((PEER_SLOTS, tm), lambda i: (0, i)),
                  pl.BlockSpec((tm, PEER_SLOTS), lambda i: (i, 0))],
        out_specs=pl.BlockSpec((tm, PEER_EXPERTS), lambda i: (i, 0)),
        scratch_shapes=[pltpu.VMEM((PEER_KEYS * pitch, PEER_KEYS), F32)],
        compiler_params=_cparams("parallel"),
        name="peer_gates",
    )(eid, eid_t, gate)


def _peer_dense_kernel(x_ref, sx_ref, u_ref, su_ref, g_ref, v_ref, sv_ref, h_ref, gn_ref, o_ref, aq_scr, sa_scr,
                       *, final_norm, n_blocks):
    j = pl.program_id(1)
    nj = n_blocks

    def step(p, do_values, do_gates):
        if do_gates:
            s = _dot_nt(x_ref[...], u_ref[...])
        if do_values:
            pv = _dot(aq_scr[1 - p], v_ref[...])
        if do_gates:
            s = s * sx_ref[...] * su_ref[...]
            a = 0.5 * s * (1.0 + lax.erf(s * (2.0 ** -0.5)))
            aq_scr[p], sa_scr[p] = _quantize_rows(a * g_ref[...].astype(F32) * sv_ref[...])
        if do_values:
            o_ref[...] += sa_scr[1 - p] * pv

    half = o_ref.shape[1] // 2

    @pl.when(j == 0)
    def _():
        o_ref[:, :half] = h_ref[...]
        o_ref[:, half:] = jnp.zeros((o_ref.shape[0], half), F32)
        step(0, False, True)

    @pl.when(j == 1)
    def _():
        o_ref[:, half:] += h_ref[...]

    for p in range(2):
        @pl.when(jnp.logical_and(jnp.logical_and(j > 0, j < nj), j % 2 == p))
        def _(p=p):
            step(p, True, True)

    @pl.when(j == nj)
    def _():
        step(nj % 2, True, False)
        if final_norm:
            o_ref[...] = _rms(o_ref[...], gn_ref[...])


def peer_dense(xq, sx, uq, su, g, vq, sv, h, gain, *, final_norm, tm, te):
    m, d = xq.shape
    e = uq.shape[0]
    tm = min(tm, m)
    nj = e // te
    def cur(j):
        return jnp.minimum(j, nj - 1)

    def prev(j):
        return jnp.maximum(j - 1, 0)

    return pl.pallas_call(
        functools.partial(_peer_dense_kernel, final_norm=final_norm, n_blocks=nj),
        out_shape=jax.ShapeDtypeStruct((m, d), F32),
        grid=(m // tm, nj + 1),
        in_specs=[pl.BlockSpec((tm, d), lambda i, j: (i, 0)),
                  pl.BlockSpec((tm, 1), lambda i, j: (i, 0)),
                  pl.BlockSpec((te, d), lambda i, j: (cur(j), 0)),
                  pl.BlockSpec((1, te), lambda i, j: (0, cur(j))),
                  pl.BlockSpec((tm, te), lambda i, j: (i, cur(j))),
                  pl.BlockSpec((te, d), lambda i, j: (prev(j), 0)),
                  pl.BlockSpec((1, te), lambda i, j: (0, cur(j))),
                  pl.BlockSpec((tm, d // 2), lambda i, j: (i, jnp.minimum(j, 1))),
                  pl.BlockSpec((1, d), lambda i, j: (0, 0))],
        out_specs=pl.BlockSpec((tm, d), lambda i, j: (i, 0)),
        scratch_shapes=[pltpu.VMEM((2, tm, te), F8), pltpu.VMEM((2, tm, 1), F32)],
        compiler_params=_cparams("parallel", "arbitrary"),
        name="peer_dense",
    )(xq, sx, uq, su.reshape(1, e), g, vq, sv.reshape(1, e), h, gain.reshape(1, d).astype(F32))


def _w_in_columns():
    src = {}
    off = 0
    for name, width in (("cq", MLA_Q_LORA), ("ckv", MLA_KV_LORA), ("kr", MLA_ROPE), ("gq", GLA_K_WIDTH),
                        ("gk", GLA_K_WIDTH), ("gv", GLA_V_WIDTH), ("lr", GLA_RANK), ("og", GLA_V_WIDTH)):
        src[name] = np.arange(off, off + width)
        off += width
    cols = np.full((IN_PAD,), -1, np.int64)
    for name, start in (("cq", COL_CQ), ("ckv", COL_CKV), ("gq", COL_GQ), ("gk", COL_GK),
                        ("gv", COL_GV), ("og", COL_OG), ("lr", COL_LR)):
        cols[start:start + len(src[name])] = src[name]
    half = MLA_ROPE // 2
    x1, x2 = src["kr"][:half], src["kr"][half:]
    cols[COL_KR:COL_KR + LANES] = np.concatenate([x1, x1, x2, x2])
    return cols


def _w_uq_columns():
    half = MLA_ROPE // 2
    nope = [h * MLA_QK + d for h in range(MLA_HEADS) for d in range(MLA_NOPE)]
    rope = []
    for j in range(MLA_HEADS // 2):
        a, b = 2 * j, 2 * j + 1
        for part in (0, 1):
            for h in (a, b):
                rope += [h * MLA_QK + MLA_NOPE + part * half + r for r in range(half)]
    return np.array(nope + rope, np.int64)


def _column_runs(cols):
    runs = []
    start = 0
    for i in range(1, len(cols) + 1):
        if i == len(cols) or cols[i] != (cols[i - 1] + 1 if cols[i - 1] >= 0 else -1):
            runs.append((int(cols[start]), i - start))
            start = i
    return runs


def _take_cols(w, cols, dtype):
    pieces = [jnp.zeros((w.shape[0], n), dtype) if src < 0 else w[:, src:src + n].astype(dtype)
              for src, n in _column_runs(cols)]
    return jnp.concatenate(pieces, axis=1)


def _take_rows(wt, rows, dtype):
    pieces = [jnp.zeros((n, wt.shape[1]), dtype) if src < 0 else wt[src:src + n].astype(dtype)
              for src, n in _column_runs(rows)]
    return jnp.concatenate(pieces, axis=0)


def _rope_tables(positions):
    half = MLA_ROPE // 2
    inv_freq = ROPE_THETA ** (-jnp.arange(0, MLA_ROPE, 2, dtype=F32) / MLA_ROPE)
    ang = positions.astype(F32).reshape(-1, 1) * inv_freq
    cos = jnp.cos(ang)
    sin = jnp.sin(ang)
    return jnp.tile(cos, (1, 4)), jnp.concatenate([-sin, -sin, sin, sin], axis=1)


def kernel(x, mem, positions, norm_mem, norm_mix, w_in, mla_q_norm, w_uq, mla_kv_norm, w_ukv, mla_out_norm,
           w_gate_up, b_gate, gla_out_norm, w_out, norm_cross, w_cq, w_ck, w_cv, w_co, norm_ffn, w_peer_q,
           peer_sub_keys, peer_u, peer_v, norm_final):
    batch, seq, d = x.shape
    mem_tokens = mem.shape[1]
    m = batch * seq
    h = x.reshape(m, d)
    cos_t, sin_t = _rope_tables(positions)

    t = TILES
    depth = norm_mix.shape[0]
    for l in range(depth):
        w_in_t = _take_rows(jnp.transpose(w_in[l]), _w_in_columns(), BF16)
        h_in = norm_matmul(h, norm_mix[l], w_in_t, k=d, tm=t.rows, tn=t.cols, out_dtype=BF16, w_transposed=True,
                           name="in_proj")

        w_uq_p = _take_cols(w_uq[l], _w_uq_columns(), BF16)
        q_full = mla_q_proj(h_in, mla_q_norm[l], w_uq_p, cos_t, sin_t, tm=t.rows)
        kv, k_rope = mla_kv_proj(h_in, mla_kv_norm[l], w_ukv[l].astype(BF16), cos_t, sin_t, tm=t.rows)
        o_mla = mla_attention(q_full, kv, k_rope, batch=batch, seq=seq, blk=t.attn_block)

        w_gate_pad = jnp.zeros((LANES, GLA_K_WIDTH), F32).at[:GLA_RANK].set(w_gate_up[l]).astype(BF16)
        o_gla = gla_mixer(h_in, w_gate_pad, b_gate[l], gla_out_norm[l], batch=batch, seq=seq, sb=t.gla_rows)

        h = mixer_out_proj(o_mla, mla_out_norm[l], o_gla, w_out[l].astype(BF16), h, tm=t.rows, tn=t.cols)

        mn_kv = norm_matmul(mem.reshape(batch * mem_tokens, d), norm_mem,
                            jnp.concatenate([w_ck[l], w_cv[l]], axis=1).astype(BF16),
                            k=d, tm=t.rows, tn=t.cols, out_dtype=BF16, name="mem_kv_proj")
        qx = norm_matmul(h, norm_cross[l], w_cq[l].astype(BF16), k=d, tm=t.rows, tn=XATTN_WIDTH, out_dtype=BF16,
                         name="xattn_q_proj")
        h = cross_attention(qx, mn_kv, w_co[l].astype(BF16), h, batch=batch, seq=seq,
                            mem_tokens=mem_tokens, tm=t.rows)

        qp, xq, sx = norm_matmul(h, norm_ffn[l], w_peer_q[l].astype(BF16), k=d, tm=t.peer_q_rows,
                                 tn=PEER_HEADS * 2 * PEER_HALF, out_dtype=BF16, emit_xq=True, name="peer_q_proj")
        keys = peer_sub_keys[l].reshape(PEER_HEADS * 2, PEER_KEYS, PEER_HALF).astype(BF16)
        eid, gate, eid_t = peer_topk(qp, keys, tm=t.topk_tokens)
        g = peer_gates(eid, eid_t, gate, tm=t.gate_tokens)
        uq, su = quantize_rows(peer_u[l], tr=t.rows, name="peer_u_quant")
        vq, sv = quantize_rows(peer_v[l], tr=t.rows, name="peer_v_quant")
        h = peer_dense(xq, sx, uq, su, g, vq, sv, h, norm_final, final_norm=(l + 1 == depth),
                       tm=t.rows, te=t.peer_experts)
    return h.reshape(batch, seq, d)
```

```python
import functools
import math
from typing import NamedTuple

import numpy as np
import jax
import jax.numpy as jnp
from jax import lax
from jax.experimental import pallas as pl
from jax.experimental.pallas import tpu as pltpu

F32 = jnp.float32
BF16 = jnp.bfloat16
F8 = jnp.float8_e4m3fn
I32 = jnp.int32
FP8_AMAX = 256.0

EPS = 1e-6
ROPE_THETA = 10000.0

MLA_HEADS = 16
MLA_Q_LORA = 1024
MLA_KV_LORA = 512
MLA_NOPE = 128
MLA_ROPE = 64
MLA_QK = MLA_NOPE + MLA_ROPE
MLA_V = 128
MLA_WIDTH = MLA_HEADS * MLA_V

GLA_HEADS = 4
GLA_HEAD_K = 256
GLA_HEAD_V = 512
GLA_K_WIDTH = GLA_HEADS * GLA_HEAD_K
GLA_V_WIDTH = GLA_HEADS * GLA_HEAD_V
GLA_RANK = 16
GLA_TAU = 16.0
GLA_CHUNK = 64

XATTN_HEADS = 4
XATTN_DIM = 256
XATTN_WIDTH = XATTN_HEADS * XATTN_DIM

PEER_HEADS = 8
PEER_KEYS = 128
PEER_HALF = 128
PEER_TOPK = 16
PEER_SLOTS = PEER_HEADS * PEER_TOPK
PEER_EXPERTS = PEER_KEYS * PEER_KEYS

LANES = 128
VMEM_LIMIT = 56 * 1024 * 1024


class _Tiles(NamedTuple):
    rows: int = 512
    cols: int = 1024
    attn_block: int = 512
    gla_rows: int = 1024
    peer_q_rows: int = 256
    topk_tokens: int = 256
    gate_tokens: int = 128
    peer_experts: int = 1024


TILES = _Tiles()

COL_CQ = 0
COL_CKV = COL_CQ + MLA_Q_LORA
COL_KR = COL_CKV + MLA_KV_LORA
COL_LR = COL_KR + LANES
COL_GQ = 2048
COL_GK = COL_GQ + GLA_K_WIDTH
COL_GV = COL_GK + GLA_K_WIDTH
COL_OG = COL_GV + GLA_V_WIDTH
IN_PAD = COL_OG + GLA_V_WIDTH

NEG = -1e30


def _cparams(*sem):
    return pltpu.CompilerParams(dimension_semantics=sem, vmem_limit_bytes=VMEM_LIMIT)


def _rms(x, g):
    ms = jnp.mean(x * x, axis=-1, keepdims=True)
    return x * lax.rsqrt(ms + EPS) * g


def _dot(a, b):
    return jnp.dot(a, b, preferred_element_type=F32)


def _dot_nt(a, b):
    return lax.dot_general(a, b, (((1,), (1,)), ((), ())), preferred_element_type=F32)


def _dot_tn(a, b):
    return lax.dot_general(a, b, (((0,), (0,)), ((), ())), preferred_element_type=F32)


def _quantize_rows(x):
    amax = jnp.max(jnp.abs(x), axis=-1, keepdims=True)
    scale = jnp.where(amax > 0.0, amax * (1.0 / FP8_AMAX), 1.0)
    return (x * (1.0 / scale)).astype(F8), scale


def _norm_matmul_kernel(x_ref, g_ref, w_ref, o_ref, *rest, emit_xq, whole_n, w_transposed):
    if emit_xq:
        xq_ref, sx_ref, xn_ref = rest
    else:
        (xn_ref,) = rest

    def normalise():
        xn = _rms(x_ref[...].astype(F32), g_ref[...])
        xn_ref[...] = xn.astype(BF16)
        if emit_xq:
            xq_ref[...], sx_ref[...] = _quantize_rows(xn)

    if whole_n:
        normalise()
    else:
        pl.when(pl.program_id(1) == 0)(normalise)
    mm = _dot_nt if w_transposed else _dot
    o_ref[...] = mm(xn_ref[...], w_ref[...]).astype(o_ref.dtype)


def norm_matmul(x, gain, w, *, k, x_col_blk=0, tm, tn, out_dtype, emit_xq=False, w_transposed=False, name):
    m = x.shape[0]
    n = w.shape[0] if w_transposed else w.shape[1]
    tm = min(tm, m)
    tn = min(tn, n)
    out_shape = [jax.ShapeDtypeStruct((m, n), out_dtype)]
    out_specs = [pl.BlockSpec((tm, tn), lambda i, j: (i, j))]
    if emit_xq:
        out_shape += [jax.ShapeDtypeStruct((m, k), F8), jax.ShapeDtypeStruct((m, 1), F32)]
        out_specs += [pl.BlockSpec((tm, k), lambda i, j: (i, 0)), pl.BlockSpec((tm, 1), lambda i, j: (i, 0))]
    whole_n = tn == n
    w_mode = pl.Buffered(1) if whole_n else None
    w_spec = (pl.BlockSpec((tn, k), lambda i, j: (j, 0), pipeline_mode=w_mode) if w_transposed
              else pl.BlockSpec((k, tn), lambda i, j: (0, j), pipeline_mode=w_mode))
    res = pl.pallas_call(
        functools.partial(_norm_matmul_kernel, emit_xq=emit_xq, whole_n=whole_n, w_transposed=w_transposed),
        out_shape=out_shape,
        grid=(m // tm, n // tn),
        in_specs=[pl.BlockSpec((tm, k), lambda i, j: (i, x_col_blk)),
                  pl.BlockSpec((1, k), lambda i, j: (0, 0)),
                  w_spec],
        out_specs=out_specs,
        scratch_shapes=[pltpu.VMEM((tm, k), BF16)],
        compiler_params=_cparams("parallel", "arbitrary"),
        name=name,
    )(x, gain.reshape(1, k).astype(F32), w)
    return res if emit_xq else res[0]


def _quantize_kernel(w_ref, q_ref, s_ref):
    q_ref[...], s_ref[...] = _quantize_rows(w_ref[...])


def quantize_rows(w, *, tr, name):
    r, d = w.shape
    return pl.pallas_call(
        _quantize_kernel,
        out_shape=[jax.ShapeDtypeStruct((r, d), F8), jax.ShapeDtypeStruct((r, 1), F32)],
        grid=(r // tr,),
        in_specs=[pl.BlockSpec((tr, d), lambda i: (i, 0))],
        out_specs=[pl.BlockSpec((tr, d), lambda i: (i, 0)), pl.BlockSpec((tr, 1), lambda i: (i, 0))],
        compiler_params=_cparams("parallel"),
        name=name,
    )(w)


def _rope_pair(r, cosv, sinv):
    return r * cosv + pltpu.roll(r, 2 * 32, axis=1) * sinv


def _qproj_kernel(c_ref, g_ref, w_ref, cos_ref, sin_ref, o_ref, *, scale):
    xn = _rms(c_ref[...].astype(F32), g_ref[...]).astype(BF16)
    q = _dot(xn, w_ref[...])
    cosv = cos_ref[...]
    sinv = sin_ref[...]
    lane = lax.broadcasted_iota(I32, (1, LANES), 1)
    rope_base = MLA_HEADS * MLA_NOPE
    for j in range(MLA_HEADS // 2):
        r = q[:, rope_base + j * LANES: rope_base + (j + 1) * LANES]
        r = _rope_pair(r, cosv, sinv) * scale
        for p in range(2):
            h = 2 * j + p
            own = ((lane // 32) % 2) == p
            o_ref[:, h * 256: h * 256 + 128] = (q[:, h * 128:(h + 1) * 128] * scale).astype(o_ref.dtype)
            o_ref[:, h * 256 + 128: h * 256 + 256] = jnp.where(own, r, 0.0).astype(o_ref.dtype)


def mla_q_proj(h_in, gain, w_uq_perm, cos_t, sin_t, *, tm):
    m = h_in.shape[0]
    tm = min(tm, m)
    return pl.pallas_call(
        functools.partial(_qproj_kernel, scale=MLA_QK ** -0.5 * math.log2(math.e)),
        out_shape=jax.ShapeDtypeStruct((m, MLA_HEADS * 256), BF16),
        grid=(m // tm,),
        in_specs=[pl.BlockSpec((tm, MLA_Q_LORA), lambda i: (i, COL_CQ // MLA_Q_LORA)),
                  pl.BlockSpec((1, MLA_Q_LORA), lambda i: (0, 0)),
                  pl.BlockSpec(w_uq_perm.shape, lambda i: (0, 0)),
                  pl.BlockSpec((tm, LANES), lambda i: (i, 0)),
                  pl.BlockSpec((tm, LANES), lambda i: (i, 0))],
        out_specs=pl.BlockSpec((tm, MLA_HEADS * 256), lambda i: (i, 0)),
        compiler_params=_cparams("parallel"),
        name="mla_q_proj",
    )(h_in, gain.reshape(1, -1).astype(F32), w_uq_perm, cos_t, sin_t)


def _kvproj_kernel(c_ref, g_ref, w_ref, kr_ref, cos_ref, sin_ref, kv_ref, kro_ref):
    xn = _rms(c_ref[...].astype(F32), g_ref[...]).astype(BF16)
    kv_ref[...] = _dot(xn, w_ref[...]).astype(kv_ref.dtype)
    kro_ref[...] = _rope_pair(kr_ref[...].astype(F32), cos_ref[...], sin_ref[...]).astype(kro_ref.dtype)


def mla_kv_proj(h_in, gain, w_ukv, cos_t, sin_t, *, tm):
    m = h_in.shape[0]
    tm = min(tm, m)
    n = w_ukv.shape[1]
    return pl.pallas_call(
        _kvproj_kernel,
        out_shape=[jax.ShapeDtypeStruct((m, n), BF16), jax.ShapeDtypeStruct((m, LANES), BF16)],
        grid=(m // tm,),
        in_specs=[pl.BlockSpec((tm, MLA_KV_LORA), lambda i: (i, COL_CKV // MLA_KV_LORA)),
                  pl.BlockSpec((1, MLA_KV_LORA), lambda i: (0, 0)),
                  pl.BlockSpec(w_ukv.shape, lambda i: (0, 0)),
                  pl.BlockSpec((tm, LANES), lambda i: (i, COL_KR // LANES)),
                  pl.BlockSpec((tm, LANES), lambda i: (i, 0)),
                  pl.BlockSpec((tm, LANES), lambda i: (i, 0))],
        out_specs=[pl.BlockSpec((tm, n), lambda i: (i, 0)),
                   pl.BlockSpec((tm, LANES), lambda i: (i, 0))],
        compiler_params=_cparams("parallel"),
        name="mla_kv_proj",
    )(h_in, gain.reshape(1, -1).astype(F32), w_ukv, h_in, cos_t, sin_t)


ATTN_UNROLL = 16


def _mla_attn_kernel(qi_tab, kj_tab, q_ref, kv_ref, kr_ref, o_ref, s0_scr, s1_scr, m_scr, acc_scr,
                     *, blk, nq, n_items):
    ones = jnp.ones((blk, LANES), BF16)
    s_scr = (s0_scr, s1_scr)

    def scores(t, s_ref):
        q_rows = pl.ds(pl.multiple_of(qi_tab[t] * blk, blk), blk)
        k_rows = pl.ds(pl.multiple_of(kj_tab[t] * blk, blk), blk)
        k = jnp.concatenate([kv_ref[k_rows, :MLA_NOPE], kr_ref[k_rows, :]], axis=1)
        s_ref[...] = _dot_nt(q_ref[q_rows, :], k)

    def finish(t, s_ref, diagonal):
        qi = qi_tab[t]
        k_rows = pl.ds(pl.multiple_of(kj_tab[t] * blk, blk), blk)
        v = jnp.concatenate([kv_ref[k_rows, MLA_NOPE:], ones], axis=1)
        s = s_ref[...]
        if diagonal:
            row = lax.broadcasted_iota(I32, (blk, blk), 0)
            col = lax.broadcasted_iota(I32, (blk, blk), 1)
            s = jnp.where(col <= row, s, NEG)
            m_new = jnp.max(s, axis=1, keepdims=True) + jnp.zeros((blk, LANES), F32)
        else:
            m = m_scr[qi]
            m_new = jnp.maximum(m, jnp.max(s, axis=1, keepdims=True))
        p = jnp.exp2(s - jnp.concatenate([m_new] * (blk // LANES), axis=1)).astype(BF16)
        pv = _dot(p, v)
        if diagonal:
            acc_scr[qi] = pv
        else:
            alpha = jnp.exp2(m - m_new)
            acc_scr[qi] = jnp.concatenate([alpha, alpha], axis=1) * acc_scr[qi] + pv
        m_scr[qi] = m_new

    def run(t0, n, slot, diagonal):
        for u in range(n):
            scores(jnp.minimum(t0 + u + 1, n_items - 1), s_scr[(slot + u + 1) % 2])
            finish(t0 + u, s_scr[(slot + u) % 2], diagonal)

    def phase(t0, n, slot, diagonal):
        def body(i, carry):
            run(t0 + i * ATTN_UNROLL, ATTN_UNROLL, slot, diagonal)
            return carry

        lax.fori_loop(0, n // ATTN_UNROLL, body, 0)
        if n % ATTN_UNROLL:
            run(t0 + n - n % ATTN_UNROLL, n % ATTN_UNROLL, slot, diagonal)
        return (slot + n) % 2

    scores(0, s0_scr)
    slot = phase(0, nq, 0, True)
    phase(nq, n_items - nq, slot, False)

    def normalise(qi, carry):
        acc = acc_scr[qi]
        o_ref[pl.ds(pl.multiple_of(qi * blk, blk), blk), :] = (acc[:, :MLA_V] / acc[:, MLA_V:]).astype(o_ref.dtype)
        return carry

    lax.fori_loop(0, nq, normalise, 0)


def mla_attention(q_full, kv, k_rope, *, batch, seq, blk):
    blk = min(blk, seq)
    nq = seq // blk
    m = batch * seq
    pairs = [(i, i) for i in range(nq)] + [(i, i - d) for d in range(1, nq) for i in range(d, nq)]
    qi_tab = jnp.asarray([p[0] for p in pairs], I32)
    kj_tab = jnp.asarray([p[1] for p in pairs], I32)
    grid_spec = pltpu.PrefetchScalarGridSpec(
        num_scalar_prefetch=2,
        grid=(batch, MLA_HEADS),
        in_specs=[pl.BlockSpec((seq, 256), lambda b, h, *_: (b, h)),
                  pl.BlockSpec((seq, 256), lambda b, h, *_: (b, h)),
                  pl.BlockSpec((seq, LANES), lambda b, h, *_: (b, 0))],
        out_specs=pl.BlockSpec((seq, MLA_V), lambda b, h, *_: (b, h)),
        scratch_shapes=[pltpu.VMEM((blk, blk), F32), pltpu.VMEM((blk, blk), F32),
                        pltpu.VMEM((nq, blk, LANES), F32), pltpu.VMEM((nq, blk, 2 * MLA_V), F32)],
    )
    return pl.pallas_call(
        functools.partial(_mla_attn_kernel, blk=blk, nq=nq, n_items=len(pairs)),
        out_shape=jax.ShapeDtypeStruct((m, MLA_WIDTH), BF16),
        grid_spec=grid_spec,
        compiler_params=_cparams("parallel", "parallel"),
        name="mla_attention",
    )(qi_tab, kj_tab, q_full, kv, k_rope)


GLA_HEADS_PER_STEP = 1


def _gla_kernel(q_ref, k_ref, v_ref, og_ref, lr_ref, wg_ref, bg_ref, gn_ref, o_ref, st_ref, *, sb, c):
    @pl.when(pl.program_id(2) == 0)
    def _():
        st_ref[...] = jnp.zeros_like(st_ref)

    z = _dot(lr_ref[...], wg_ref[...]) + bg_ref[...]
    log_a = (jnp.minimum(z, 0.0) - jnp.log(1.0 + jnp.exp(-jnp.abs(z)))) * (1.0 / GLA_TAU)
    rows = lax.broadcasted_iota(I32, (c, c), 0)
    cols = lax.broadcasted_iota(I32, (c, c), 1)
    causal = cols <= rows
    tril = causal.astype(F32)
    gn = gn_ref[...]
    scale = GLA_HEAD_K ** -0.5

    for n in range(sb // c):
        sl = slice(n * c, (n + 1) * c)
        for hh in range(GLA_HEADS_PER_STEP):
            kcols = slice(hh * GLA_HEAD_K, (hh + 1) * GLA_HEAD_K)
            vcols = slice(hh * GLA_HEAD_V, (hh + 1) * GLA_HEAD_V)
            g = log_a[sl, kcols]
            b = jnp.dot(tril, g, preferred_element_type=F32, precision=lax.Precision.HIGHEST)
            b_last = b[c - 1:c, :]
            q = q_ref[sl, kcols].astype(F32)
            k = k_ref[sl, kcols].astype(F32)
            v = v_ref[sl, vcols]
            qe = (q * scale * jnp.exp(b)).astype(BF16)
            ke = (k * jnp.exp(-b)).astype(BF16)
            kd = (k * jnp.exp(b_last - b)).astype(BF16)
            att = jnp.where(causal, _dot_nt(qe, ke), 0.0).astype(BF16)
            st = st_ref[hh]
            o = _dot(att, v) + _dot_nt(qe, st.astype(BF16))
            st_ref[hh] = st * jnp.exp(b_last) + _dot_tn(v, kd)
            og = og_ref[sl, vcols].astype(F32)
            o = _rms(o, gn) * (og * (1.0 / (1.0 + jnp.exp(-og))))
            o_ref[sl, vcols] = o.astype(o_ref.dtype)


def gla_mixer(h_in, w_gate_pad, b_gate, gla_out_norm, *, batch, seq, sb):
    sb = min(sb, seq)
    nsb = seq // sb
    m = batch * seq
    g = GLA_HEADS_PER_STEP
    kw, vw = g * GLA_HEAD_K, g * GLA_HEAD_V
    return pl.pallas_call(
        functools.partial(_gla_kernel, sb=sb, c=GLA_CHUNK),
        out_shape=jax.ShapeDtypeStruct((m, GLA_V_WIDTH), BF16),
        grid=(batch, GLA_HEADS // g, nsb),
        in_specs=[pl.BlockSpec((sb, kw), lambda b, h, s: (b * nsb + s, COL_GQ // kw + h)),
                  pl.BlockSpec((sb, kw), lambda b, h, s: (b * nsb + s, COL_GK // kw + h)),
                  pl.BlockSpec((sb, vw), lambda b, h, s: (b * nsb + s, COL_GV // vw + h)),
                  pl.BlockSpec((sb, vw), lambda b, h, s: (b * nsb + s, COL_OG // vw + h)),
                  pl.BlockSpec((sb, LANES), lambda b, h, s: (b * nsb + s, COL_LR // LANES)),
                  pl.BlockSpec((LANES, kw), lambda b, h, s: (0, h)),
                  pl.BlockSpec((1, kw), lambda b, h, s: (0, h)),
                  pl.BlockSpec((1, GLA_HEAD_V), lambda b, h, s: (0, 0))],
        out_specs=pl.BlockSpec((sb, vw), lambda b, h, s: (b * nsb + s, h)),
        scratch_shapes=[pltpu.VMEM((g, GLA_HEAD_V, GLA_HEAD_K), F32)],
        compiler_params=_cparams("parallel", "parallel", "arbitrary"),
        name="gla_mixer",
    )(h_in, h_in, h_in, h_in, h_in, w_gate_pad, b_gate.reshape(1, -1).astype(F32),
      gla_out_norm.reshape(1, -1).astype(F32))


def _out_proj_kernel(om_ref, g_ref, og_ref, w_ref, x_ref, o_ref, mix_ref):
    @pl.when(pl.program_id(1) == 0)
    def _():
        mix_ref[:, :MLA_WIDTH] = _rms(om_ref[...].astype(F32), g_ref[...]).astype(BF16)
        mix_ref[:, MLA_WIDTH:] = og_ref[...]

    o_ref[...] = x_ref[...] + _dot(mix_ref[...], w_ref[...])


def mixer_out_proj(o_mla, mla_out_norm, o_gla, w_out, x, *, tm, tn):
    m, d = x.shape
    tm = min(tm, m)
    tn = min(tn, d)
    kmix = MLA_WIDTH + GLA_V_WIDTH
    return pl.pallas_call(
        _out_proj_kernel,
        out_shape=jax.ShapeDtypeStruct((m, d), F32),
        grid=(m // tm, d // tn),
        in_specs=[pl.BlockSpec((tm, MLA_WIDTH), lambda i, j: (i, 0)),
                  pl.BlockSpec((1, MLA_WIDTH), lambda i, j: (0, 0)),
                  pl.BlockSpec((tm, GLA_V_WIDTH), lambda i, j: (i, 0)),
                  pl.BlockSpec((kmix, tn), lambda i, j: (0, j)),
                  pl.BlockSpec((tm, tn), lambda i, j: (i, j))],
        out_specs=pl.BlockSpec((tm, tn), lambda i, j: (i, j)),
        scratch_shapes=[pltpu.VMEM((tm, kmix), BF16)],
        compiler_params=_cparams("parallel", "arbitrary"),
        name="mixer_out_proj",
    )(o_mla, mla_out_norm.reshape(1, -1).astype(F32), o_gla, w_out, x)


def _xattn_kernel(q_ref, kv_ref, w_ref, h_ref, o_ref):
    scale = XATTN_DIM ** -0.5
    outs = []
    for h in range(XATTN_HEADS):
        q = q_ref[:, h * XATTN_DIM:(h + 1) * XATTN_DIM]
        k = kv_ref[:, h * XATTN_DIM:(h + 1) * XATTN_DIM]
        v = kv_ref[:, XATTN_WIDTH + h * XATTN_DIM: XATTN_WIDTH + (h + 1) * XATTN_DIM]
        s = _dot_nt(q, k) * scale
        p = jnp.exp(s - jnp.max(s, axis=1, keepdims=True))
        p = p / jnp.sum(p, axis=1, keepdims=True)
        outs.append(_dot(p.astype(BF16), v).astype(BF16))
    o = jnp.concatenate(outs, axis=1)
    o_ref[...] = h_ref[...] + _dot(o, w_ref[...])


def cross_attention(qx, kvm, w_co, h1, *, batch, seq, mem_tokens, tm):
    m, d = h1.shape
    tm = min(tm, seq)
    per_b = seq // tm
    return pl.pallas_call(
        _xattn_kernel,
        out_shape=jax.ShapeDtypeStruct((m, d), F32),
        grid=(m // tm,),
        in_specs=[pl.BlockSpec((tm, XATTN_WIDTH), lambda i: (i, 0)),
                  pl.BlockSpec((mem_tokens, 2 * XATTN_WIDTH), lambda i: (i // per_b, 0)),
                  pl.BlockSpec(w_co.shape, lambda i: (0, 0), pipeline_mode=pl.Buffered(1)),
                  pl.BlockSpec((tm, d), lambda i: (i, 0))],
        out_specs=pl.BlockSpec((tm, d), lambda i: (i, 0)),
        compiler_params=_cparams("parallel"),
        name="cross_attention",
    )(qx, kvm, w_co, h1)


def _order_key(x):
    b = lax.bitcast_convert_type(x, I32)
    return b ^ (lax.shift_right_arithmetic(b, 31) & 0x7FFFFFFF)


def _order_key_inv(k):
    return lax.bitcast_convert_type(k ^ (lax.shift_right_arithmetic(k, 31) & 0x7FFFFFFF), F32)


KEY_BIAS = 0x20000000


def _pack_keys(x, bits):
    low = (1 << bits) - 1
    row = lax.broadcasted_iota(I32, x.shape, 0)
    k = (_order_key(x) & ~((low << 2) | 3)) | ((low - row) << 2)
    return lax.bitcast_convert_type(lax.shift_right_arithmetic(k, 2) + KEY_BIAS, F32)


def _unpack_keys(kf, bits):
    low = (1 << bits) - 1
    k = lax.shift_left(lax.bitcast_convert_type(kf, I32) - KEY_BIAS, 2)
    return _order_key_inv(k & ~((low << 2) | 3)), low - (lax.shift_right_arithmetic(k, 2) & low)


SUBLANES = 8


def _max_rows(k3):
    m8 = jnp.max(k3, axis=0)
    for shift in (4, 2, 1):
        m8 = jnp.maximum(m8, pltpu.roll(m8, shift, axis=0))
    return m8


def _top16_rows(s, bits):
    rows, t = s.shape
    k3 = _pack_keys(s, bits).reshape(rows // SUBLANES, SUBLANES, t)
    vals, idxs = [], []
    for _ in range(PEER_TOPK):
        m8 = _max_rows(k3)
        val, idx = _unpack_keys(m8[0:1, :], bits)
        vals.append(val)
        idxs.append(idx)
        k3 = jnp.where(k3 == m8[None], 0.0, k3)
    return vals, idxs


_PAIRS = [(a, b) for a in range(PEER_TOPK) for b in range(PEER_TOPK) if (a + 1) * (b + 1) <= PEER_TOPK]
_PAIR_ROWS = -(-len(_PAIRS) // 8) * 8


TOPK_HEADS_PER_TRIP = 4


def _peer_topk_kernel(q_ref, keys_ref, eid_ref, gate_ref, eid_t, gate_t, *, tm):
    def head(h):
        tops = []
        for p in range(2):
            col = pl.multiple_of((2 * h + p) * PEER_HALF, PEER_HALF)
            qh = q_ref[:, pl.ds(col, PEER_HALF)]
            s = _dot_nt(keys_ref[2 * h + p], qh)
            tops.append(_top16_rows(s, 7))
        (s1, i1), (s2, i2) = tops
        cand_s = [s1[a] + s2[b] for a, b in _PAIRS]
        cand_e = [i1[a] * PEER_KEYS + i2[b] for a, b in _PAIRS]
        pad = _PAIR_ROWS - len(_PAIRS)
        cs = jnp.concatenate(cand_s + [jnp.full((pad, tm), NEG, F32)], axis=0)
        ce = jnp.concatenate(cand_e + [jnp.zeros((pad, tm), I32)], axis=0)
        k3 = _pack_keys(cs, 6).reshape(_PAIR_ROWS // SUBLANES, SUBLANES, tm)
        ce3 = ce.astype(F32).reshape(_PAIR_ROWS // SUBLANES, SUBLANES, tm)
        sel_s, sel_e = [], []
        for _ in range(PEER_TOPK):
            m8 = _max_rows(k3)
            hit = k3 == m8[None]
            sel_s.append(_unpack_keys(m8[0:1, :], 6)[0])
            sel_e.append(_max_rows(jnp.where(hit, ce3, -1.0))[0:1, :].astype(I32))
            k3 = jnp.where(hit, 0.0, k3)
        top_s = jnp.concatenate(sel_s, axis=0)
        top_e = jnp.concatenate(sel_e, axis=0)
        w = jnp.exp(top_s - top_s[0:1, :])
        gate = w / jnp.sum(w, axis=0, keepdims=True)
        row = pl.multiple_of(h * PEER_TOPK, PEER_TOPK)
        eid_t[pl.ds(row, PEER_TOPK), :] = top_e
        gate_t[pl.ds(row, PEER_TOPK), :] = gate

    def trip(i, carry):
        for u in range(TOPK_HEADS_PER_TRIP):
            head(i * TOPK_HEADS_PER_TRIP + u)
        return carry

    lax.fori_loop(0, PEER_HEADS // TOPK_HEADS_PER_TRIP, trip, 0)
    eid_ref[...] = eid_t[...].T
    gate_ref[...] = gate_t[...].T


def peer_topk(qp, keys, *, tm):
    m = qp.shape[0]
    tm = min(tm, m)
    return pl.pallas_call(
        functools.partial(_peer_topk_kernel, tm=tm),
        out_shape=[jax.ShapeDtypeStruct((m, PEER_SLOTS), I32), jax.ShapeDtypeStruct((m, PEER_SLOTS), F32),
                   jax.ShapeDtypeStruct((PEER_SLOTS, m), I32)],
        grid=(m // tm,),
        in_specs=[pl.BlockSpec((tm, qp.shape[1]), lambda i: (i, 0)),
                  pl.BlockSpec(keys.shape, lambda i: (0, 0, 0))],
        out_specs=[pl.BlockSpec((tm, PEER_SLOTS), lambda i: (i, 0)),
                   pl.BlockSpec((tm, PEER_SLOTS), lambda i: (i, 0)),
                   pl.BlockSpec((PEER_SLOTS, tm), lambda i: (0, i))],
        scratch_shapes=[pltpu.VMEM((PEER_SLOTS, tm), F32)],
        compiler_params=_cparams("parallel"),
        name="peer_topk",
    )(qp, keys)


def _peer_gates_kernel(eid_ref, eidt_ref, gate_ref, o_ref, g3_ref, *, tm, pitch):
    sub = lax.broadcasted_iota(I32, (PEER_KEYS, PEER_SLOTS), 0)
    lane = lax.broadcasted_iota(I32, (PEER_SLOTS, PEER_KEYS), 1).astype(F32).astype(BF16)
    lo_all = jnp.bitwise_and(eidt_ref[...], PEER_KEYS - 1).astype(F32).astype(BF16)
    one = jnp.ones((), BF16)
    zero = jnp.zeros((), BF16)
    for t in range(tm):
        e = eid_ref[t:t + 1, :]
        hi = lax.shift_right_logical(e, 7)
        a1 = jnp.where(sub == hi, gate_ref[t:t + 1, :], 0.0).astype(BF16)
        a2 = jnp.where(lane == lo_all[:, t:t + 1], one, zero)
        g3_ref[pl.ds(t, PEER_KEYS, stride=pitch), :] = _dot(a1, a2)
    for e1 in range(PEER_KEYS):
        o_ref[:, e1 * PEER_KEYS:(e1 + 1) * PEER_KEYS] = g3_ref[e1 * pitch:e1 * pitch + tm, :].astype(o_ref.dtype)


def peer_gates(eid, eid_t, gate, *, tm):
    m = eid.shape[0]
    tm = min(tm, m)
    pitch = tm + 8
    return pl.pallas_call(
        functools.partial(_peer_gates_kernel, tm=tm, pitch=pitch),
        out_shape=jax.ShapeDtypeStruct((m, PEER_EXPERTS), BF16),
        grid=(m // tm,),
        in_specs=[pl.BlockSpec((tm, PEER_SLOTS), lambda i: (i, 0)),
                  pl.BlockSpec((PEER_SLOTS, tm), lambda i: (0, i)),
                  pl.BlockSpec((tm, PEER_SLOTS), lambda i: (i, 0))],
        out_specs=pl.BlockSpec((tm, PEER_EXPERTS), lambda i: (i, 0)),
        scratch_shapes=[pltpu.VMEM((PEER_KEYS * pitch, PEER_KEYS), F32)],
        compiler_params=_cparams("parallel"),
        name="peer_gates",
    )(eid, eid_t, gate)


def _peer_dense_kernel(x_ref, sx_ref, u_ref, su_ref, g_ref, v_ref, sv_ref, h_ref, gn_ref, o_ref, aq_scr, sa_scr,
                       *, final_norm, n_blocks):
    j = pl.program_id(1)
    nj = n_blocks

    def step(p, do_values, do_gates):
        if do_gates:
            s = _dot_nt(x_ref[...], u_ref[...])
        if do_values:
            pv = _dot(aq_scr[1 - p], v_ref[...])
        if do_gates:
            s = s * sx_ref[...] * su_ref[...]
            a = 0.5 * s * (1.0 + lax.erf(s * (2.0 ** -0.5)))
            aq_scr[p], sa_scr[p] = _quantize_rows(a * g_ref[...].astype(F32) * sv_ref[...])
        if do_values:
            o_ref[...] += sa_scr[1 - p] * pv

    half = o_ref.shape[1] // 2

    @pl.when(j == 0)
    def _():
        o_ref[:, :half] = h_ref[...]
        o_ref[:, half:] = jnp.zeros((o_ref.shape[0], half), F32)
        step(0, False, True)

    @pl.when(j == 1)
    def _():
        o_ref[:, half:] += h_ref[...]

    for p in range(2):
        @pl.when(jnp.logical_and(jnp.logical_and(j > 0, j < nj), j % 2 == p))
        def _(p=p):
            step(p, True, True)

    @pl.when(j == nj)
    def _():
        step(nj % 2, True, False)
        if final_norm:
            o_ref[...] = _rms(o_ref[...], gn_ref[...])


def peer_dense(xq, sx, uq, su, g, vq, sv, h, gain, *, final_norm, tm, te):
    m, d = xq.shape
    e = uq.shape[0]
    tm = min(tm, m)
    nj = e // te
    def cur(j):
        return jnp.minimum(j, nj - 1)

    def prev(j):
        return jnp.maximum(j - 1, 0)

    return pl.pallas_call(
        functools.partial(_peer_dense_kernel, final_norm=final_norm, n_blocks=nj),
        out_shape=jax.ShapeDtypeStruct((m, d), F32),
        grid=(m // tm, nj + 1),
        in_specs=[pl.BlockSpec((tm, d), lambda i, j: (i, 0)),
                  pl.BlockSpec((tm, 1), lambda i, j: (i, 0)),
                  pl.BlockSpec((te, d), lambda i, j: (cur(j), 0)),
                  pl.BlockSpec((1, te), lambda i, j: (0, cur(j))),
                  pl.BlockSpec((tm, te), lambda i, j: (i, cur(j))),
                  pl.BlockSpec((te, d), lambda i, j: (prev(j), 0)),
                  pl.BlockSpec((1, te), lambda i, j: (0, cur(j))),
                  pl.BlockSpec((tm, d // 2), lambda i, j: (i, jnp.minimum(j, 1))),
                  pl.BlockSpec((1, d), lambda i, j: (0, 0))],
        out_specs=pl.BlockSpec((tm, d), lambda i, j: (i, 0)),
        scratch_shapes=[pltpu.VMEM((2, tm, te), F8), pltpu.VMEM((2, tm, 1), F32)],
        compiler_params=_cparams("parallel", "arbitrary"),
        name="peer_dense",
    )(xq, sx, uq, su.reshape(1, e), g, vq, sv.reshape(1, e), h, gain.reshape(1, d).astype(F32))


def _w_in_columns():
    src = {}
    off = 0
    for name, width in (("cq", MLA_Q_LORA), ("ckv", MLA_KV_LORA), ("kr", MLA_ROPE), ("gq", GLA_K_WIDTH),
                        ("gk", GLA_K_WIDTH), ("gv", GLA_V_WIDTH), ("lr", GLA_RANK), ("og", GLA_V_WIDTH)):
        src[name] = np.arange(off, off + width)
        off += width
    cols = np.full((IN_PAD,), -1, np.int64)
    for name, start in (("cq", COL_CQ), ("ckv", COL_CKV), ("gq", COL_GQ), ("gk", COL_GK),
                        ("gv", COL_GV), ("og", COL_OG), ("lr", COL_LR)):
        cols[start:start + len(src[name])] = src[name]
    half = MLA_ROPE // 2
    x1, x2 = src["kr"][:half], src["kr"][half:]
    cols[COL_KR:COL_KR + LANES] = np.concatenate([x1, x1, x2, x2])
    return cols


def _w_uq_columns():
    half = MLA_ROPE // 2
    nope = [h * MLA_QK + d for h in range(MLA_HEADS) for d in range(MLA_NOPE)]
    rope = []
    for j in range(MLA_HEADS // 2):
        a, b = 2 * j, 2 * j + 1
        for part in (0, 1):
            for h in (a, b):
                rope += [h * MLA_QK + MLA_NOPE + part * half + r for r in range(half)]
    return np.array(nope + rope, np.int64)


def _column_runs(cols):
    runs = []
    start = 0
    for i in range(1, len(cols) + 1):
        if i == len(cols) or cols[i] != (cols[i - 1] + 1 if cols[i - 1] >= 0 else -1):
            runs.append((int(cols[start]), i - start))
            start = i
    return runs


def _take_cols(w, cols, dtype):
    pieces = [jnp.zeros((w.shape[0], n), dtype) if src < 0 else w[:, src:src + n].astype(dtype)
              for src, n in _column_runs(cols)]
    return jnp.concatenate(pieces, axis=1)


def _take_rows(wt, rows, dtype):
    pieces = [jnp.zeros((n, wt.shape[1]), dtype) if src < 0 else wt[src:src + n].astype(dtype)
              for src, n in _column_runs(rows)]
    return jnp.concatenate(pieces, axis=0)


def _rope_tables(positions):
    half = MLA_ROPE // 2
    inv_freq = ROPE_THETA ** (-jnp.arange(0, MLA_ROPE, 2, dtype=F32) / MLA_ROPE)
    ang = positions.astype(F32).reshape(-1, 1) * inv_freq
    cos = jnp.cos(ang)
    sin = jnp.sin(ang)
    return jnp.tile(cos, (1, 4)), jnp.concatenate([-sin, -sin, sin, sin], axis=1)


def kernel(x, mem, positions, norm_mem, norm_mix, w_in, mla_q_norm, w_uq, mla_kv_norm, w_ukv, mla_out_norm,
           w_gate_up, b_gate, gla_out_norm, w_out, norm_cross, w_cq, w_ck, w_cv, w_co, norm_ffn, w_peer_q,
           peer_sub_keys, peer_u, peer_v, norm_final):
    batch, seq, d = x.shape
    mem_tokens = mem.shape[1]
    m = batch * seq
    h = x.reshape(m, d)
    cos_t, sin_t = _rope_tables(positions)

    t = TILES
    depth = norm_mix.shape[0]
    for l in range(depth):
        w_in_t = _take_rows(jnp.transpose(w_in[l]), _w_in_columns(), BF16)
        h_in = norm_matmul(h, norm_mix[l], w_in_t, k=d, tm=t.rows, tn=t.cols, out_dtype=BF16, w_transposed=True,
                           name="in_proj")

        w_uq_p = _take_cols(w_uq[l], _w_uq_columns(), BF16)
        q_full = mla_q_proj(h_in, mla_q_norm[l], w_uq_p, cos_t, sin_t, tm=t.rows)
        kv, k_rope = mla_kv_proj(h_in, mla_kv_norm[l], w_ukv[l].astype(BF16), cos_t, sin_t, tm=t.rows)
        o_mla = mla_attention(q_full, kv, k_rope, batch=batch, seq=seq, blk=t.attn_block)

        w_gate_pad = jnp.zeros((LANES, GLA_K_WIDTH), F32).at[:GLA_RANK].set(w_gate_up[l]).astype(BF16)
        o_gla = gla_mixer(h_in, w_gate_pad, b_gate[l], gla_out_norm[l], batch=batch, seq=seq, sb=t.gla_rows)

        h = mixer_out_proj(o_mla, mla_out_norm[l], o_gla, w_out[l].astype(BF16), h, tm=t.rows, tn=t.cols)

        mn_kv = norm_matmul(mem.reshape(batch * mem_tokens, d), norm_mem,
                            jnp.concatenate([w_ck[l], w_cv[l]], axis=1).astype(BF16),
                            k=d, tm=t.rows, tn=t.cols, out_dtype=BF16, name="mem_kv_proj")
        qx = norm_matmul(h, norm_cross[l], w_cq[l].astype(BF16), k=d, tm=t.rows, tn=XATTN_WIDTH, out_dtype=BF16,
                         name="xattn_q_proj")
        h = cross_attention(qx, mn_kv, w_co[l].astype(BF16), h, batch=batch, seq=seq,
                            mem_tokens=mem_tokens, tm=t.rows)

        qp, xq, sx = norm_matmul(h, norm_ffn[l], w_peer_q[l].astype(BF16), k=d, tm=t.peer_q_rows,
                                 tn=PEER_HEADS * 2 * PEER_HALF, out_dtype=BF16, emit_xq=True, name="peer_q_proj")
        keys = peer_sub_keys[l].reshape(PEER_HEADS * 2, PEER_KEYS, PEER_HALF).astype(BF16)
        eid, gate, eid_t = peer_topk(qp, keys, tm=t.topk_tokens)
        g = peer_gates(eid, eid_t, gate, tm=t.gate_tokens)
        uq, su = quantize_rows(peer_u[l], tr=t.rows, name="peer_u_quant")
        vq, sv = quantize_rows(peer_v[l], tr=t.rows, name="peer_v_quant")
        h = peer_dense(xq, sx, uq, su, g, vq, sv, h, norm_final, final_norm=(l + 1 == depth),
                       tm=t.rows, te=t.peer_experts)
    return h.reshape(batch, seq, d)
```

```python
import functools
import math
from typing import NamedTuple

import numpy as np
import jax
import jax.numpy as jnp
from jax import lax
from jax.experimental import pallas as pl
from jax.experimental.pallas import tpu as pltpu

F32 = jnp.float32
BF16 = jnp.bfloat16
F8 = jnp.float8_e4m3fn
I32 = jnp.int32
FP8_AMAX = 256.0

EPS = 1e-6
ROPE_THETA = 10000.0

MLA_HEADS = 16
MLA_Q_LORA = 1024
MLA_KV_LORA = 512
MLA_NOPE = 128
MLA_ROPE = 64
MLA_QK = MLA_NOPE + MLA_ROPE
MLA_V = 128
MLA_WIDTH = MLA_HEADS * MLA_V

GLA_HEADS = 4
GLA_HEAD_K = 256
GLA_HEAD_V = 512
GLA_K_WIDTH = GLA_HEADS * GLA_HEAD_K
GLA_V_WIDTH = GLA_HEADS * GLA_HEAD_V
GLA_RANK = 16
GLA_TAU = 16.0
GLA_CHUNK = 64

XATTN_HEADS = 4
XATTN_DIM = 256
XATTN_WIDTH = XATTN_HEADS * XATTN_DIM

PEER_HEADS = 8
PEER_KEYS = 128
PEER_HALF = 128
PEER_TOPK = 16
PEER_SLOTS = PEER_HEADS * PEER_TOPK
PEER_EXPERTS = PEER_KEYS * PEER_KEYS

LANES = 128
VMEM_LIMIT = 56 * 1024 * 1024


class _Tiles(NamedTuple):
    rows: int = 512
    cols: int = 1024
    attn_block: int = 512
    gla_rows: int = 1024
    peer_q_rows: int = 256
    topk_tokens: int = 256
    gate_tokens: int = 128
    peer_experts: int = 1024


TILES = _Tiles()

COL_CQ = 0
COL_CKV = COL_CQ + MLA_Q_LORA
COL_KR = COL_CKV + MLA_KV_LORA
COL_LR = COL_KR + LANES
COL_GQ = 2048
COL_GK = COL_GQ + GLA_K_WIDTH
COL_GV = COL_GK + GLA_K_WIDTH
COL_OG = COL_GV + GLA_V_WIDTH
IN_PAD = COL_OG + GLA_V_WIDTH

NEG = -1e30


def _cparams(*sem):
    return pltpu.CompilerParams(dimension_semantics=sem, vmem_limit_bytes=VMEM_LIMIT)


def _rms(x, g):
    ms = jnp.mean(x * x, axis=-1, keepdims=True)
    return x * lax.rsqrt(ms + EPS) * g


def _dot(a, b):
    return jnp.dot(a, b, preferred_element_type=F32)


def _dot_nt(a, b):
    return lax.dot_general(a, b, (((1,), (1,)), ((), ())), preferred_element_type=F32)


def _dot_tn(a, b):
    return lax.dot_general(a, b, (((0,), (0,)), ((), ())), preferred_element_type=F32)


def _quantize_rows(x):
    amax = jnp.max(jnp.abs(x), axis=-1, keepdims=True)
    scale = jnp.where(amax > 0.0, amax * (1.0 / FP8_AMAX), 1.0)
    return (x * (1.0 / scale)).astype(F8), scale


def _norm_matmul_kernel(x_ref, g_ref, w_ref, o_ref, *rest, emit_xq, whole_n, w_transposed):
    if emit_xq:
        xq_ref, sx_ref, xn_ref = rest
    else:
        (xn_ref,) = rest

    def normalise():
        xn = _rms(x_ref[...].astype(F32), g_ref[...])
        xn_ref[...] = xn.astype(BF16)
        if emit_xq:
            xq_ref[...], sx_ref[...] = _quantize_rows(xn)

    if whole_n:
        normalise()
    else:
        pl.when(pl.program_id(1) == 0)(normalise)
    mm = _dot_nt if w_transposed else _dot
    o_ref[...] = mm(xn_ref[...], w_ref[...]).astype(o_ref.dtype)


def norm_matmul(x, gain, w, *, k, x_col_blk=0, tm, tn, out_dtype, emit_xq=False, w_transposed=False, name):
    m = x.shape[0]
    n = w.shape[0] if w_transposed else w.shape[1]
    tm = min(tm, m)
    tn = min(tn, n)
    out_shape = [jax.ShapeDtypeStruct((m, n), out_dtype)]
    out_specs = [pl.BlockSpec((tm, tn), lambda i, j: (i, j))]
    if emit_xq:
        out_shape += [jax.ShapeDtypeStruct((m, k), F8), jax.ShapeDtypeStruct((m, 1), F32)]
        out_specs += [pl.BlockSpec((tm, k), lambda i, j: (i, 0)), pl.BlockSpec((tm, 1), lambda i, j: (i, 0))]
    whole_n = tn == n
    w_mode = pl.Buffered(1) if whole_n else None
    w_spec = (pl.BlockSpec((tn, k), lambda i, j: (j, 0), pipeline_mode=w_mode) if w_transposed
              else pl.BlockSpec((k, tn), lambda i, j: (0, j), pipeline_mode=w_mode))
    res = pl.pallas_call(
        functools.partial(_norm_matmul_kernel, emit_xq=emit_xq, whole_n=whole_n, w_transposed=w_transposed),
        out_shape=out_shape,
        grid=(m // tm, n // tn),
        in_specs=[pl.BlockSpec((tm, k), lambda i, j: (i, x_col_blk)),
                  pl.BlockSpec((1, k), lambda i, j: (0, 0)),
                  w_spec],
        out_specs=out_specs,
        scratch_shapes=[pltpu.VMEM((tm, k), BF16)],
        compiler_params=_cparams("parallel", "arbitrary"),
        name=name,
    )(x, gain.reshape(1, k).astype(F32), w)
    return res if emit_xq else res[0]


def _quantize_kernel(w_ref, q_ref, s_ref):
    q_ref[...], s_ref[...] = _quantize_rows(w_ref[...])


def quantize_rows(w, *, tr, name):
    r, d = w.shape
    return pl.pallas_call(
        _quantize_kernel,
        out_shape=[jax.ShapeDtypeStruct((r, d), F8), jax.ShapeDtypeStruct((r, 1), F32)],
        grid=(r // tr,),
        in_specs=[pl.BlockSpec((tr, d), lambda i: (i, 0))],
        out_specs=[pl.BlockSpec((tr, d), lambda i: (i, 0)), pl.BlockSpec((tr, 1), lambda i: (i, 0))],
        compiler_params=_cparams("parallel"),
        name=name,
    )(w)


def _rope_pair(r, cosv, sinv):
    return r * cosv + pltpu.roll(r, 2 * 32, axis=1) * sinv


def _qproj_kernel(c_ref, g_ref, w_ref, cos_ref, sin_ref, o_ref, *, scale):
    xn = _rms(c_ref[...].astype(F32), g_ref[...]).astype(BF16)
    q = _dot(xn, w_ref[...])
    cosv = cos_ref[...]
    sinv = sin_ref[...]
    lane = lax.broadcasted_iota(I32, (1, LANES), 1)
    rope_base = MLA_HEADS * MLA_NOPE
    for j in range(MLA_HEADS // 2):
        r = q[:, rope_base + j * LANES: rope_base + (j + 1) * LANES]
        r = _rope_pair(r, cosv, sinv) * scale
        for p in range(2):
            h = 2 * j + p
            own = ((lane // 32) % 2) == p
            o_ref[:, h * 256: h * 256 + 128] = (q[:, h * 128:(h + 1) * 128] * scale).astype(o_ref.dtype)
            o_ref[:, h * 256 + 128: h * 256 + 256] = jnp.where(own, r, 0.0).astype(o_ref.dtype)


def mla_q_proj(h_in, gain, w_uq_perm, cos_t, sin_t, *, tm):
    m = h_in.shape[0]
    tm = min(tm, m)
    return pl.pallas_call(
        functools.partial(_qproj_kernel, scale=MLA_QK ** -0.5 * math.log2(math.e)),
        out_shape=jax.ShapeDtypeStruct((m, MLA_HEADS * 256), BF16),
        grid=(m // tm,),
        in_specs=[pl.BlockSpec((tm, MLA_Q_LORA), lambda i: (i, COL_CQ // MLA_Q_LORA)),
                  pl.BlockSpec((1, MLA_Q_LORA), lambda i: (0, 0)),
                  pl.BlockSpec(w_uq_perm.shape, lambda i: (0, 0)),
                  pl.BlockSpec((tm, LANES), lambda i: (i, 0)),
                  pl.BlockSpec((tm, LANES), lambda i: (i, 0))],
        out_specs=pl.BlockSpec((tm, MLA_HEADS * 256), lambda i: (i, 0)),
        compiler_params=_cparams("parallel"),
        name="mla_q_proj",
    )(h_in, gain.reshape(1, -1).astype(F32), w_uq_perm, cos_t, sin_t)


def _kvproj_kernel(c_ref, g_ref, w_ref, kr_ref, cos_ref, sin_ref, kv_ref, kro_ref):
    xn = _rms(c_ref[...].astype(F32), g_ref[...]).astype(BF16)
    kv_ref[...] = _dot(xn, w_ref[...]).astype(kv_ref.dtype)
    kro_ref[...] = _rope_pair(kr_ref[...].astype(F32), cos_ref[...], sin_ref[...]).astype(kro_ref.dtype)


def mla_kv_proj(h_in, gain, w_ukv, cos_t, sin_t, *, tm):
    m = h_in.shape[0]
    tm = min(tm, m)
    n = w_ukv.shape[1]
    return pl.pallas_call(
        _kvproj_kernel,
        out_shape=[jax.ShapeDtypeStruct((m, n), BF16), jax.ShapeDtypeStruct((m, LANES), BF16)],
        grid=(m // tm,),
        in_specs=[pl.BlockSpec((tm, MLA_KV_LORA), lambda i: (i, COL_CKV // MLA_KV_LORA)),
                  pl.BlockSpec((1, MLA_KV_LORA), lambda i: (0, 0)),
                  pl.BlockSpec(w_ukv.shape, lambda i: (0, 0)),
                  pl.BlockSpec((tm, LANES), lambda i: (i, COL_KR // LANES)),
                  pl.BlockSpec((tm, LANES), lambda i: (i, 0)),
                  pl.BlockSpec((tm, LANES), lambda i: (i, 0))],
        out_specs=[pl.BlockSpec((tm, n), lambda i: (i, 0)),
                   pl.BlockSpec((tm, LANES), lambda i: (i, 0))],
        compiler_params=_cparams("parallel"),
        name="mla_kv_proj",
    )(h_in, gain.reshape(1, -1).astype(F32), w_ukv, h_in, cos_t, sin_t)


ATTN_UNROLL = 16


def _mla_attn_kernel(qi_tab, kj_tab, q_ref, kv_ref, kr_ref, o_ref, s0_scr, s1_scr, m_scr, acc_scr,
                     *, blk, nq, n_items):
    ones = jnp.ones((blk, LANES), BF16)
    s_scr = (s0_scr, s1_scr)

    def scores(t, s_ref):
        q_rows = pl.ds(pl.multiple_of(qi_tab[t] * blk, blk), blk)
        k_rows = pl.ds(pl.multiple_of(kj_tab[t] * blk, blk), blk)
        k = jnp.concatenate([kv_ref[k_rows, :MLA_NOPE], kr_ref[k_rows, :]], axis=1)
        s_ref[...] = _dot_nt(q_ref[q_rows, :], k)

    def finish(t, s_ref, diagonal):
        qi = qi_tab[t]
        k_rows = pl.ds(pl.multiple_of(kj_tab[t] * blk, blk), blk)
        v = jnp.concatenate([kv_ref[k_rows, MLA_NOPE:], ones], axis=1)
        s = s_ref[...]
        if diagonal:
            row = lax.broadcasted_iota(I32, (blk, blk), 0)
            col = lax.broadcasted_iota(I32, (blk, blk), 1)
            s = jnp.where(col <= row, s, NEG)
            m_new = jnp.max(s, axis=1, keepdims=True) + jnp.zeros((blk, LANES), F32)
        else:
            m = m_scr[qi]
            m_new = jnp.maximum(m, jnp.max(s, axis=1, keepdims=True))
        p = jnp.exp2(s - jnp.concatenate([m_new] * (blk // LANES), axis=1)).astype(BF16)
        pv = _dot(p, v)
        if diagonal:
            acc_scr[qi] = pv
        else:
            alpha = jnp.exp2(m - m_new)
            acc_scr[qi] = jnp.concatenate([alpha, alpha], axis=1) * acc_scr[qi] + pv
        m_scr[qi] = m_new

    def run(t0, n, slot, diagonal):
        for u in range(n):
            scores(jnp.minimum(t0 + u + 1, n_items - 1), s_scr[(slot + u + 1) % 2])
            finish(t0 + u, s_scr[(slot + u) % 2], diagonal)

    def phase(t0, n, slot, diagonal):
        def body(i, carry):
            run(t0 + i * ATTN_UNROLL, ATTN_UNROLL, slot, diagonal)
            return carry

        lax.fori_loop(0, n // ATTN_UNROLL, body, 0)
        if n % ATTN_UNROLL:
            run(t0 + n - n % ATTN_UNROLL, n % ATTN_UNROLL, slot, diagonal)
        return (slot + n) % 2

    scores(0, s0_scr)
    slot = phase(0, nq, 0, True)
    phase(nq, n_items - nq, slot, False)

    def normalise(qi, carry):
        acc = acc_scr[qi]
        o_ref[pl.ds(pl.multiple_of(qi * blk, blk), blk), :] = (acc[:, :MLA_V] / acc[:, MLA_V:]).astype(o_ref.dtype)
        return carry

    lax.fori_loop(0, nq, normalise, 0)


def mla_attention(q_full, kv, k_rope, *, batch, seq, blk):
    blk = min(blk, seq)
    nq = seq // blk
    m = batch * seq
    pairs = [(i, i) for i in range(nq)] + [(i, i - d) for d in range(1, nq) for i in range(d, nq)]
    qi_tab = jnp.asarray([p[0] for p in pairs], I32)
    kj_tab = jnp.asarray([p[1] for p in pairs], I32)
    grid_spec = pltpu.PrefetchScalarGridSpec(
        num_scalar_prefetch=2,
        grid=(batch, MLA_HEADS),
        in_specs=[pl.BlockSpec((seq, 256), lambda b, h, *_: (b, h)),
                  pl.BlockSpec((seq, 256), lambda b, h, *_: (b, h)),
                  pl.BlockSpec((seq, LANES), lambda b, h, *_: (b, 0))],
        out_specs=pl.BlockSpec((seq, MLA_V), lambda b, h, *_: (b, h)),
        scratch_shapes=[pltpu.VMEM((blk, blk), F32), pltpu.VMEM((blk, blk), F32),
                        pltpu.VMEM((nq, blk, LANES), F32), pltpu.VMEM((nq, blk, 2 * MLA_V), F32)],
    )
    return pl.pallas_call(
        functools.partial(_mla_attn_kernel, blk=blk, nq=nq, n_items=len(pairs)),
        out_shape=jax.ShapeDtypeStruct((m, MLA_WIDTH), BF16),
        grid_spec=grid_spec,
        compiler_params=_cparams("parallel", "parallel"),
        name="mla_attention",
    )(qi_tab, kj_tab, q_full, kv, k_rope)


GLA_HEADS_PER_STEP = 1


def _gla_kernel(q_ref, k_ref, v_ref, og_ref, lr_ref, wg_ref, bg_ref, gn_ref, o_ref, st_ref, *, sb, c):
    @pl.when(pl.program_id(2) == 0)
    def _():
        st_ref[...] = jnp.zeros_like(st_ref)

    z = _dot(lr_ref[...], wg_ref[...]) + bg_ref[...]
    log_a = (jnp.minimum(z, 0.0) - jnp.log(1.0 + jnp.exp(-jnp.abs(z)))) * (1.0 / GLA_TAU)
    rows = lax.broadcasted_iota(I32, (c, c), 0)
    cols = lax.broadcasted_iota(I32, (c, c), 1)
    causal = cols <= rows
    tril = causal.astype(F32)
    gn = gn_ref[...]
    scale = GLA_HEAD_K ** -0.5

    for n in range(sb // c):
        sl = slice(n * c, (n + 1) * c)
        for hh in range(GLA_HEADS_PER_STEP):
            kcols = slice(hh * GLA_HEAD_K, (hh + 1) * GLA_HEAD_K)
            vcols = slice(hh * GLA_HEAD_V, (hh + 1) * GLA_HEAD_V)
            g = log_a[sl, kcols]
            b = jnp.dot(tril, g, preferred_element_type=F32, precision=lax.Precision.HIGHEST)
            b_last = b[c - 1:c, :]
            q = q_ref[sl, kcols].astype(F32)
            k = k_ref[sl, kcols].astype(F32)
            v = v_ref[sl, vcols]
            qe = (q * scale * jnp.exp(b)).astype(BF16)
            ke = (k * jnp.exp(-b)).astype(BF16)
            kd = (k * jnp.exp(b_last - b)).astype(BF16)
            att = jnp.where(causal, _dot_nt(qe, ke), 0.0).astype(BF16)
            st = st_ref[hh]
            o = _dot(att, v) + _dot_nt(qe, st.astype(BF16))
            st_ref[hh] = st * jnp.exp(b_last) + _dot_tn(v, kd)
            og = og_ref[sl, vcols].astype(F32)
            o = _rms(o, gn) * (og * (1.0 / (1.0 + jnp.exp(-og))))
            o_ref[sl, vcols] = o.astype(o_ref.dtype)


def gla_mixer(h_in, w_gate_pad, b_gate, gla_out_norm, *, batch, seq, sb):
    sb = min(sb, seq)
    nsb = seq // sb
    m = batch * seq
    g = GLA_HEADS_PER_STEP
    kw, vw = g * GLA_HEAD_K, g * GLA_HEAD_V
    return pl.pallas_call(
        functools.partial(_gla_kernel, sb=sb, c=GLA_CHUNK),
        out_shape=jax.ShapeDtypeStruct((m, GLA_V_WIDTH), BF16),
        grid=(batch, GLA_HEADS // g, nsb),
        in_specs=[pl.BlockSpec((sb, kw), lambda b, h, s: (b * nsb + s, COL_GQ // kw + h)),
                  pl.BlockSpec((sb, kw), lambda b, h, s: (b * nsb + s, COL_GK // kw + h)),
                  pl.BlockSpec((sb, vw), lambda b, h, s: (b * nsb + s, COL_GV // vw + h)),
                  pl.BlockSpec((sb, vw), lambda b, h, s: (b * nsb + s, COL_OG // vw + h)),
                  pl.BlockSpec((sb, LANES), lambda b, h, s: (b * nsb + s, COL_LR // LANES)),
                  pl.BlockSpec((LANES, kw), lambda b, h, s: (0, h)),
                  pl.BlockSpec((1, kw), lambda b, h, s: (0, h)),
                  pl.BlockSpec((1, GLA_HEAD_V), lambda b, h, s: (0, 0))],
        out_specs=pl.BlockSpec((sb, vw), lambda b, h, s: (b * nsb + s, h)),
        scratch_shapes=[pltpu.VMEM((g, GLA_HEAD_V, GLA_HEAD_K), F32)],
        compiler_params=_cparams("parallel", "parallel", "arbitrary"),
        name="gla_mixer",
    )(h_in, h_in, h_in, h_in, h_in, w_gate_pad, b_gate.reshape(1, -1).astype(F32),
      gla_out_norm.reshape(1, -1).astype(F32))


def _out_proj_kernel(om_ref, g_ref, og_ref, w_ref, x_ref, o_ref, mix_ref):
    @pl.when(pl.program_id(1) == 0)
    def _():
        mix_ref[:, :MLA_WIDTH] = _rms(om_ref[...].astype(F32), g_ref[...]).astype(BF16)
        mix_ref[:, MLA_WIDTH:] = og_ref[...]

    o_ref[...] = x_ref[...] + _dot(mix_ref[...], w_ref[...])


def mixer_out_proj(o_mla, mla_out_norm, o_gla, w_out, x, *, tm, tn):
    m, d = x.shape
    tm = min(tm, m)
    tn = min(tn, d)
    kmix = MLA_WIDTH + GLA_V_WIDTH
    return pl.pallas_call(
        _out_proj_kernel,
        out_shape=jax.ShapeDtypeStruct((m, d), F32),
        grid=(m // tm, d // tn),
        in_specs=[pl.BlockSpec((tm, MLA_WIDTH), lambda i, j: (i, 0)),
                  pl.BlockSpec((1, MLA_WIDTH), lambda i, j: (0, 0)),
                  pl.BlockSpec((tm, GLA_V_WIDTH), lambda i, j: (i, 0)),
                  pl.BlockSpec((kmix, tn), lambda i, j: (0, j)),
                  pl.BlockSpec((tm, tn), lambda i, j: (i, j))],
        out_specs=pl.BlockSpec((tm, tn), lambda i, j: (i, j)),
        scratch_shapes=[pltpu.VMEM((tm, kmix), BF16)],
        compiler_params=_cparams("parallel", "arbitrary"),
        name="mixer_out_proj",
    )(o_mla, mla_out_norm.reshape(1, -1).astype(F32), o_gla, w_out, x)


def _xattn_kernel(q_ref, kv_ref, w_ref, h_ref, o_ref):
    scale = XATTN_DIM ** -0.5
    outs = []
    for h in range(XATTN_HEADS):
        q = q_ref[:, h * XATTN_DIM:(h + 1) * XATTN_DIM]
        k = kv_ref[:, h * XATTN_DIM:(h + 1) * XATTN_DIM]
        v = kv_ref[:, XATTN_WIDTH + h * XATTN_DIM: XATTN_WIDTH + (h + 1) * XATTN_DIM]
        s = _dot_nt(q, k) * scale
        p = jnp.exp(s - jnp.max(s, axis=1, keepdims=True))
        p = p / jnp.sum(p, axis=1, keepdims=True)
        outs.append(_dot(p.astype(BF16), v).astype(BF16))
    o = jnp.concatenate(outs, axis=1)
    o_ref[...] = h_ref[...] + _dot(o, w_ref[...])


def cross_attention(qx, kvm, w_co, h1, *, batch, seq, mem_tokens, tm):
    m, d = h1.shape
    tm = min(tm, seq)
    per_b = seq // tm
    return pl.pallas_call(
        _xattn_kernel,
        out_shape=jax.ShapeDtypeStruct((m, d), F32),
        grid=(m // tm,),
        in_specs=[pl.BlockSpec((tm, XATTN_WIDTH), lambda i: (i, 0)),
                  pl.BlockSpec((mem_tokens, 2 * XATTN_WIDTH), lambda i: (i // per_b, 0)),
                  pl.BlockSpec(w_co.shape, lambda i: (0, 0), pipeline_mode=pl.Buffered(1)),
                  pl.BlockSpec((tm, d), lambda i: (i, 0))],
        out_specs=pl.BlockSpec((tm, d), lambda i: (i, 0)),
        compiler_params=_cparams("parallel"),
        name="cross_attention",
    )(qx, kvm, w_co, h1)


def _order_key(x):
    b = lax.bitcast_convert_type(x, I32)
    return b ^ (lax.shift_right_arithmetic(b, 31) & 0x7FFFFFFF)


def _order_key_inv(k):
    return lax.bitcast_convert_type(k ^ (lax.shift_right_arithmetic(k, 31) & 0x7FFFFFFF), F32)


KEY_BIAS = 0x20000000


def _pack_keys(x, bits):
    low = (1 << bits) - 1
    row = lax.broadcasted_iota(I32, x.shape, 0)
    k = (_order_key(x) & ~((low << 2) | 3)) | ((low - row) << 2)
    return lax.bitcast_convert_type(lax.shift_right_arithmetic(k, 2) + KEY_BIAS, F32)


def _unpack_keys(kf, bits):
    low = (1 << bits) - 1
    k = lax.shift_left(lax.bitcast_convert_type(kf, I32) - KEY_BIAS, 2)
    return _order_key_inv(k & ~((low << 2) | 3)), low - (lax.shift_right_arithmetic(k, 2) & low)


SUBLANES = 8


def _max_rows(k3):
    m8 = jnp.max(k3, axis=0)
    for shift in (4, 2, 1):
        m8 = jnp.maximum(m8, pltpu.roll(m8, shift, axis=0))
    return m8


def _top16_rows(s, bits):
    rows, t = s.shape
    k3 = _pack_keys(s, bits).reshape(rows // SUBLANES, SUBLANES, t)
    vals, idxs = [], []
    for _ in range(PEER_TOPK):
        m8 = _max_rows(k3)
        val, idx = _unpack_keys(m8[0:1, :], bits)
        vals.append(val)
        idxs.append(idx)
        k3 = jnp.where(k3 == m8[None], 0.0, k3)
    return vals, idxs


_PAIRS = [(a, b) for a in range(PEER_TOPK) for b in range(PEER_TOPK) if (a + 1) * (b + 1) <= PEER_TOPK]
_PAIR_ROWS = -(-len(_PAIRS) // 8) * 8


TOPK_HEADS_PER_TRIP = 8


def _peer_topk_kernel(q_ref, keys_ref, eid_ref, gate_ref, eid_t, gate_t, *, tm):
    def head(h):
        tops = []
        for p in range(2):
            col = pl.multiple_of((2 * h + p) * PEER_HALF, PEER_HALF)
            qh = q_ref[:, pl.ds(col, PEER_HALF)]
            s = _dot_nt(keys_ref[2 * h + p], qh)
            tops.append(_top16_rows(s, 7))
        (s1, i1), (s2, i2) = tops
        cand_s = [s1[a] + s2[b] for a, b in _PAIRS]
        cand_e = [i1[a] * PEER_KEYS + i2[b] for a, b in _PAIRS]
        pad = _PAIR_ROWS - len(_PAIRS)
        cs = jnp.concatenate(cand_s + [jnp.full((pad, tm), NEG, F32)], axis=0)
        ce = jnp.concatenate(cand_e + [jnp.zeros((pad, tm), I32)], axis=0)
        k3 = _pack_keys(cs, 6).reshape(_PAIR_ROWS // SUBLANES, SUBLANES, tm)
        ce3 = ce.astype(F32).reshape(_PAIR_ROWS // SUBLANES, SUBLANES, tm)
        sel_s, sel_e = [], []
        for _ in range(PEER_TOPK):
            m8 = _max_rows(k3)
            hit = k3 == m8[None]
            sel_s.append(_unpack_keys(m8[0:1, :], 6)[0])
            sel_e.append(_max_rows(jnp.where(hit, ce3, -1.0))[0:1, :].astype(I32))
            k3 = jnp.where(hit, 0.0, k3)
        top_s = jnp.concatenate(sel_s, axis=0)
        top_e = jnp.concatenate(sel_e, axis=0)
        w = jnp.exp(top_s - top_s[0:1, :])
        gate = w / jnp.sum(w, axis=0, keepdims=True)
        row = pl.multiple_of(h * PEER_TOPK, PEER_TOPK)
        eid_t[pl.ds(row, PEER_TOPK), :] = top_e
        gate_t[pl.ds(row, PEER_TOPK), :] = gate

    def trip(i, carry):
        for u in range(TOPK_HEADS_PER_TRIP):
            head(i * TOPK_HEADS_PER_TRIP + u)
        return carry

    lax.fori_loop(0, PEER_HEADS // TOPK_HEADS_PER_TRIP, trip, 0)
    eid_ref[...] = eid_t[...].T
    gate_ref[...] = gate_t[...].T


def peer_topk(qp, keys, *, tm):
    m = qp.shape[0]
    tm = min(tm, m)
    return pl.pallas_call(
        functools.partial(_peer_topk_kernel, tm=tm),
        out_shape=[jax.ShapeDtypeStruct((m, PEER_SLOTS), I32), jax.ShapeDtypeStruct((m, PEER_SLOTS), F32),
                   jax.ShapeDtypeStruct((PEER_SLOTS, m), I32)],
        grid=(m // tm,),
        in_specs=[pl.BlockSpec((tm, qp.shape[1]), lambda i: (i, 0)),
                  pl.BlockSpec(keys.shape, lambda i: (0, 0, 0))],
        out_specs=[pl.BlockSpec((tm, PEER_SLOTS), lambda i: (i, 0)),
                   pl.BlockSpec((tm, PEER_SLOTS), lambda i: (i, 0)),
                   pl.BlockSpec((PEER_SLOTS, tm), lambda i: (0, i))],
        scratch_shapes=[pltpu.VMEM((PEER_SLOTS, tm), F32)],
        compiler_params=_cparams("parallel"),
        name="peer_topk",
    )(qp, keys)


def _peer_gates_kernel(eid_ref, eidt_ref, gate_ref, o_ref, g3_ref, *, tm, pitch):
    sub = lax.broadcasted_iota(I32, (PEER_KEYS, PEER_SLOTS), 0)
    lane = lax.broadcasted_iota(I32, (PEER_SLOTS, PEER_KEYS), 1).astype(F32).astype(BF16)
    lo_all = jnp.bitwise_and(eidt_ref[...], PEER_KEYS - 1).astype(F32).astype(BF16)
    one = jnp.ones((), BF16)
    zero = jnp.zeros((), BF16)
    for t in range(tm):
        e = eid_ref[t:t + 1, :]
        hi = lax.shift_right_logical(e, 7)
        a1 = jnp.where(sub == hi, gate_ref[t:t + 1, :], 0.0).astype(BF16)
        a2 = jnp.where(lane == lo_all[:, t:t + 1], one, zero)
        g3_ref[pl.ds(t, PEER_KEYS, stride=pitch), :] = _dot(a1, a2)
    for e1 in range(PEER_KEYS):
        o_ref[:, e1 * PEER_KEYS:(e1 + 1) * PEER_KEYS] = g3_ref[e1 * pitch:e1 * pitch + tm, :].astype(o_ref.dtype)


def peer_gates(eid, eid_t, gate, *, tm):
    m = eid.shape[0]
    tm = min(tm, m)
    pitch = tm + 8
    return pl.pallas_call(
        functools.partial(_peer_gates_kernel, tm=tm, pitch=pitch),
        out_shape=jax.ShapeDtypeStruct((m, PEER_EXPERTS), BF16),
        grid=(m // tm,),
        in_specs=[pl.BlockSpec((tm, PEER_SLOTS), lambda i: (i, 0)),
                  pl.BlockSpec((PEER_SLOTS, tm), lambda i: (0, i)),
                  pl.BlockSpec((tm, PEER_SLOTS), lambda i: (i, 0))],
        out_specs=pl.BlockSpec((tm, PEER_EXPERTS), lambda i: (i, 0)),
        scratch_shapes=[pltpu.VMEM((PEER_KEYS * pitch, PEER_KEYS), F32)],
        compiler_params=_cparams("parallel"),
        name="peer_gates",
    )(eid, eid_t, gate)


def _peer_dense_kernel(x_ref, sx_ref, u_ref, su_ref, g_ref, v_ref, sv_ref, h_ref, gn_ref, o_ref, aq_scr, sa_scr,
                       *, final_norm, n_blocks):
    j = pl.program_id(1)
    nj = n_blocks

    def step(p, do_values, do_gates):
        if do_gates:
            s = _dot_nt(x_ref[...], u_ref[...])
        if do_values:
            pv = _dot(aq_scr[1 - p], v_ref[...])
        if do_gates:
            s = s * sx_ref[...] * su_ref[...]
            a = 0.5 * s * (1.0 + lax.erf(s * (2.0 ** -0.5)))
            aq_scr[p], sa_scr[p] = _quantize_rows(a * g_ref[...].astype(F32) * sv_ref[...])
        if do_values:
            o_ref[...] += sa_scr[1 - p] * pv

    half = o_ref.shape[1] // 2

    @pl.when(j == 0)
    def _():
        o_ref[:, :half] = h_ref[...]
        o_ref[:, half:] = jnp.zeros((o_ref.shape[0], half), F32)
        step(0, False, True)

    @pl.when(j == 1)
    def _():
        o_ref[:, half:] += h_ref[...]

    for p in range(2):
        @pl.when(jnp.logical_and(jnp.logical_and(j > 0, j < nj), j % 2 == p))
        def _(p=p):
            step(p, True, True)

    @pl.when(j == nj)
    def _():
        step(nj % 2, True, False)
        if final_norm:
            o_ref[...] = _rms(o_ref[...], gn_ref[...])


def peer_dense(xq, sx, uq, su, g, vq, sv, h, gain, *, final_norm, tm, te):
    m, d = xq.shape
    e = uq.shape[0]
    tm = min(tm, m)
    nj = e // te
    def cur(j):
        return jnp.minimum(j, nj - 1)

    def prev(j):
        return jnp.maximum(j - 1, 0)

    return pl.pallas_call(
        functools.partial(_peer_dense_kernel, final_norm=final_norm, n_blocks=nj),
        out_shape=jax.ShapeDtypeStruct((m, d), F32),
        grid=(m // tm, nj + 1),
        in_specs=[pl.BlockSpec((tm, d), lambda i, j: (i, 0)),
                  pl.BlockSpec((tm, 1), lambda i, j: (i, 0)),
                  pl.BlockSpec((te, d), lambda i, j: (cur(j), 0)),
                  pl.BlockSpec((1, te), lambda i, j: (0, cur(j))),
                  pl.BlockSpec((tm, te), lambda i, j: (i, cur(j))),
                  pl.BlockSpec((te, d), lambda i, j: (prev(j), 0)),
                  pl.BlockSpec((1, te), lambda i, j: (0, cur(j))),
                  pl.BlockSpec((tm, d // 2), lambda i, j: (i, jnp.minimum(j, 1))),
                  pl.BlockSpec((1, d), lambda i, j: (0, 0))],
        out_specs=pl.BlockSpec((tm, d), lambda i, j: (i, 0)),
        scratch_shapes=[pltpu.VMEM((2, tm, te), F8), pltpu.VMEM((2, tm, 1), F32)],
        compiler_params=_cparams("parallel", "arbitrary"),
        name="peer_dense",
    )(xq, sx, uq, su.reshape(1, e), g, vq, sv.reshape(1, e), h, gain.reshape(1, d).astype(F32))


def _w_in_columns():
    src = {}
    off = 0
    for name, width in (("cq", MLA_Q_LORA), ("ckv", MLA_KV_LORA), ("kr", MLA_ROPE), ("gq", GLA_K_WIDTH),
                        ("gk", GLA_K_WIDTH), ("gv", GLA_V_WIDTH), ("lr", GLA_RANK), ("og", GLA_V_WIDTH)):
        src[name] = np.arange(off, off + width)
        off += width
    cols = np.full((IN_PAD,), -1, np.int64)
    for name, start in (("cq", COL_CQ), ("ckv", COL_CKV), ("gq", COL_GQ), ("gk", COL_GK),
                        ("gv", COL_GV), ("og", COL_OG), ("lr", COL_LR)):
        cols[start:start + len(src[name])] = src[name]
    half = MLA_ROPE // 2
    x1, x2 = src["kr"][:half], src["kr"][half:]
    cols[COL_KR:COL_KR + LANES] = np.concatenate([x1, x1, x2, x2])
    return cols


def _w_uq_columns():
    half = MLA_ROPE // 2
    nope = [h * MLA_QK + d for h in range(MLA_HEADS) for d in range(MLA_NOPE)]
    rope = []
    for j in range(MLA_HEADS // 2):
        a, b = 2 * j, 2 * j + 1
        for part in (0, 1):
            for h in (a, b):
                rope += [h * MLA_QK + MLA_NOPE + part * half + r for r in range(half)]
    return np.array(nope + rope, np.int64)


def _column_runs(cols):
    runs = []
    start = 0
    for i in range(1, len(cols) + 1):
        if i == len(cols) or cols[i] != (cols[i - 1] + 1 if cols[i - 1] >= 0 else -1):
            runs.append((int(cols[start]), i - start))
            start = i
    return runs


def _take_cols(w, cols, dtype):
    pieces = [jnp.zeros((w.shape[0], n), dtype) if src < 0 else w[:, src:src + n].astype(dtype)
              for src, n in _column_runs(cols)]
    return jnp.concatenate(pieces, axis=1)


def _take_rows(wt, rows, dtype):
    pieces = [jnp.zeros((n, wt.shape[1]), dtype) if src < 0 else wt[src:src + n].astype(dtype)
              for src, n in _column_runs(rows)]
    return jnp.concatenate(pieces, axis=0)


def _rope_tables(positions):
    half = MLA_ROPE // 2
    inv_freq = ROPE_THETA ** (-jnp.arange(0, MLA_ROPE, 2, dtype=F32) / MLA_ROPE)
    ang = positions.astype(F32).reshape(-1, 1) * inv_freq
    cos = jnp.cos(ang)
    sin = jnp.sin(ang)
    return jnp.tile(cos, (1, 4)), jnp.concatenate([-sin, -sin, sin, sin], axis=1)


def kernel(x, mem, positions, norm_mem, norm_mix, w_in, mla_q_norm, w_uq, mla_kv_norm, w_ukv, mla_out_norm,
           w_gate_up, b_gate, gla_out_norm, w_out, norm_cross, w_cq, w_ck, w_cv, w_co, norm_ffn, w_peer_q,
           peer_sub_keys, peer_u, peer_v, norm_final):
    batch, seq, d = x.shape
    mem_tokens = mem.shape[1]
    m = batch * seq
    h = x.reshape(m, d)
    cos_t, sin_t = _rope_tables(positions)

    t = TILES
    depth = norm_mix.shape[0]
    for l in range(depth):
        w_in_t = _take_rows(jnp.transpose(w_in[l]), _w_in_columns(), BF16)
        h_in = norm_matmul(h, norm_mix[l], w_in_t, k=d, tm=t.rows, tn=t.cols, out_dtype=BF16, w_transposed=True,
                           name="in_proj")

        w_uq_p = _take_cols(w_uq[l], _w_uq_columns(), BF16)
        q_full = mla_q_proj(h_in, mla_q_norm[l], w_uq_p, cos_t, sin_t, tm=t.rows)
        kv, k_rope = mla_kv_proj(h_in, mla_kv_norm[l], w_ukv[l].astype(BF16), cos_t, sin_t, tm=t.rows)
        o_mla = mla_attention(q_full, kv, k_rope, batch=batch, seq=seq, blk=t.attn_block)

        w_gate_pad = jnp.zeros((LANES, GLA_K_WIDTH), F32).at[:GLA_RANK].set(w_gate_up[l]).astype(BF16)
        o_gla = gla_mixer(h_in, w_gate_pad, b_gate[l], gla_out_norm[l], batch=batch, seq=seq, sb=t.gla_rows)

        h = mixer_out_proj(o_mla, mla_out_norm[l], o_gla, w_out[l].astype(BF16), h, tm=t.rows, tn=t.cols)

        mn_kv = norm_matmul(mem.reshape(batch * mem_tokens, d), norm_mem,
                            jnp.concatenate([w_ck[l], w_cv[l]], axis=1).astype(BF16),
                            k=d, tm=t.rows, tn=t.cols, out_dtype=BF16, name="mem_kv_proj")
        qx = norm_matmul(h, norm_cross[l], w_cq[l].astype(BF16), k=d, tm=t.rows, tn=XATTN_WIDTH, out_dtype=BF16,
                         name="xattn_q_proj")
        h = cross_attention(qx, mn_kv, w_co[l].astype(BF16), h, batch=batch, seq=seq,
                            mem_tokens=mem_tokens, tm=t.rows)

        qp, xq, sx = norm_matmul(h, norm_ffn[l], w_peer_q[l].astype(BF16), k=d, tm=t.peer_q_rows,
                                 tn=PEER_HEADS * 2 * PEER_HALF, out_dtype=BF16, emit_xq=True, name="peer_q_proj")
        keys = peer_sub_keys[l].reshape(PEER_HEADS * 2, PEER_KEYS, PEER_HALF).astype(BF16)
        eid, gate, eid_t = peer_topk(qp, keys, tm=t.topk_tokens)
        g = peer_gates(eid, eid_t, gate, tm=t.gate_tokens)
        uq, su = quantize_rows(peer_u[l], tr=t.rows, name="peer_u_quant")
        vq, sv = quantize_rows(peer_v[l], tr=t.rows, name="peer_v_quant")
        h = peer_dense(xq, sx, uq, su, g, vq, sv, h, norm_final, final_norm=(l + 1 == depth),
                       tm=t.rows, te=t.peer_experts)
    return h.reshape(batch, seq, d)
```
